```python
import jax, jax.numpy as jnp
from jax import lax
import numpy as np

D_MODEL = 2048
BATCH = 8
SEQ = 4096
DEPTH = 2
DEC_BATCH = 16
DEC_SEQ = 64
PAST_LEN = 4096

CHUNK = 64
N_HEADS = 8
HEAD_DIM = 128
N_KV_HEADS = 2
GROUP = N_HEADS // N_KV_HEADS
ATTN_W = N_HEADS * HEAD_DIM
KV_W = N_KV_HEADS * HEAD_DIM
N_IDX_HEADS = 16
IDX_DIM = 64
IDX_Q_W = N_IDX_HEADS * IDX_DIM
MAX_TOPK = 256
Q_BLOCK = 128
CONV_W = 512
CONV_K = 3
POOL_W = 512
POOL_WINDOWS = (2, 4, 8, 16)
N_POOL_GROUPS = 4
POOL_GROUP = POOL_W // N_POOL_GROUPS
POOL_HIST = 15
ROPE_THETA = 10000.0
EPS = 1e-6

SPLIT_SIZES = (ATTN_W, KV_W, KV_W, IDX_Q_W, IDX_DIM, N_IDX_HEADS, ATTN_W,
               CONV_W, CONV_W, CONV_W, CONV_W,
               POOL_W, POOL_W,
               D_MODEL, D_MODEL, D_MODEL)
IN_W = 2 * ATTN_W + 2 * KV_W + IDX_Q_W + IDX_DIM + N_IDX_HEADS + 4 * CONV_W + 2 * POOL_W + 3 * D_MODEL

kernel_name = "hybrid_dsa_conv_pool_stream_step"


def split_cols(p):
    offs = np.cumsum(np.array(SPLIT_SIZES))[:-1].tolist()
    return jnp.split(p, offs, axis=-1)


def rms_norm(x, g):
    xf = x.astype(jnp.float32)
    y = xf * lax.rsqrt(jnp.mean(xf * xf, axis=-1, keepdims=True) + EPS)
    return (y * g.astype(jnp.float32)).astype(x.dtype)


def rope(x, pos):
    half = x.shape[-1] // 2
    inv = ROPE_THETA ** (-jnp.arange(half, dtype=jnp.float32) / half)
    ang = pos.astype(jnp.float32)[:, None] * inv[None, :]
    cos = jnp.cos(ang)[None, :, None, :]
    sin = jnp.sin(ang)[None, :, None, :]
    x1 = x[..., :half].astype(jnp.float32)
    x2 = x[..., half:].astype(jnp.float32)
    return jnp.concatenate([x1 * cos - x2 * sin, x2 * cos + x1 * sin], axis=-1).astype(x.dtype)


def sparse_attend(q, iq, iw, q_pos, k_all, v_all, ik_all, topk):
    B, Tq = q.shape[:2]
    L = k_all.shape[1]
    dots = jnp.einsum('bthd,bsd->bths', iq.astype(jnp.float32), ik_all.astype(jnp.float32))
    score = jnp.einsum('bths,bth->bts', jax.nn.relu(dots), iw.astype(jnp.float32))
    q_chunk = q_pos // CHUNK
    admissible = (jnp.arange(L) // CHUNK)[None, :] <= q_chunk[:, None]
    score = jnp.where(admissible[None], score, -jnp.inf)
    _, idx = lax.top_k(score, topk)
    valid = (idx // CHUNK) <= q_chunk[None, :, None]
    k_sel = jax.vmap(lambda kb, ib: kb[ib])(k_all, idx)
    v_sel = jax.vmap(lambda vb, ib: vb[ib])(v_all, idx)
    qg = q.reshape(B, Tq, N_KV_HEADS, GROUP, HEAD_DIM).astype(jnp.float32)
    logits = jnp.einsum('btkgd,btnkd->btkgn', qg, k_sel.astype(jnp.float32)) * (HEAD_DIM ** -0.5)
    logits = jnp.where(valid[:, :, None, None, :], logits, -jnp.inf)
    p = jax.nn.softmax(logits, axis=-1)
    o = jnp.einsum('btkgn,btnkd->btkgd', p, v_sel.astype(jnp.float32))
    return o.reshape(B, Tq, ATTN_W).astype(q.dtype)


def prompt_attention(q, iq, iw, pos, k, v, ik, topk):
    B, T = q.shape[:2]
    nb = T // Q_BLOCK

    def blocks(a):
        return jnp.moveaxis(a.reshape((B, nb, Q_BLOCK) + a.shape[2:]), 1, 0)

    def one(args):
        qb, iqb, iwb, pb = args
        return sparse_attend(qb, iqb, iwb, pb, k, v, ik, topk)

    o = lax.map(one, (blocks(q), blocks(iq), blocks(iw), pos.reshape(nb, Q_BLOCK)))
    return jnp.moveaxis(o, 0, 1).reshape(B, T, ATTN_W)


def mixer_layer(x, pos, past, norm_g, w_in, conv_w, conv_b, pool_w, pool_scale,
                lift_a, lift_b, lift_c, w_out, topk):
    B, T, _ = x.shape
    h = rms_norm(x, norm_g)
    (q, k, v, iq, ik, iw, gate_a, u, b_gate, c_gate, gate_b, p_in, gate_c,
     m_a, m_b, m_c) = split_cols(h @ w_in)
    q = rope(q.reshape(B, T, N_HEADS, HEAD_DIM), pos)
    k = rope(k.reshape(B, T, N_KV_HEADS, HEAD_DIM), pos)
    v = v.reshape(B, T, N_KV_HEADS, HEAD_DIM)
    iq = rope(iq.reshape(B, T, N_IDX_HEADS, IDX_DIM), pos)
    ik = rope(ik.reshape(B, T, 1, IDX_DIM), pos)[:, :, 0]
    iw = iw * ((IDX_DIM ** -0.5) * (N_IDX_HEADS ** -0.5))
    if past is None:
        conv_hist = jnp.zeros((B, CONV_K - 1, CONV_W), x.dtype)
        pool_hist = jnp.zeros((B, POOL_HIST, POOL_W), x.dtype)
        attn = prompt_attention(q, iq, iw, pos, k, v, ik, topk)
    else:
        ck, cv, cik, conv_hist, pool_hist = past
        k_all = jnp.concatenate([ck.astype(k.dtype), k], axis=1)
        v_all = jnp.concatenate([cv.astype(v.dtype), v], axis=1)
        ik_all = jnp.concatenate([cik.astype(ik.dtype), ik], axis=1)
        attn = sparse_attend(q, iq, iw, pos, k_all, v_all, ik_all, topk)
    y_a = attn * jax.nn.silu(gate_a)
    cin = c_gate * u
    xc = jnp.concatenate([conv_hist.astype(cin.dtype), cin], axis=1)
    conv = (xc[:, 0:T] * conv_w[0] + xc[:, 1:T + 1] * conv_w[1]
            + xc[:, 2:T + 2] * conv_w[2] + conv_b)
    y_b = b_gate * conv * jax.nn.silu(gate_b)
    xp = jnp.concatenate([pool_hist.astype(p_in.dtype), p_in], axis=1)
    cs = jnp.concatenate([jnp.zeros((B, 1, POOL_W), jnp.float32),
                          jnp.cumsum(xp.astype(jnp.float32), axis=1)], axis=1)
    end = cs[:, POOL_HIST + 1:]
    means = []
    for g, w in enumerate(POOL_WINDOWS):
        sl = slice(g * POOL_GROUP, (g + 1) * POOL_GROUP)
        start = cs[:, POOL_HIST + 1 - w:POOL_HIST + 1 - w + T, sl]
        cnt = jnp.minimum(w, pos + 1).astype(jnp.float32)[None, :, None]
        means.append((end[..., sl] - start) / cnt)
    pooled = jnp.concatenate(means, axis=-1)
    d = (pooled - p_in.astype(jnp.float32)).reshape(B, T, N_POOL_GROUPS, POOL_GROUP)
    mixed = jnp.einsum('btgc,gcd->btgd', d, pool_w.astype(jnp.float32)).reshape(B, T, POOL_W)
    y_c = (mixed * pool_scale.astype(jnp.float32)).astype(x.dtype) * jax.nn.silu(gate_c)
    z = (jax.nn.sigmoid(m_a) * (y_a @ lift_a) + jax.nn.sigmoid(m_b) * (y_b @ lift_b)
         + jax.nn.sigmoid(m_c) * (y_c @ lift_c))
    out = x + z @ w_out
    new_state = (k, v, ik, xc[:, -(CONV_K - 1):], xp[:, -POOL_HIST:])
    return out, new_state


def setup_inputs(seed: int = 0) -> dict:
    key = jax.random.key(seed)
    ks = jax.random.split(key, 20)
    f = jnp.float32
    nrm = lambda k, s, sc: jax.random.normal(k, s, f) * sc
    return {
        "x_prompt": nrm(ks[0], (BATCH, SEQ, D_MODEL), 1.0),
        "x_sample": nrm(ks[1], (DEC_BATCH, DEC_SEQ, D_MODEL), 1.0),
        "cache_k": nrm(ks[2], (DEPTH, DEC_BATCH, PAST_LEN, N_KV_HEADS, HEAD_DIM), 1.0),
        "cache_v": nrm(ks[3], (DEPTH, DEC_BATCH, PAST_LEN, N_KV_HEADS, HEAD_DIM), 1.0),
        "cache_idx_k": nrm(ks[4], (DEPTH, DEC_BATCH, PAST_LEN, IDX_DIM), 1.0),
        "state_conv": nrm(ks[5], (DEPTH, DEC_BATCH, CONV_K - 1, CONV_W), 1.0),
        "state_pool": nrm(ks[6], (DEPTH, DEC_BATCH, POOL_HIST, POOL_W), 1.0),
        "norm_g": 1.0 + nrm(ks[7], (DEPTH, D_MODEL), 0.01),
        "w_in": nrm(ks[8], (DEPTH, D_MODEL, IN_W), D_MODEL ** -0.5),
        "conv_w": nrm(ks[9], (DEPTH, CONV_K, CONV_W), CONV_K ** -0.5),
        "conv_b": nrm(ks[10], (DEPTH, CONV_W), 0.01),
        "pool_w": nrm(ks[11], (DEPTH, N_POOL_GROUPS, POOL_GROUP, POOL_GROUP), POOL_GROUP ** -0.5),
        "pool_scale": 1.0 + nrm(ks[12], (DEPTH, POOL_W), 0.1),
        "lift_a": nrm(ks[13], (DEPTH, ATTN_W, D_MODEL), ATTN_W ** -0.5),
        "lift_b": nrm(ks[14], (DEPTH, CONV_W, D_MODEL), CONV_W ** -0.5),
        "lift_c": nrm(ks[15], (DEPTH, POOL_W, D_MODEL), POOL_W ** -0.5),
        "w_out": nrm(ks[16], (DEPTH, D_MODEL, D_MODEL), D_MODEL ** -0.5),
        "final_norm_g": 1.0 + nrm(ks[17], (D_MODEL,), 0.01),
    }


def reference(x_prompt, x_sample, cache_k, cache_v, cache_idx_k, state_conv, state_pool,
              norm_g, w_in, conv_w, conv_b, pool_w, pool_scale, lift_a, lift_b, lift_c,
              w_out, final_norm_g):
    seq = x_prompt.shape[1]
    past_len = cache_k.shape[2]
    dec_seq = x_sample.shape[1]
    topk_prompt = min(MAX_TOPK, seq // 4)
    topk_sample = min(MAX_TOPK, (past_len + dec_seq) // 4)
    pos_prompt = jnp.arange(seq, dtype=jnp.int32)
    pos_sample = past_len + jnp.arange(dec_seq, dtype=jnp.int32)

    hp, hs = x_prompt, x_sample
    p_states = ([], [], [], [], [])
    s_states = ([], [], [], [], [])
    for l in range(DEPTH):
        params = (norm_g[l], w_in[l], conv_w[l], conv_b[l], pool_w[l], pool_scale[l],
                  lift_a[l], lift_b[l], lift_c[l], w_out[l])
        hp, st_p = mixer_layer(hp, pos_prompt, None, *params, topk_prompt)
        past = (cache_k[l], cache_v[l], cache_idx_k[l], state_conv[l], state_pool[l])
        hs, st_s = mixer_layer(hs, pos_sample, past, *params, topk_sample)
        for i in range(5):
            p_states[i].append(st_p[i])
            s_states[i].append(st_s[i])
    y_prompt = rms_norm(hp, final_norm_g)
    y_sample = rms_norm(hs, final_norm_g)
    k_prompt = jnp.stack(p_states[0])
    v_prompt = jnp.stack(p_states[1])
    idxk_prompt = jnp.stack(p_states[2])
    conv_prompt = jnp.stack(p_states[3])
    pool_prompt = jnp.stack(p_states[4])
    k_sample = jnp.stack(s_states[0])
    v_sample = jnp.stack(s_states[1])
    idxk_sample = jnp.stack(s_states[2])
    conv_sample = jnp.stack(s_states[3])
    pool_sample = jnp.stack(s_states[4])
    return (y_prompt, y_sample, k_prompt, v_prompt, idxk_prompt, conv_prompt, pool_prompt,
            k_sample, v_sample, idxk_sample, conv_sample, pool_sample)
```

```python
import functools

import jax
import jax.numpy as jnp
from jax import lax
from jax.experimental import pallas as pl
from jax.experimental.pallas import tpu as pltpu

D_MODEL = 2048
CHUNK = 64
N_HEADS = 8
HEAD_DIM = 128
N_KV_HEADS = 2
GROUP = N_HEADS // N_KV_HEADS
ATTN_W = N_HEADS * HEAD_DIM
KV_W = N_KV_HEADS * HEAD_DIM
N_IDX_HEADS = 16
IDX_DIM = 64
IDX_Q_W = N_IDX_HEADS * IDX_DIM
MAX_TOPK = 256
CONV_W = 512
CONV_K = 3
POOL_W = 512
POOL_WINDOWS = (2, 4, 8, 16)
POOL_GROUP = 128
POOL_HIST = 15
ROPE_THETA = 10000.0
EPS = 1e-6

LANES = 128
CONV_PAD = 8
POOL_PAD = 16
KEY_BLOCK = 512
Q_TILE = 128
NEG_BIAS = -1e30
F32_LOWEST = -3.0e38
VMEM_LIMIT = 56 * 1024 * 1024

F32 = jnp.float32
BF16 = jnp.bfloat16

_SIZES = (ATTN_W, KV_W, KV_W, IDX_Q_W, IDX_DIM, N_IDX_HEADS, ATTN_W,
          CONV_W, CONV_W, CONV_W, CONV_W, POOL_W, POOL_W, D_MODEL, D_MODEL, D_MODEL)
_OFFS = [0]
for _s in _SIZES:
    _OFFS.append(_OFFS[-1] + _s)
(O_Q, O_K, O_V, O_IQ, O_IK, O_IW, O_GA, O_U, O_BG, O_CG, O_GB, O_PIN, O_GC, O_MA, O_MB, O_MC, O_END) = _OFFS

A_Q, A_K, A_V, A_IQ, A_IKW, A_GA, A_END = 0, 1024, 1280, 1536, 2560, 2688, 3712
B_U, B_BG, B_CG, B_GB, B_PIN, B_GC, B_MB, B_MC, B_END = 0, 512, 1024, 1536, 2048, 2560, 3072, 5120, 7168


def _dot(a, b):
    return jnp.dot(a, b, preferred_element_type=F32)


def _dot_nt(a, b):
    return lax.dot_general(a, b, (((1,), (1,)), ((), ())), preferred_element_type=F32)


def _rms_h(x, g):
    h = x * lax.rsqrt(jnp.mean(x * x, axis=-1, keepdims=True) + EPS) * g
    return h.astype(BF16)


def _silu(x):
    return x * jax.nn.sigmoid(x)


def _const_spec(shape):
    nd = len(shape)
    return pl.BlockSpec(shape, lambda *_: (0,) * nd, pipeline_mode=pl.Buffered(1))


def _proj_a_kernel(x_ref, g_ref, w_ref, c128_ref, s128_ref, c64_ref, s64_ref,
                   q_ref, k_ref, v_ref, kb_ref, vb_ref, iq_ref, ik_ref, ik2_ref, iw_ref, ga_ref):
    h = _rms_h(x_ref[...], g_ref[...])
    c128 = c128_ref[...]
    s128 = s128_ref[...]
    c64 = c64_ref[...]
    s64 = s64_ref[...]
    lane = lax.broadcasted_iota(jnp.int32, (1, LANES), 1)
    first_half64 = (lane % IDX_DIM) < (IDX_DIM // 2)

    def rope128(y):
        return y * c128 + pltpu.roll(y, HEAD_DIM // 2, 1) * s128

    def rope64(y):
        partner = jnp.where(first_half64, pltpu.roll(y, LANES - IDX_DIM // 2, 1),
                            pltpu.roll(y, IDX_DIM // 2, 1))
        return y * c64 + partner * s64

    y = _dot(h, w_ref[:, A_Q:A_K])
    for hd in range(N_HEADS):
        sl = slice(hd * HEAD_DIM, (hd + 1) * HEAD_DIM)
        q_ref[:, sl] = (rope128(y[:, sl]) * (HEAD_DIM ** -0.5)).astype(BF16)

    y = _dot(h, w_ref[:, A_K:A_IQ])
    for hd in range(N_KV_HEADS):
        sl = slice(hd * HEAD_DIM, (hd + 1) * HEAD_DIM)
        kr = rope128(y[:, sl])
        k_ref[:, sl] = kr
        kb_ref[:, sl] = kr.astype(BF16)
    v = y[:, KV_W:]
    v_ref[...] = v
    vb_ref[...] = v.astype(BF16)

    y = _dot(h, w_ref[:, A_IQ:A_IKW])
    for c in range(IDX_Q_W // LANES):
        sl = slice(c * LANES, (c + 1) * LANES)
        iq_ref[:, sl] = rope64(y[:, sl]).astype(BF16)

    y = _dot(h, w_ref[:, A_IKW:A_GA])
    ikr = rope64(y)
    ik_ref[...] = ikr[:, :IDX_DIM]
    ikz = jnp.where(lane < IDX_DIM, ikr, 0.0)
    ik2_ref[:, :LANES] = ikz.astype(BF16)
    ik2_ref[:, LANES:] = pltpu.roll(ikz, IDX_DIM, 1).astype(BF16)
    iw_ref[...] = y[:, IDX_DIM:IDX_DIM + N_IDX_HEADS] * ((IDX_DIM ** -0.5) * (N_IDX_HEADS ** -0.5))

    ga_ref[...] = _silu(_dot(h, w_ref[:, A_GA:A_END])).astype(BF16)


def _proj_a(x, g, w_a, tabs, tm):
    m = x.shape[0]
    c128, s128, c64, s64 = tabs
    n_pt = c128.shape[0] // tm
    row = lambda w: pl.BlockSpec((tm, w), lambda i: (i, 0))
    tab = pl.BlockSpec((tm, LANES), lambda i: (i % n_pt, 0))
    out_shape = (
        jax.ShapeDtypeStruct((m, ATTN_W), BF16),
        jax.ShapeDtypeStruct((m, KV_W), F32),
        jax.ShapeDtypeStruct((m, KV_W), F32),
        jax.ShapeDtypeStruct((m, KV_W), BF16),
        jax.ShapeDtypeStruct((m, KV_W), BF16),
        jax.ShapeDtypeStruct((m, IDX_Q_W), BF16),
        jax.ShapeDtypeStruct((m, IDX_DIM), F32),
        jax.ShapeDtypeStruct((m, 2 * LANES), BF16),
        jax.ShapeDtypeStruct((m, N_IDX_HEADS), F32),
        jax.ShapeDtypeStruct((m, ATTN_W), BF16),
    )
    out_specs = (row(ATTN_W), row(KV_W), row(KV_W), row(KV_W), row(KV_W), row(IDX_Q_W),
                 row(IDX_DIM), row(2 * LANES), row(N_IDX_HEADS), row(ATTN_W))
    return pl.pallas_call(
        _proj_a_kernel,
        grid=(m // tm,),
        in_specs=[row(D_MODEL), _const_spec((1, D_MODEL)), _const_spec((D_MODEL, A_END)),
                  tab, tab, tab, tab],
        out_specs=out_specs,
        out_shape=out_shape,
        compiler_params=pltpu.CompilerParams(dimension_semantics=("arbitrary",),
                                             vmem_limit_bytes=VMEM_LIMIT),
        name="proj_a",
    )(x, g, w_a, c128, s128, c64, s64)


def _attend_kernel(q_ref, iq_ref, iwt_ref, kb_ref, vb_ref, ik2_ref, o_ref,
                   st_ref, bias_ref, state_ref, m_ref, l_ref, acc_ref,
                   *, tq_real, q0, topk):
    tq = Q_TILE
    kb_sz = KEY_BLOCK
    i = pl.program_id(1)
    qpos0 = q0 + i * tq_real
    n_keys = ((qpos0 + tq_real - 1) // CHUNK + 1) * CHUNK
    nkb = (n_keys + kb_sz - 1) // kb_sz

    lane_q = lax.broadcasted_iota(jnp.int32, (1, tq), 1)
    qchunk = (qpos0 + lane_q) // CHUNK
    iw = iwt_ref[0]

    def score_blk(kb, carry):
        mn, mx = carry
        r0 = pl.multiple_of(kb * kb_sz, kb_sz)
        ik = ik2_ref[0, pl.ds(r0, kb_sz), :]
        ik_even = ik[:, :LANES]
        ik_odd = ik[:, LANES:]
        acc = jnp.zeros((kb_sz, tq), F32)
        for p in range(N_IDX_HEADS // 2):
            iq_pair = iq_ref[0, :, p * LANES:(p + 1) * LANES]
            d0 = _dot_nt(ik_even, iq_pair)
            d1 = _dot_nt(ik_odd, iq_pair)
            acc = acc + jnp.maximum(d0, 0.0) * iw[2 * p:2 * p + 1]
            acc = acc + jnp.maximum(d1, 0.0) * iw[2 * p + 1:2 * p + 2]
        key = r0 + lax.broadcasted_iota(jnp.int32, (kb_sz, 1), 0)
        adm = (key // CHUNK) <= qchunk
        st_ref[pl.ds(r0, kb_sz), :] = jnp.where(adm, acc, -jnp.inf)
        mx = jnp.maximum(mx, jnp.max(jnp.where(adm, acc, -jnp.inf), axis=0, keepdims=True))
        mn = jnp.minimum(mn, jnp.min(jnp.where(adm, acc, jnp.inf), axis=0, keepdims=True))
        return mn, mx

    mn, mx = lax.fori_loop(0, nkb, score_blk,
                           (jnp.full((1, tq), jnp.inf, F32), jnp.full((1, tq), -jnp.inf, F32)))

    n_adm = (qchunk + 1) * CHUNK
    active0 = jnp.logical_and(n_adm > topk, lane_q < tq_real).astype(F32)
    state_ref[0:1, :] = mn
    state_ref[1:2, :] = mx + jnp.maximum(jnp.abs(mx), 1e-30)
    state_ref[2:3, :] = jnp.full((1, tq), F32_LOWEST, F32)
    state_ref[3:4, :] = active0
    topk_f = float(topk)

    def bisect_cond(c):
        return jnp.logical_and(c[0] > 0.0, c[1] < 400)

    def bisect_body(c):
        lo = state_ref[0:1, :]
        hi = state_ref[1:2, :]
        thr = state_ref[2:3, :]
        active = state_ref[3:4, :] > 0.0
        mid = 0.5 * lo + 0.5 * hi
        stuck = jnp.logical_or(mid <= lo, mid >= hi)

        def cnt_blk(kb, acc8):
            r0 = pl.multiple_of(kb * kb_sz, kb_sz)
            ge = (st_ref[pl.ds(r0, kb_sz), :] >= mid).astype(F32)
            return acc8 + jnp.sum(ge.reshape(kb_sz // 8, 8, tq), axis=0)

        cnt8 = lax.fori_loop(0, nkb, cnt_blk, jnp.zeros((8, tq), F32))
        cnt = jnp.sum(cnt8, axis=0, keepdims=True)
        ge_k = cnt >= topk_f
        exact = cnt == topk_f
        moving = jnp.logical_and(active, jnp.logical_not(stuck))
        thr = jnp.where(jnp.logical_and(active, stuck), lo,
                        jnp.where(jnp.logical_and(moving, exact), mid, thr))
        state_ref[0:1, :] = jnp.where(jnp.logical_and(moving, ge_k), mid, lo)
        state_ref[1:2, :] = jnp.where(jnp.logical_and(moving, jnp.logical_not(ge_k)), mid, hi)
        state_ref[2:3, :] = thr
        new_active = jnp.logical_and(moving, jnp.logical_not(exact)).astype(F32)
        state_ref[3:4, :] = new_active
        return jnp.max(new_active), c[1] + 1

    lax.while_loop(bisect_cond, bisect_body, (jnp.max(active0), jnp.int32(0)))
    thr = state_ref[2:3, :]

    def bias_blk(kb, _):
        r0 = pl.multiple_of(kb * kb_sz, kb_sz)
        sel = st_ref[pl.ds(r0, kb_sz), :] >= thr
        bias_ref[kb] = jnp.where(sel, 0.0, NEG_BIAS).T
        return 0

    lax.fori_loop(0, nkb, bias_blk, 0)

    for j in range(N_KV_HEADS):
        qj = jnp.concatenate(
            [q_ref[0, :, (GROUP * j + g) * HEAD_DIM:(GROUP * j + g + 1) * HEAD_DIM] for g in range(GROUP)],
            axis=0)
        m_ref[...] = jnp.full(m_ref.shape, NEG_BIAS, F32)
        l_ref[...] = jnp.zeros(l_ref.shape, F32)
        acc_ref[...] = jnp.zeros(acc_ref.shape, F32)

        def att_blk(kb, _):
            r0 = pl.multiple_of(kb * kb_sz, kb_sz)
            kblk = kb_ref[0, pl.ds(r0, kb_sz), j * HEAD_DIM:(j + 1) * HEAD_DIM]
            vblk = vb_ref[0, pl.ds(r0, kb_sz), j * HEAD_DIM:(j + 1) * HEAD_DIM]
            b = bias_ref[kb]
            s = _dot_nt(qj, kblk) + jnp.concatenate([b] * GROUP, axis=0)
            m_prev = m_ref[...]
            m_new = jnp.maximum(m_prev, jnp.max(s, axis=-1, keepdims=True))
            alpha = jnp.exp(m_prev - m_new)
            p = jnp.exp(s - m_new)
            l_ref[...] = alpha * l_ref[...] + jnp.sum(p, axis=-1, keepdims=True)
            acc_ref[...] = alpha * acc_ref[...] + _dot(p.astype(BF16), vblk)
            m_ref[...] = m_new
            return 0

        lax.fori_loop(0, nkb, att_blk, 0)
        o = acc_ref[...] / l_ref[...]
        for g in range(GROUP):
            hd = GROUP * j + g
            o_ref[0, :, hd * HEAD_DIM:(hd + 1) * HEAD_DIM] = o[g * tq:(g + 1) * tq].astype(o_ref.dtype)


def _attend(q, iq, iwt, kb, vb, ik2, *, tq_real, q0, topk):
    b, t_q, _ = q.shape
    l_keys = kb.shape[1]
    n_q = t_q // Q_TILE
    n_kb = l_keys // KEY_BLOCK
    qspec = pl.BlockSpec((1, Q_TILE, ATTN_W), lambda bi, i: (bi, i, 0))
    kspec = pl.BlockSpec((1, l_keys, KV_W), lambda bi, i: (bi, 0, 0))
    return pl.pallas_call(
        functools.partial(_attend_kernel, tq_real=tq_real, q0=q0, topk=topk),
        grid=(b, n_q),
        in_specs=[qspec, qspec, pl.BlockSpec((1, N_IDX_HEADS, Q_TILE), lambda bi, i: (bi, 0, i)),
                  kspec, kspec, kspec],
        out_specs=qspec,
        out_shape=jax.ShapeDtypeStruct((b, t_q, ATTN_W), F32),
        scratch_shapes=[
            pltpu.VMEM((l_keys, Q_TILE), F32),
            pltpu.VMEM((n_kb, Q_TILE, KEY_BLOCK), F32),
            pltpu.VMEM((8, Q_TILE), F32),
            pltpu.VMEM((GROUP * Q_TILE, 1), F32),
            pltpu.VMEM((GROUP * Q_TILE, 1), F32),
            pltpu.VMEM((GROUP * Q_TILE, HEAD_DIM), F32),
        ],
        compiler_params=pltpu.CompilerParams(dimension_semantics=("arbitrary", "arbitrary"),
                                             vmem_limit_bytes=VMEM_LIMIT),
        name="attend",
    )(q, iq, iwt, kb, vb, ik2)


def _proj_b_kernel(x_ref, g_ref, w_ref, convw_ref, convb_ref, poolw_ref, pscale_ref, liftb_ref, liftc_ref,
                   chist_ref, phist_ref, zbc_ref, cstate_ref, pstate_ref, cin_ext, pin_ext,
                   *, tm, pos0):
    j = pl.program_id(1)

    @pl.when(j == 0)
    def _():
        cin_ext[0:CONV_PAD, :] = chist_ref[0]
        pin_ext[0:POOL_PAD, :] = phist_ref[0]

    @pl.when(j > 0)
    def _():
        cin_ext[0:CONV_PAD, :] = cin_ext[tm:tm + CONV_PAD, :]
        pin_ext[0:POOL_PAD, :] = pin_ext[tm:tm + POOL_PAD, :]

    h = _rms_h(x_ref[...], g_ref[...])

    y = _dot(h, w_ref[:, B_U:B_PIN])
    u = y[:, 0:CONV_W]
    b_gate = y[:, CONV_W:2 * CONV_W]
    c_gate = y[:, 2 * CONV_W:3 * CONV_W]
    gate_b = y[:, 3 * CONV_W:4 * CONV_W]
    cin = c_gate * u
    cin_ext[CONV_PAD:CONV_PAD + tm, :] = cin
    conv = (cin_ext[CONV_PAD - 2:CONV_PAD - 2 + tm, :] * convw_ref[0:1, :]
            + cin_ext[CONV_PAD - 1:CONV_PAD - 1 + tm, :] * convw_ref[1:2, :]
            + cin * convw_ref[2:3, :] + convb_ref[...])
    y_b = b_gate * conv * _silu(gate_b)
    zb = _dot(y_b.astype(BF16), liftb_ref[...])

    y = _dot(h, w_ref[:, B_PIN:B_MB])
    p_in = y[:, 0:POOL_W]
    gate_c = y[:, POOL_W:2 * POOL_W]
    pin_ext[POOL_PAD:POOL_PAD + tm, :] = p_in
    pos = pos0 + j * tm + lax.broadcasted_iota(jnp.int32, (tm, 1), 0)
    yc_parts = []
    for gi, win in enumerate(POOL_WINDOWS):
        sl = slice(gi * POOL_GROUP, (gi + 1) * POOL_GROUP)
        tot = p_in[:, sl]
        for back in range(1, win):
            tot = tot + pin_ext[POOL_PAD - back:POOL_PAD - back + tm, sl]
        cnt = jnp.minimum(win, pos + 1).astype(F32)
        d = tot / cnt - p_in[:, sl]
        mixed = _dot(d.astype(BF16), poolw_ref[gi])
        yc_parts.append(mixed * pscale_ref[:, sl] * _silu(gate_c[:, sl]))
    y_c = jnp.concatenate(yc_parts, axis=-1)
    zc = _dot(y_c.astype(BF16), liftc_ref[...])

    m_b = _dot(h, w_ref[:, B_MB:B_MC])
    m_c = _dot(h, w_ref[:, B_MC:B_END])
    zbc_ref[...] = jax.nn.sigmoid(m_b) * zb + jax.nn.sigmoid(m_c) * zc

    @pl.when(j == pl.num_programs(1) - 1)
    def _():
        cstate_ref[0] = cin_ext[tm:tm + CONV_PAD, :]
        pstate_ref[0] = pin_ext[tm:tm + POOL_PAD, :]


def _proj_b(x, g, w_b, conv_w, conv_b, pool_w, pool_scale, lift_b, lift_c, chist, phist, *, seq_len, tm, pos0):
    m = x.shape[0]
    n_seq = m // seq_len
    n_t = seq_len // tm
    out_shape = (
        jax.ShapeDtypeStruct((m, D_MODEL), F32),
        jax.ShapeDtypeStruct((n_seq, CONV_PAD, CONV_W), F32),
        jax.ShapeDtypeStruct((n_seq, POOL_PAD, POOL_W), F32),
    )
    hist = lambda r, w: pl.BlockSpec((1, r, w), lambda s, j: (s, 0, 0))
    return pl.pallas_call(
        functools.partial(_proj_b_kernel, tm=tm, pos0=pos0),
        grid=(n_seq, n_t),
        in_specs=[pl.BlockSpec((tm, D_MODEL), lambda s, j: (s * n_t + j, 0)),
                  _const_spec((1, D_MODEL)), _const_spec((D_MODEL, B_END)),
                  _const_spec((CONV_K, CONV_W)), _const_spec((1, CONV_W)),
                  _const_spec((len(POOL_WINDOWS), POOL_GROUP, POOL_GROUP)), _const_spec((1, POOL_W)),
                  _const_spec((CONV_W, D_MODEL)), _const_spec((POOL_W, D_MODEL)),
                  hist(CONV_PAD, CONV_W), hist(POOL_PAD, POOL_W)],
        out_specs=(pl.BlockSpec((tm, D_MODEL), lambda s, j: (s * n_t + j, 0)),
                   hist(CONV_PAD, CONV_W), hist(POOL_PAD, POOL_W)),
        out_shape=out_shape,
        scratch_shapes=[pltpu.VMEM((tm + CONV_PAD, CONV_W), F32),
                        pltpu.VMEM((tm + POOL_PAD, POOL_W), F32)],
        compiler_params=pltpu.CompilerParams(dimension_semantics=("arbitrary", "arbitrary"),
                                             vmem_limit_bytes=VMEM_LIMIT),
        name="proj_b",
    )(x, g, w_b, conv_w, conv_b, pool_w, pool_scale, lift_b, lift_c, chist, phist)


def _merge_kernel(x_ref, g_ref, wma_ref, attn_ref, ga_ref, lifta_ref, zbc_ref, wout_ref, fg_ref,
                  *out_refs, final):
    x = x_ref[...]
    h = _rms_h(x, g_ref[...])
    y_a = attn_ref[...] * ga_ref[...].astype(F32)
    z = jax.nn.sigmoid(_dot(h, wma_ref[...])) * _dot(y_a.astype(BF16), lifta_ref[...]) + zbc_ref[...]
    out = x + _dot(z.astype(BF16), wout_ref[...])
    out_refs[0][...] = out
    if final:
        out_refs[1][...] = out * lax.rsqrt(jnp.mean(out * out, axis=-1, keepdims=True) + EPS) * fg_ref[...]


def _merge(x, g, w_ma, attn, ga, lift_a, zbc, w_out, final_g, *, tm, final):
    m = x.shape[0]
    row = lambda w: pl.BlockSpec((tm, w), lambda i: (i, 0))
    n_out = 2 if final else 1
    return pl.pallas_call(
        functools.partial(_merge_kernel, final=final),
        grid=(m // tm,),
        in_specs=[row(D_MODEL), _const_spec((1, D_MODEL)), _const_spec((D_MODEL, D_MODEL)),
                  row(ATTN_W), row(ATTN_W), _const_spec((ATTN_W, D_MODEL)), row(D_MODEL),
                  _const_spec((D_MODEL, D_MODEL)), _const_spec((1, D_MODEL))],
        out_specs=tuple(row(D_MODEL) for _ in range(n_out)),
        out_shape=tuple(jax.ShapeDtypeStruct((m, D_MODEL), F32) for _ in range(n_out)),
        compiler_params=pltpu.CompilerParams(dimension_semantics=("arbitrary",),
                                             vmem_limit_bytes=VMEM_LIMIT),
        name="merge",
    )(x, g, w_ma, attn, ga, lift_a, zbc, w_out, final_g)


def _rope_tables(pos):
    def tab(half, reps):
        inv = ROPE_THETA ** (-jnp.arange(half, dtype=F32) / half)
        ang = pos.astype(F32)[:, None] * inv[None, :]
        cos, sin = jnp.cos(ang), jnp.sin(ang)
        return (jnp.tile(jnp.concatenate([cos, cos], axis=-1), (1, reps)),
                jnp.tile(jnp.concatenate([-sin, sin], axis=-1), (1, reps)))
    c128, s128 = tab(HEAD_DIM // 2, 1)
    c64, s64 = tab(IDX_DIM // 2, LANES // IDX_DIM)
    return c128, s128, c64, s64


def _layer(x, tabs, lw, hist, cache, *, n_seq, seq_len, pos0, topk, tm_tok, tm_seq, final, final_g):
    m = n_seq * seq_len
    q, k, v, kb, vb, iq, ik, ik2, iw, ga = _proj_a(x, lw["g"], lw["w_a"], tabs, tm_tok)

    def seq(a):
        return a.reshape(n_seq, seq_len, a.shape[-1])

    iwt = jnp.swapaxes(seq(iw), 1, 2)
    q3, iq3, kb3, vb3, ik23 = seq(q), seq(iq), seq(kb), seq(vb), seq(ik2)
    if cache is not None:
        ck, cv, cik2 = cache
        l_all = ck.shape[1] + seq_len
        pad_keys = -l_all % KEY_BLOCK
        cat = lambda c, n: jnp.pad(jnp.concatenate([c, n], axis=1), ((0, 0), (0, pad_keys), (0, 0)))
        kb3, vb3, ik23 = cat(ck, kb3), cat(cv, vb3), cat(cik2, ik23)
    pad_q = -seq_len % Q_TILE
    if pad_q:
        q3 = jnp.pad(q3, ((0, 0), (0, pad_q), (0, 0)))
        iq3 = jnp.pad(iq3, ((0, 0), (0, pad_q), (0, 0)))
        iwt = jnp.pad(iwt, ((0, 0), (0, 0), (0, pad_q)))
    attn = _attend(q3, iq3, iwt, kb3, vb3, ik23, tq_real=min(seq_len, Q_TILE), q0=pos0, topk=topk)
    attn = attn[:, :seq_len].reshape(m, ATTN_W)

    zbc, cstate, pstate = _proj_b(x, lw["g"], lw["w_b"], lw["conv_w"], lw["conv_b"], lw["pool_w"],
                                  lw["pool_scale"], lw["lift_b"], lw["lift_c"], hist[0], hist[1],
                                  seq_len=seq_len, tm=tm_seq, pos0=pos0)
    outs = _merge(x, lw["g"], lw["w_ma"], attn, ga, lw["lift_a"], zbc, lw["w_out"], final_g,
                  tm=tm_tok, final=final)
    states = (k.reshape(n_seq, seq_len, N_KV_HEADS, HEAD_DIM), v.reshape(n_seq, seq_len, N_KV_HEADS, HEAD_DIM),
              ik.reshape(n_seq, seq_len, IDX_DIM), cstate[:, CONV_PAD - (CONV_K - 1):], pstate[:, POOL_PAD - POOL_HIST:])
    return outs, states


def kernel(x_prompt, x_sample, cache_k, cache_v, cache_idx_k, state_conv, state_pool, norm_g, w_in, conv_w,
           conv_b, pool_w, pool_scale, lift_a, lift_b, lift_c, w_out, final_norm_g):
    batch, seq, _ = x_prompt.shape
    dec_batch, dec_seq, _ = x_sample.shape
    depth = w_in.shape[0]
    past_len = cache_k.shape[2]
    topk_prompt = min(MAX_TOPK, seq // 4)
    topk_sample = min(MAX_TOPK, (past_len + dec_seq) // 4)

    tm_p, tm_s = 256, 256
    tabs_p = _rope_tables(jnp.arange(seq, dtype=jnp.int32))
    tabs_s = _rope_tables(past_len + (jnp.arange(tm_s, dtype=jnp.int32) % dec_seq))
    final_g = final_norm_g.reshape(1, D_MODEL)

    hp = x_prompt.reshape(batch * seq, D_MODEL)
    hs = x_sample.reshape(dec_batch * dec_seq, D_MODEL)
    zero_hist = (jnp.zeros((batch, CONV_PAD, CONV_W), F32), jnp.zeros((batch, POOL_PAD, POOL_W), F32))
    p_states, s_states = [], []
    y_p = y_s = None
    for l in range(depth):
        wl = w_in[l]
        lw = {
            "g": norm_g[l].reshape(1, D_MODEL),
            "w_a": jnp.concatenate([wl[:, O_Q:O_GA], jnp.zeros((D_MODEL, A_GA - A_IKW - IDX_DIM - N_IDX_HEADS), F32),
                                    wl[:, O_GA:O_U]], axis=1).astype(BF16),
            "w_b": jnp.concatenate([wl[:, O_U:O_MA], wl[:, O_MB:O_END]], axis=1).astype(BF16),
            "w_ma": wl[:, O_MA:O_MB].astype(BF16),
            "conv_w": conv_w[l], "conv_b": conv_b[l].reshape(1, CONV_W),
            "pool_w": pool_w[l].astype(BF16), "pool_scale": pool_scale[l].reshape(1, POOL_W),
            "lift_a": lift_a[l].astype(BF16), "lift_b": lift_b[l].astype(BF16), "lift_c": lift_c[l].astype(BF16),
            "w_out": w_out[l].astype(BF16),
        }
        final = l == depth - 1
        outs_p, st_p = _layer(hp, tabs_p, lw, zero_hist, None, n_seq=batch, seq_len=seq, pos0=0,
                              topk=topk_prompt, tm_tok=tm_p, tm_seq=tm_p, final=final, final_g=final_g)
        cik = cache_idx_k[l].astype(BF16)
        cik2 = jnp.concatenate([cik, jnp.zeros_like(cik), jnp.zeros_like(cik), cik], axis=-1)
        cache = (cache_k[l].reshape(dec_batch, past_len, KV_W).astype(BF16),
                 cache_v[l].reshape(dec_batch, past_len, KV_W).astype(BF16), cik2)
        hist_s = (jnp.pad(state_conv[l], ((0, 0), (CONV_PAD - (CONV_K - 1), 0), (0, 0))),
                  jnp.pad(state_pool[l], ((0, 0), (POOL_PAD - POOL_HIST, 0), (0, 0))))
        outs_s, st_s = _layer(hs, tabs_s, lw, hist_s, cache, n_seq=dec_batch, seq_len=dec_seq, pos0=past_len,
                              topk=topk_sample, tm_tok=tm_s, tm_seq=dec_seq, final=final, final_g=final_g)
        hp, hs = outs_p[0], outs_s[0]
        if final:
            y_p, y_s = outs_p[1], outs_s[1]
        p_states.append(st_p)
        s_states.append(st_s)

    stack = lambda sts, i: jnp.stack([st[i] for st in sts])
    return (y_p.reshape(batch, seq, D_MODEL), y_s.reshape(dec_batch, dec_seq, D_MODEL),
            stack(p_states, 0), stack(p_states, 1), stack(p_states, 2), stack(p_states, 3), stack(p_states, 4),
            stack(s_states, 0), stack(s_states, 1), stack(s_states, 2), stack(s_states, 3), stack(s_states, 4))
```

```python
import functools

import jax
import jax.numpy as jnp
from jax import lax
from jax.experimental import pallas as pl
from jax.experimental.pallas import tpu as pltpu

D_MODEL = 2048
CHUNK = 64
N_HEADS = 8
HEAD_DIM = 128
N_KV_HEADS = 2
GROUP = N_HEADS // N_KV_HEADS
ATTN_W = N_HEADS * HEAD_DIM
KV_W = N_KV_HEADS * HEAD_DIM
N_IDX_HEADS = 16
IDX_DIM = 64
IDX_Q_W = N_IDX_HEADS * IDX_DIM
MAX_TOPK = 256
CONV_W = 512
CONV_K = 3
POOL_W = 512
POOL_WINDOWS = (2, 4, 8, 16)
POOL_GROUP = 128
POOL_HIST = 15
ROPE_THETA = 10000.0
EPS = 1e-6

LANES = 128
CONV_PAD = 8
POOL_PAD = 16
KEY_BLOCK = 512
Q_TILE = 128
NEG_BIAS = -1e30
COUNT_CHAINS = 4
BISECT_UNROLL = 4
LOG2E = 1.4426950408889634
F32_LOWEST = -3.0e38
VMEM_LIMIT = 56 * 1024 * 1024

F32 = jnp.float32
BF16 = jnp.bfloat16

_SIZES = (ATTN_W, KV_W, KV_W, IDX_Q_W, IDX_DIM, N_IDX_HEADS, ATTN_W,
          CONV_W, CONV_W, CONV_W, CONV_W, POOL_W, POOL_W, D_MODEL, D_MODEL, D_MODEL)
_OFFS = [0]
for _s in _SIZES:
    _OFFS.append(_OFFS[-1] + _s)
(O_Q, O_K, O_V, O_IQ, O_IK, O_IW, O_GA, O_U, O_BG, O_CG, O_GB, O_PIN, O_GC, O_MA, O_MB, O_MC, O_END) = _OFFS

A_Q, A_K, A_V, A_IQ, A_IKW, A_GA, A_END = 0, 1024, 1280, 1536, 2560, 2688, 3712
B_U, B_BG, B_CG, B_GB, B_PIN, B_GC, B_MB, B_MC, B_END = 0, 512, 1024, 1536, 2048, 2560, 3072, 5120, 7168


def _dot(a, b):
    return jnp.dot(a, b, preferred_element_type=F32)


def _dot_nt(a, b):
    return lax.dot_general(a, b, (((1,), (1,)), ((), ())), preferred_element_type=F32)


def _rms_h(x, g):
    h = x * lax.rsqrt(jnp.mean(x * x, axis=-1, keepdims=True) + EPS) * g
    return h.astype(BF16)


def _silu(x):
    return x * jax.nn.sigmoid(x)


def _const_spec(shape):
    nd = len(shape)
    return pl.BlockSpec(shape, lambda *_: (0,) * nd, pipeline_mode=pl.Buffered(1))


def _proj_a_kernel(x_ref, g_ref, w_ref, c128_ref, s128_ref, c64_ref, s64_ref,
                   q_ref, k_ref, v_ref, kb_ref, vb_ref, iq_ref, ik_ref, ik2_ref, iw_ref, ga_ref):
    h = _rms_h(x_ref[...], g_ref[...])
    c128 = c128_ref[...]
    s128 = s128_ref[...]
    c64 = c64_ref[...]
    s64 = s64_ref[...]
    lane = lax.broadcasted_iota(jnp.int32, (1, LANES), 1)
    first_half64 = (lane % IDX_DIM) < (IDX_DIM // 2)

    def rope128(y):
        return y * c128 + pltpu.roll(y, HEAD_DIM // 2, 1) * s128

    def rope64(y):
        partner = jnp.where(first_half64, pltpu.roll(y, LANES - IDX_DIM // 2, 1),
                            pltpu.roll(y, IDX_DIM // 2, 1))
        return y * c64 + partner * s64

    y = _dot(h, w_ref[:, A_Q:A_K])
    for hd in range(N_HEADS):
        sl = slice(hd * HEAD_DIM, (hd + 1) * HEAD_DIM)
        q_ref[:, sl] = (rope128(y[:, sl]) * (HEAD_DIM ** -0.5 * LOG2E)).astype(BF16)

    y = _dot(h, w_ref[:, A_K:A_IQ])
    for hd in range(N_KV_HEADS):
        sl = slice(hd * HEAD_DIM, (hd + 1) * HEAD_DIM)
        kr = rope128(y[:, sl])
        k_ref[:, sl] = kr
        kb_ref[:, sl] = kr.astype(BF16)
    v = y[:, KV_W:]
    v_ref[...] = v
    vb_ref[...] = v.astype(BF16)

    y = _dot(h, w_ref[:, A_IQ:A_IKW])
    for c in range(IDX_Q_W // LANES):
        sl = slice(c * LANES, (c + 1) * LANES)
        iq_ref[:, sl] = rope64(y[:, sl]).astype(BF16)

    y = _dot(h, w_ref[:, A_IKW:A_GA])
    ikr = rope64(y)
    ik_ref[...] = ikr[:, :IDX_DIM]
    ikz = jnp.where(lane < IDX_DIM, ikr, 0.0)
    ik2_ref[:, :LANES] = ikz.astype(BF16)
    ik2_ref[:, LANES:] = pltpu.roll(ikz, IDX_DIM, 1).astype(BF16)
    iw_ref[...] = y[:, IDX_DIM:IDX_DIM + N_IDX_HEADS] * ((IDX_DIM ** -0.5) * (N_IDX_HEADS ** -0.5))

    ga_ref[...] = _silu(_dot(h, w_ref[:, A_GA:A_END])).astype(BF16)


def _proj_a(x, g, w_a, tabs, tm):
    m = x.shape[0]
    c128, s128, c64, s64 = tabs
    n_pt = c128.shape[0] // tm
    row = lambda w: pl.BlockSpec((tm, w), lambda i: (i, 0))
    tab = pl.BlockSpec((tm, LANES), lambda i: (i % n_pt, 0))
    out_shape = (
        jax.ShapeDtypeStruct((m, ATTN_W), BF16),
        jax.ShapeDtypeStruct((m, KV_W), F32),
        jax.ShapeDtypeStruct((m, KV_W), F32),
        jax.ShapeDtypeStruct((m, KV_W), BF16),
        jax.ShapeDtypeStruct((m, KV_W), BF16),
        jax.ShapeDtypeStruct((m, IDX_Q_W), BF16),
        jax.ShapeDtypeStruct((m, IDX_DIM), F32),
        jax.ShapeDtypeStruct((m, 2 * LANES), BF16),
        jax.ShapeDtypeStruct((m, N_IDX_HEADS), F32),
        jax.ShapeDtypeStruct((m, ATTN_W), BF16),
    )
    out_specs = (row(ATTN_W), row(KV_W), row(KV_W), row(KV_W), row(KV_W), row(IDX_Q_W),
                 row(IDX_DIM), row(2 * LANES), row(N_IDX_HEADS), row(ATTN_W))
    return pl.pallas_call(
        _proj_a_kernel,
        grid=(m // tm,),
        in_specs=[row(D_MODEL), _const_spec((1, D_MODEL)), _const_spec((D_MODEL, A_END)),
                  tab, tab, tab, tab],
        out_specs=out_specs,
        out_shape=out_shape,
        compiler_params=pltpu.CompilerParams(dimension_semantics=("arbitrary",),
                                             vmem_limit_bytes=VMEM_LIMIT),
        name="proj_a",
    )(x, g, w_a, c128, s128, c64, s64)


def _attend_kernel(q_ref, iq_ref, iwt_ref, kb_ref, vb_ref, ik2_ref, o_ref,
                   st_ref, sbuf_ref, state_ref, acc_ref,
                   *, tq_real, q0, topk):
    tq = Q_TILE
    kb_sz = KEY_BLOCK
    i = pl.program_id(1)
    qpos0 = q0 + i * tq_real
    n_keys = ((qpos0 + tq_real - 1) // CHUNK + 1) * CHUNK
    nkb = (n_keys + kb_sz - 1) // kb_sz

    lane_q = lax.broadcasted_iota(jnp.int32, (1, tq), 1)
    qchunk = (qpos0 + lane_q) // CHUNK
    iw = iwt_ref[0]

    def score_blk(kb, carry):
        mn, mx = carry
        r0 = pl.multiple_of(kb * kb_sz, kb_sz)
        ik = ik2_ref[0, pl.ds(r0, kb_sz), :]
        ik_even = ik[:, :LANES]
        ik_odd = ik[:, LANES:]
        acc = jnp.zeros((kb_sz, tq), F32)
        for p in range(N_IDX_HEADS // 2):
            iq_pair = iq_ref[0, :, p * LANES:(p + 1) * LANES]
            d0 = _dot_nt(ik_even, iq_pair)
            d1 = _dot_nt(ik_odd, iq_pair)
            acc = acc + jnp.maximum(d0, 0.0) * iw[2 * p:2 * p + 1]
            acc = acc + jnp.maximum(d1, 0.0) * iw[2 * p + 1:2 * p + 2]
        key = r0 + lax.broadcasted_iota(jnp.int32, (kb_sz, 1), 0)
        adm = (key // CHUNK) <= qchunk
        st_ref[pl.ds(r0, kb_sz), :] = jnp.where(adm, acc, -jnp.inf)
        mx = jnp.maximum(mx, jnp.max(jnp.where(adm, acc, -jnp.inf), axis=0, keepdims=True))
        mn = jnp.minimum(mn, jnp.min(jnp.where(adm, acc, jnp.inf), axis=0, keepdims=True))
        return mn, mx

    mn, mx = lax.fori_loop(0, nkb, score_blk,
                           (jnp.full((1, tq), jnp.inf, F32), jnp.full((1, tq), -jnp.inf, F32)))

    n_adm = ((qchunk + 1) * CHUNK).astype(F32)
    topk_f = float(topk)
    active0 = jnp.logical_and(n_adm > topk_f, lane_q < tq_real).astype(F32)
    state_ref[0:1, :] = mn
    state_ref[1:2, :] = mx + jnp.maximum(jnp.abs(mx), 1e-30)
    state_ref[2:3, :] = jnp.full((1, tq), F32_LOWEST, F32)
    state_ref[3:4, :] = active0
    state_ref[4:5, :] = n_adm
    state_ref[5:6, :] = jnp.zeros((1, tq), F32)

    def count_ge(mid):
        def cnt_blk(kb, acc):
            r0 = pl.multiple_of(kb * kb_sz, kb_sz)
            ge = (st_ref[pl.ds(r0, kb_sz), :] >= mid).astype(F32)
            return acc + jnp.sum(ge.reshape(COUNT_CHAINS, kb_sz // (8 * COUNT_CHAINS), 8, tq), axis=1)

        acc = lax.fori_loop(0, nkb, cnt_blk, jnp.zeros((COUNT_CHAINS, 8, tq), F32))
        return jnp.sum(jnp.sum(acc, axis=0), axis=0, keepdims=True)

    def bisect_step(st):
        lo, hi, thr, active, clo, tie = st
        mid = 0.5 * lo + 0.5 * hi
        stuck = jnp.logical_or(mid <= lo, mid >= hi)
        cnt = count_ge(mid)
        ge_k = cnt >= topk_f
        exact = cnt == topk_f
        moving = jnp.logical_and(active, jnp.logical_not(stuck))
        ended = jnp.logical_and(active, stuck)
        thr = jnp.where(ended, lo, jnp.where(jnp.logical_and(moving, exact), mid, thr))
        tie = jnp.logical_or(tie, jnp.logical_and(ended, clo > topk_f))
        up = jnp.logical_and(moving, ge_k)
        down = jnp.logical_and(moving, jnp.logical_not(ge_k))
        return (jnp.where(up, mid, lo), jnp.where(down, mid, hi), thr,
                jnp.logical_and(moving, jnp.logical_not(exact)), jnp.where(up, cnt, clo), tie)

    def bisect_cond(c):
        return jnp.logical_and(c[0] > 0.0, c[1] < 128)

    def bisect_body(c):
        st = (state_ref[0:1, :], state_ref[1:2, :], state_ref[2:3, :], state_ref[3:4, :] > 0.0,
              state_ref[4:5, :], state_ref[5:6, :] > 0.0)
        for _ in range(BISECT_UNROLL):
            st = bisect_step(st)
        active_f = st[3].astype(F32)
        state_ref[0:1, :] = st[0]
        state_ref[1:2, :] = st[1]
        state_ref[2:3, :] = st[2]
        state_ref[3:4, :] = active_f
        state_ref[4:5, :] = st[4]
        state_ref[5:6, :] = st[5].astype(F32)
        return jnp.max(active_f), c[1] + 1

    lax.while_loop(bisect_cond, bisect_body, (jnp.max(active0), jnp.int32(0)))
    thr = state_ref[2:3, :]

    @pl.when(jnp.max(state_ref[5:6, :]) > 0.0)
    def _():
        tie_row = state_ref[5:6, :] > 0.0

        def gt_blk(kb, acc):
            r0 = pl.multiple_of(kb * kb_sz, kb_sz)
            return acc + jnp.sum((st_ref[pl.ds(r0, kb_sz), :] > thr).astype(F32), axis=0, keepdims=True)

        need = topk_f - lax.fori_loop(0, nkb, gt_blk, jnp.zeros((1, tq), F32))
        tri = (lax.broadcasted_iota(jnp.int32, (kb_sz, kb_sz), 0)
               >= lax.broadcasted_iota(jnp.int32, (kb_sz, kb_sz), 1)).astype(BF16)

        def fix_blk(kb, run):
            r0 = pl.multiple_of(kb * kb_sz, kb_sz)
            blk = st_ref[pl.ds(r0, kb_sz), :]
            eq = jnp.logical_and(blk == thr, tie_row)
            eq_f = eq.astype(F32)
            rank = _dot(tri, eq_f.astype(BF16)) - eq_f + run
            st_ref[pl.ds(r0, kb_sz), :] = jnp.where(jnp.logical_and(eq, rank >= need), -jnp.inf, blk)
            return run + jnp.sum(eq_f, axis=0, keepdims=True)

        lax.fori_loop(0, nkb, fix_blk, jnp.zeros((1, tq), F32))

    red = lambda a, op: op(a.reshape(2, kb_sz // 16, 8, a.shape[-1]), axis=1)
    for j in range(N_KV_HEADS):
        kv_cols = slice(j * HEAD_DIM, (j + 1) * HEAD_DIM)
        qj = jnp.concatenate(
            [q_ref[0, :, (GROUP * j + g) * HEAD_DIM:(GROUP * j + g + 1) * HEAD_DIM] for g in range(GROUP)],
            axis=0)

        def qk_blk(kb, m8):
            r0 = pl.multiple_of(kb * kb_sz, kb_sz)
            s = _dot_nt(kb_ref[0, pl.ds(r0, kb_sz), kv_cols], qj)
            bias = jnp.where(st_ref[pl.ds(r0, kb_sz), :] >= thr, 0.0, NEG_BIAS)
            parts = []
            for g in range(GROUP):
                sg = s[:, g * tq:(g + 1) * tq] + bias
                sbuf_ref[kb, :, g * tq:(g + 1) * tq] = sg
                parts.append(red(sg, jnp.max))
            return jnp.maximum(m8, jnp.concatenate(parts, axis=-1))

        m8 = lax.fori_loop(0, nkb, qk_blk, jnp.full((2, 8, GROUP * tq), NEG_BIAS, F32))
        m = jnp.max(jnp.max(m8, axis=0), axis=0, keepdims=True)
        acc_ref[...] = jnp.zeros(acc_ref.shape, F32)

        def pv_blk(kb, l8):
            r0 = pl.multiple_of(kb * kb_sz, kb_sz)
            p = jnp.exp2(sbuf_ref[kb] - m)
            acc_ref[...] += lax.dot_general(vb_ref[0, pl.ds(r0, kb_sz), kv_cols], p.astype(BF16),
                                            (((0,), (0,)), ((), ())), preferred_element_type=F32)
            return l8 + red(p, jnp.sum)

        l8 = lax.fori_loop(0, nkb, pv_blk, jnp.zeros((2, 8, GROUP * tq), F32))
        denom = jnp.sum(jnp.sum(l8, axis=0), axis=0, keepdims=True)
        o = acc_ref[...] / denom
        for g in range(GROUP):
            hd = GROUP * j + g
            o_ref[0, :, hd * HEAD_DIM:(hd + 1) * HEAD_DIM] = o[:, g * tq:(g + 1) * tq].T.astype(o_ref.dtype)


def _attend(q, iq, iwt, kb, vb, ik2, *, tq_real, q0, topk):
    b, t_q, _ = q.shape
    l_keys = kb.shape[1]
    n_q = t_q // Q_TILE
    n_kb = l_keys // KEY_BLOCK
    qspec = pl.BlockSpec((1, Q_TILE, ATTN_W), lambda bi, i: (bi, i, 0))
    kspec = pl.BlockSpec((1, l_keys, KV_W), lambda bi, i: (bi, 0, 0))
    return pl.pallas_call(
        functools.partial(_attend_kernel, tq_real=tq_real, q0=q0, topk=topk),
        grid=(b, n_q),
        in_specs=[qspec, qspec, pl.BlockSpec((1, N_IDX_HEADS, Q_TILE), lambda bi, i: (bi, 0, i)),
                  kspec, kspec, kspec],
        out_specs=qspec,
        out_shape=jax.ShapeDtypeStruct((b, t_q, ATTN_W), F32),
        scratch_shapes=[
            pltpu.VMEM((l_keys, Q_TILE), F32),
            pltpu.VMEM((n_kb, KEY_BLOCK, GROUP * Q_TILE), F32),
            pltpu.VMEM((8, Q_TILE), F32),
            pltpu.VMEM((HEAD_DIM, GROUP * Q_TILE), F32),
        ],
        compiler_params=pltpu.CompilerParams(dimension_semantics=("arbitrary", "arbitrary"),
                                             vmem_limit_bytes=VMEM_LIMIT),
        name="attend",
    )(q, iq, iwt, kb, vb, ik2)


def _proj_b_kernel(x_ref, g_ref, w_ref, convw_ref, convb_ref, poolw_ref, pscale_ref, liftb_ref, liftc_ref,
                   chist_ref, phist_ref, zbc_ref, cstate_ref, pstate_ref, cin_ext, pin_ext,
                   *, tm, pos0):
    j = pl.program_id(1)

    @pl.when(j == 0)
    def _():
        cin_ext[0:CONV_PAD, :] = chist_ref[0]
        pin_ext[0:POOL_PAD, :] = phist_ref[0]

    @pl.when(j > 0)
    def _():
        cin_ext[0:CONV_PAD, :] = cin_ext[tm:tm + CONV_PAD, :]
        pin_ext[0:POOL_PAD, :] = pin_ext[tm:tm + POOL_PAD, :]

    h = _rms_h(x_ref[...], g_ref[...])

    y = _dot(h, w_ref[:, B_U:B_PIN])
    u = y[:, 0:CONV_W]
    b_gate = y[:, CONV_W:2 * CONV_W]
    c_gate = y[:, 2 * CONV_W:3 * CONV_W]
    gate_b = y[:, 3 * CONV_W:4 * CONV_W]
    cin = c_gate * u
    cin_ext[CONV_PAD:CONV_PAD + tm, :] = cin
    conv = (cin_ext[CONV_PAD - 2:CONV_PAD - 2 + tm, :] * convw_ref[0:1, :]
            + cin_ext[CONV_PAD - 1:CONV_PAD - 1 + tm, :] * convw_ref[1:2, :]
            + cin * convw_ref[2:3, :] + convb_ref[...])
    y_b = b_gate * conv * _silu(gate_b)
    zb = _dot(y_b.astype(BF16), liftb_ref[...])

    y = _dot(h, w_ref[:, B_PIN:B_MB])
    p_in = y[:, 0:POOL_W]
    gate_c = y[:, POOL_W:2 * POOL_W]
    pin_ext[POOL_PAD:POOL_PAD + tm, :] = p_in
    pos = pos0 + j * tm + lax.broadcasted_iota(jnp.int32, (tm, 1), 0)
    yc_parts = []
    for gi, win in enumerate(POOL_WINDOWS):
        sl = slice(gi * POOL_GROUP, (gi + 1) * POOL_GROUP)
        tot = p_in[:, sl]
        for back in range(1, win):
            tot = tot + pin_ext[POOL_PAD - back:POOL_PAD - back + tm, sl]
        cnt = jnp.minimum(win, pos + 1).astype(F32)
        d = tot / cnt - p_in[:, sl]
        mixed = _dot(d.astype(BF16), poolw_ref[gi])
        yc_parts.append(mixed * pscale_ref[:, sl] * _silu(gate_c[:, sl]))
    y_c = jnp.concatenate(yc_parts, axis=-1)
    zc = _dot(y_c.astype(BF16), liftc_ref[...])

    m_b = _dot(h, w_ref[:, B_MB:B_MC])
    m_c = _dot(h, w_ref[:, B_MC:B_END])
    zbc_ref[...] = jax.nn.sigmoid(m_b) * zb + jax.nn.sigmoid(m_c) * zc

    @pl.when(j == pl.num_programs(1) - 1)
    def _():
        cstate_ref[0] = cin_ext[tm:tm + CONV_PAD, :]
        pstate_ref[0] = pin_ext[tm:tm + POOL_PAD, :]


def _proj_b(x, g, w_b, conv_w, conv_b, pool_w, pool_scale, lift_b, lift_c, chist, phist, *, seq_len, tm, pos0):
    m = x.shape[0]
    n_seq = m // seq_len
    n_t = seq_len // tm
    out_shape = (
        jax.ShapeDtypeStruct((m, D_MODEL), F32),
        jax.ShapeDtypeStruct((n_seq, CONV_PAD, CONV_W), F32),
        jax.ShapeDtypeStruct((n_seq, POOL_PAD, POOL_W), F32),
    )
    hist = lambda r, w: pl.BlockSpec((1, r, w), lambda s, j: (s, 0, 0))
    return pl.pallas_call(
        functools.partial(_proj_b_kernel, tm=tm, pos0=pos0),
        grid=(n_seq, n_t),
        in_specs=[pl.BlockSpec((tm, D_MODEL), lambda s, j: (s * n_t + j, 0)),
                  _const_spec((1, D_MODEL)), _const_spec((D_MODEL, B_END)),
                  _const_spec((CONV_K, CONV_W)), _const_spec((1, CONV_W)),
                  _const_spec((len(POOL_WINDOWS), POOL_GROUP, POOL_GROUP)), _const_spec((1, POOL_W)),
                  _const_spec((CONV_W, D_MODEL)), _const_spec((POOL_W, D_MODEL)),
                  hist(CONV_PAD, CONV_W), hist(POOL_PAD, POOL_W)],
        out_specs=(pl.BlockSpec((tm, D_MODEL), lambda s, j: (s * n_t + j, 0)),
                   hist(CONV_PAD, CONV_W), hist(POOL_PAD, POOL_W)),
        out_shape=out_shape,
        scratch_shapes=[pltpu.VMEM((tm + CONV_PAD, CONV_W), F32),
                        pltpu.VMEM((tm + POOL_PAD, POOL_W), F32)],
        compiler_params=pltpu.CompilerParams(dimension_semantics=("arbitrary", "arbitrary"),
                                             vmem_limit_bytes=VMEM_LIMIT),
        name="proj_b",
    )(x, g, w_b, conv_w, conv_b, pool_w, pool_scale, lift_b, lift_c, chist, phist)


def _merge_kernel(x_ref, g_ref, wma_ref, attn_ref, ga_ref, lifta_ref, zbc_ref, wout_ref, fg_ref,
                  *out_refs, final):
    x = x_ref[...]
    h = _rms_h(x, g_ref[...])
    y_a = attn_ref[...] * ga_ref[...].astype(F32)
    z = jax.nn.sigmoid(_dot(h, wma_ref[...])) * _dot(y_a.astype(BF16), lifta_ref[...]) + zbc_ref[...]
    out = x + _dot(z.astype(BF16), wout_ref[...])
    out_refs[0][...] = out
    if final:
        out_refs[1][...] = out * lax.rsqrt(jnp.mean(out * out, axis=-1, keepdims=True) + EPS) * fg_ref[...]


def _merge(x, g, w_ma, attn, ga, lift_a, zbc, w_out, final_g, *, tm, final):
    m = x.shape[0]
    row = lambda w: pl.BlockSpec((tm, w), lambda i: (i, 0))
    n_out = 2 if final else 1
    return pl.pallas_call(
        functools.partial(_merge_kernel, final=final),
        grid=(m // tm,),
        in_specs=[row(D_MODEL), _const_spec((1, D_MODEL)), _const_spec((D_MODEL, D_MODEL)),
                  row(ATTN_W), row(ATTN_W), _const_spec((ATTN_W, D_MODEL)), row(D_MODEL),
                  _const_spec((D_MODEL, D_MODEL)), _const_spec((1, D_MODEL))],
        out_specs=tuple(row(D_MODEL) for _ in range(n_out)),
        out_shape=tuple(jax.ShapeDtypeStruct((m, D_MODEL), F32) for _ in range(n_out)),
        compiler_params=pltpu.CompilerParams(dimension_semantics=("arbitrary",),
                                             vmem_limit_bytes=VMEM_LIMIT),
        name="merge",
    )(x, g, w_ma, attn, ga, lift_a, zbc, w_out, final_g)


def _rope_tables(pos):
    def tab(half, reps):
        inv = ROPE_THETA ** (-jnp.arange(half, dtype=F32) / half)
        ang = pos.astype(F32)[:, None] * inv[None, :]
        cos, sin = jnp.cos(ang), jnp.sin(ang)
        return (jnp.tile(jnp.concatenate([cos, cos], axis=-1), (1, reps)),
                jnp.tile(jnp.concatenate([-sin, sin], axis=-1), (1, reps)))
    c128, s128 = tab(HEAD_DIM // 2, 1)
    c64, s64 = tab(IDX_DIM // 2, LANES // IDX_DIM)
    return c128, s128, c64, s64


def _layer(x, tabs, lw, hist, cache, *, n_seq, seq_len, pos0, topk, tm_tok, tm_seq, final, final_g):
    m = n_seq * seq_len
    q, k, v, kb, vb, iq, ik, ik2, iw, ga = _proj_a(x, lw["g"], lw["w_a"], tabs, tm_tok)

    def seq(a):
        return a.reshape(n_seq, seq_len, a.shape[-1])

    iwt = jnp.swapaxes(seq(iw), 1, 2)
    q3, iq3, kb3, vb3, ik23 = seq(q), seq(iq), seq(kb), seq(vb), seq(ik2)
    if cache is not None:
        ck, cv, cik2 = cache
        l_all = ck.shape[1] + seq_len
        pad_keys = -l_all % KEY_BLOCK
        cat = lambda c, n: jnp.pad(jnp.concatenate([c, n], axis=1), ((0, 0), (0, pad_keys), (0, 0)))
        kb3, vb3, ik23 = cat(ck, kb3), cat(cv, vb3), cat(cik2, ik23)
    pad_q = -seq_len % Q_TILE
    if pad_q:
        q3 = jnp.pad(q3, ((0, 0), (0, pad_q), (0, 0)))
        iq3 = jnp.pad(iq3, ((0, 0), (0, pad_q), (0, 0)))
        iwt = jnp.pad(iwt, ((0, 0), (0, 0), (0, pad_q)))
    attn = _attend(q3, iq3, iwt, kb3, vb3, ik23, tq_real=min(seq_len, Q_TILE), q0=pos0, topk=topk)
    attn = attn[:, :seq_len].reshape(m, ATTN_W)

    zbc, cstate, pstate = _proj_b(x, lw["g"], lw["w_b"], lw["conv_w"], lw["conv_b"], lw["pool_w"],
                                  lw["pool_scale"], lw["lift_b"], lw["lift_c"], hist[0], hist[1],
                                  seq_len=seq_len, tm=tm_seq, pos0=pos0)
    outs = _merge(x, lw["g"], lw["w_ma"], attn, ga, lw["lift_a"], zbc, lw["w_out"], final_g,
                  tm=tm_tok, final=final)
    states = (k.reshape(n_seq, seq_len, N_KV_HEADS, HEAD_DIM), v.reshape(n_seq, seq_len, N_KV_HEADS, HEAD_DIM),
              ik.reshape(n_seq, seq_len, IDX_DIM), cstate[:, CONV_PAD - (CONV_K - 1):], pstate[:, POOL_PAD - POOL_HIST:])
    return outs, states


def kernel(x_prompt, x_sample, cache_k, cache_v, cache_idx_k, state_conv, state_pool, norm_g, w_in, conv_w,
           conv_b, pool_w, pool_scale, lift_a, lift_b, lift_c, w_out, final_norm_g):
    batch, seq, _ = x_prompt.shape
    dec_batch, dec_seq, _ = x_sample.shape
    depth = w_in.shape[0]
    past_len = cache_k.shape[2]
    topk_prompt = min(MAX_TOPK, seq // 4)
    topk_sample = min(MAX_TOPK, (past_len + dec_seq) // 4)

    tm_p, tm_s = 256, 256
    tabs_p = _rope_tables(jnp.arange(seq, dtype=jnp.int32))
    tabs_s = _rope_tables(past_len + (jnp.arange(tm_s, dtype=jnp.int32) % dec_seq))
    final_g = final_norm_g.reshape(1, D_MODEL)

    hp = x_prompt.reshape(batch * seq, D_MODEL)
    hs = x_sample.reshape(dec_batch * dec_seq, D_MODEL)
    zero_hist = (jnp.zeros((batch, CONV_PAD, CONV_W), F32), jnp.zeros((batch, POOL_PAD, POOL_W), F32))
    p_states, s_states = [], []
    y_p = y_s = None
    for l in range(depth):
        wl = w_in[l]
        lw = {
            "g": norm_g[l].reshape(1, D_MODEL),
            "w_a": jnp.concatenate([wl[:, O_Q:O_GA], jnp.zeros((D_MODEL, A_GA - A_IKW - IDX_DIM - N_IDX_HEADS), F32),
                                    wl[:, O_GA:O_U]], axis=1).astype(BF16),
            "w_b": jnp.concatenate([wl[:, O_U:O_MA], wl[:, O_MB:O_END]], axis=1).astype(BF16),
            "w_ma": wl[:, O_MA:O_MB].astype(BF16),
            "conv_w": conv_w[l], "conv_b": conv_b[l].reshape(1, CONV_W),
            "pool_w": pool_w[l].astype(BF16), "pool_scale": pool_scale[l].reshape(1, POOL_W),
            "lift_a": lift_a[l].astype(BF16), "lift_b": lift_b[l].astype(BF16), "lift_c": lift_c[l].astype(BF16),
            "w_out": w_out[l].astype(BF16),
        }
        final = l == depth - 1
        outs_p, st_p = _layer(hp, tabs_p, lw, zero_hist, None, n_seq=batch, seq_len=seq, pos0=0,
                              topk=topk_prompt, tm_tok=tm_p, tm_seq=tm_p, final=final, final_g=final_g)
        cik = cache_idx_k[l].astype(BF16)
        cik2 = jnp.concatenate([cik, jnp.zeros_like(cik), jnp.zeros_like(cik), cik], axis=-1)
        cache = (cache_k[l].reshape(dec_batch, past_len, KV_W).astype(BF16),
                 cache_v[l].reshape(dec_batch, past_len, KV_W).astype(BF16), cik2)
        hist_s = (jnp.pad(state_conv[l], ((0, 0), (CONV_PAD - (CONV_K - 1), 0), (0, 0))),
                  jnp.pad(state_pool[l], ((0, 0), (POOL_PAD - POOL_HIST, 0), (0, 0))))
        outs_s, st_s = _layer(hs, tabs_s, lw, hist_s, cache, n_seq=dec_batch, seq_len=dec_seq, pos0=past_len,
                              topk=topk_sample, tm_tok=tm_s, tm_seq=dec_seq, final=final, final_g=final_g)
        hp, hs = outs_p[0], outs_s[0]
        if final:
            y_p, y_s = outs_p[1], outs_s[1]
        p_states.append(st_p)
        s_states.append(st_s)

    stack = lambda sts, i: jnp.stack([st[i] for st in sts])
    return (y_p.reshape(batch, seq, D_MODEL), y_s.reshape(dec_batch, dec_seq, D_MODEL),
            stack(p_states, 0), stack(p_states, 1), stack(p_states, 2), stack(p_states, 3), stack(p_states, 4),
            stack(s_states, 0), stack(s_states, 1), stack(s_states, 2), stack(s_states, 3), stack(s_states, 4))
```

```python
import functools

import jax
import jax.numpy as jnp
from jax import lax
from jax.experimental import pallas as pl
from jax.experimental.pallas import tpu as pltpu

D_MODEL = 2048
CHUNK = 64
N_HEADS = 8
HEAD_DIM = 128
N_KV_HEADS = 2
GROUP = N_HEADS // N_KV_HEADS
ATTN_W = N_HEADS * HEAD_DIM
KV_W = N_KV_HEADS * HEAD_DIM
N_IDX_HEADS = 16
IDX_DIM = 64
IDX_Q_W = N_IDX_HEADS * IDX_DIM
MAX_TOPK = 256
CONV_W = 512
CONV_K = 3
POOL_W = 512
POOL_WINDOWS = (2, 4, 8, 16)
POOL_GROUP = 128
POOL_HIST = 15
ROPE_THETA = 10000.0
EPS = 1e-6

LANES = 128
CONV_PAD = 8
POOL_PAD = 16
KEY_BLOCK = 512
Q_TILE = 128
NEG_BIAS = -1e30
COUNT_CHAINS = 4
BISECT_UNROLL = 4
LOG2E = 1.4426950408889634
F32_LOWEST = -3.0e38
VMEM_LIMIT = 56 * 1024 * 1024

F32 = jnp.float32
BF16 = jnp.bfloat16

_SIZES = (ATTN_W, KV_W, KV_W, IDX_Q_W, IDX_DIM, N_IDX_HEADS, ATTN_W,
          CONV_W, CONV_W, CONV_W, CONV_W, POOL_W, POOL_W, D_MODEL, D_MODEL, D_MODEL)
_OFFS = [0]
for _s in _SIZES:
    _OFFS.append(_OFFS[-1] + _s)
(O_Q, O_K, O_V, O_IQ, O_IK, O_IW, O_GA, O_U, O_BG, O_CG, O_GB, O_PIN, O_GC, O_MA, O_MB, O_MC, O_END) = _OFFS

A_Q, A_K, A_V, A_IQ, A_IKW, A_GA, A_END = 0, 1024, 1280, 1536, 2560, 2688, 3712
B_U, B_BG, B_CG, B_GB, B_PIN, B_GC, B_MB, B_MC, B_END = 0, 512, 1024, 1536, 2048, 2560, 3072, 5120, 7168


def _dot(a, b):
    return jnp.dot(a, b, preferred_element_type=F32)


def _dot_nt(a, b):
    return lax.dot_general(a, b, (((1,), (1,)), ((), ())), preferred_element_type=F32)


def _rms_h(x, g):
    h = x * lax.rsqrt(jnp.mean(x * x, axis=-1, keepdims=True) + EPS) * g
    return h.astype(BF16)


def _silu(x):
    return x * jax.nn.sigmoid(x)


def _const_spec(shape):
    nd = len(shape)
    return pl.BlockSpec(shape, lambda *_: (0,) * nd, pipeline_mode=pl.Buffered(1))


def _proj_a_kernel(x_ref, g_ref, w_ref, c128_ref, s128_ref, c64_ref, s64_ref,
                   q_ref, k_ref, v_ref, kb_ref, vb_ref, iq_ref, ik_ref, ik2_ref, iw_ref, ga_ref):
    h = _rms_h(x_ref[...], g_ref[...])
    c128 = c128_ref[...]
    s128 = s128_ref[...]
    c64 = c64_ref[...]
    s64 = s64_ref[...]
    lane = lax.broadcasted_iota(jnp.int32, (1, LANES), 1)
    first_half64 = (lane % IDX_DIM) < (IDX_DIM // 2)

    def rope128(y):
        return y * c128 + pltpu.roll(y, HEAD_DIM // 2, 1) * s128

    def rope64(y):
        partner = jnp.where(first_half64, pltpu.roll(y, LANES - IDX_DIM // 2, 1),
                            pltpu.roll(y, IDX_DIM // 2, 1))
        return y * c64 + partner * s64

    y = _dot(h, w_ref[:, A_Q:A_K])
    for hd in range(N_HEADS):
        sl = slice(hd * HEAD_DIM, (hd + 1) * HEAD_DIM)
        q_ref[:, sl] = (rope128(y[:, sl]) * (HEAD_DIM ** -0.5 * LOG2E)).astype(BF16)

    y = _dot(h, w_ref[:, A_K:A_IQ])
    for hd in range(N_KV_HEADS):
        sl = slice(hd * HEAD_DIM, (hd + 1) * HEAD_DIM)
        kr = rope128(y[:, sl])
        k_ref[:, hd, :] = kr
        kb_ref[:, sl] = kr.astype(BF16)
        v_ref[:, hd, :] = y[:, KV_W + hd * HEAD_DIM:KV_W + (hd + 1) * HEAD_DIM]
    vb_ref[...] = y[:, KV_W:].astype(BF16)

    y = _dot(h, w_ref[:, A_IQ:A_IKW])
    for c in range(IDX_Q_W // LANES):
        sl = slice(c * LANES, (c + 1) * LANES)
        iq_ref[:, sl] = rope64(y[:, sl]).astype(BF16)

    y = _dot(h, w_ref[:, A_IKW:A_GA])
    ikr = rope64(y)
    ik_ref[...] = ikr[:, :IDX_DIM]
    ikz = jnp.where(lane < IDX_DIM, ikr, 0.0)
    ik2_ref[:, :LANES] = ikz.astype(BF16)
    ik2_ref[:, LANES:] = pltpu.roll(ikz, IDX_DIM, 1).astype(BF16)
    iw_ref[...] = y[:, IDX_DIM:IDX_DIM + N_IDX_HEADS] * ((IDX_DIM ** -0.5) * (N_IDX_HEADS ** -0.5))

    ga_ref[...] = _silu(_dot(h, w_ref[:, A_GA:A_END])).astype(BF16)


def _proj_a(x, g, w_a, tabs, tm):
    m = x.shape[0]
    c128, s128, c64, s64 = tabs
    n_pt = c128.shape[0] // tm
    row = lambda w: pl.BlockSpec((tm, w), lambda i: (i, 0))
    tab = pl.BlockSpec((tm, LANES), lambda i: (i % n_pt, 0))
    out_shape = (
        jax.ShapeDtypeStruct((m, ATTN_W), BF16),
        jax.ShapeDtypeStruct((m, N_KV_HEADS, HEAD_DIM), F32),
        jax.ShapeDtypeStruct((m, N_KV_HEADS, HEAD_DIM), F32),
        jax.ShapeDtypeStruct((m, KV_W), BF16),
        jax.ShapeDtypeStruct((m, KV_W), BF16),
        jax.ShapeDtypeStruct((m, IDX_Q_W), BF16),
        jax.ShapeDtypeStruct((m, IDX_DIM), F32),
        jax.ShapeDtypeStruct((m, 2 * LANES), BF16),
        jax.ShapeDtypeStruct((m, N_IDX_HEADS), F32),
        jax.ShapeDtypeStruct((m, ATTN_W), BF16),
    )
    kv_heads = pl.BlockSpec((tm, N_KV_HEADS, HEAD_DIM), lambda i: (i, 0, 0))
    out_specs = (row(ATTN_W), kv_heads, kv_heads, row(KV_W), row(KV_W), row(IDX_Q_W),
                 row(IDX_DIM), row(2 * LANES), row(N_IDX_HEADS), row(ATTN_W))
    return pl.pallas_call(
        _proj_a_kernel,
        grid=(m // tm,),
        in_specs=[row(D_MODEL), _const_spec((1, D_MODEL)), _const_spec((D_MODEL, A_END)),
                  tab, tab, tab, tab],
        out_specs=out_specs,
        out_shape=out_shape,
        compiler_params=pltpu.CompilerParams(dimension_semantics=("arbitrary",),
                                             vmem_limit_bytes=VMEM_LIMIT),
        name="proj_a",
    )(x, g, w_a, c128, s128, c64, s64)


def _attend_kernel(q_ref, iq_ref, iwt_ref, kb_ref, vb_ref, ik2_ref, o_ref,
                   st_ref, sbuf_ref, state_ref, acc_ref,
                   *, tq_real, q0, topk):
    tq = Q_TILE
    kb_sz = KEY_BLOCK
    i = pl.program_id(1)
    qpos0 = q0 + i * tq_real
    n_keys = ((qpos0 + tq_real - 1) // CHUNK + 1) * CHUNK
    nkb = (n_keys + kb_sz - 1) // kb_sz

    lane_q = lax.broadcasted_iota(jnp.int32, (1, tq), 1)
    n_adm_i = ((qpos0 + lane_q) // CHUNK + 1) * CHUNK
    iw = iwt_ref[0]

    def score_blk(kb, carry):
        mn, mx = carry
        r0 = pl.multiple_of(kb * kb_sz, kb_sz)
        ik = ik2_ref[0, pl.ds(r0, kb_sz), :]
        ik_even = ik[:, :LANES]
        ik_odd = ik[:, LANES:]
        acc = jnp.zeros((kb_sz, tq), F32)
        for p in range(N_IDX_HEADS // 2):
            iq_pair = iq_ref[0, :, p * LANES:(p + 1) * LANES]
            d0 = _dot_nt(ik_even, iq_pair)
            d1 = _dot_nt(ik_odd, iq_pair)
            acc = acc + jnp.maximum(d0, 0.0) * iw[2 * p:2 * p + 1]
            acc = acc + jnp.maximum(d1, 0.0) * iw[2 * p + 1:2 * p + 2]
        key = r0 + lax.broadcasted_iota(jnp.int32, (kb_sz, 1), 0)
        masked = jnp.where(key < n_adm_i, acc, -jnp.inf)
        st_ref[pl.ds(r0, kb_sz), :] = masked
        mx = jnp.maximum(mx, jnp.max(masked, axis=0, keepdims=True))
        mn = jnp.minimum(mn, jnp.min(acc, axis=0, keepdims=True))
        return mn, mx

    mn, mx = lax.fori_loop(0, nkb, score_blk,
                           (jnp.full((1, tq), jnp.inf, F32), jnp.full((1, tq), -jnp.inf, F32)))

    n_adm = n_adm_i.astype(F32)
    topk_f = float(topk)
    active0 = jnp.logical_and(n_adm > topk_f, lane_q < tq_real).astype(F32)
    state_ref[0:1, :] = mn
    state_ref[1:2, :] = mx + jnp.maximum(jnp.abs(mx), 1e-30)
    state_ref[2:3, :] = jnp.full((1, tq), F32_LOWEST, F32)
    state_ref[3:4, :] = active0
    state_ref[4:5, :] = n_adm
    state_ref[5:6, :] = jnp.zeros((1, tq), F32)
    state_ref[6:7, :] = jnp.zeros((1, tq), F32)

    def count_ge(mid):
        def cnt_blk(kb, acc):
            r0 = pl.multiple_of(kb * kb_sz, kb_sz)
            ge = (st_ref[pl.ds(r0, kb_sz), :] >= mid).astype(F32)
            return acc + jnp.sum(ge.reshape(COUNT_CHAINS, kb_sz // (8 * COUNT_CHAINS), 8, tq), axis=1)

        acc = lax.fori_loop(0, nkb, cnt_blk, jnp.zeros((COUNT_CHAINS, 8, tq), F32))
        return jnp.sum(jnp.sum(acc, axis=0), axis=0, keepdims=True)

    def bisect_step(st, interpolate):
        lo, hi, thr, active, clo, tie, chi = st
        mid = 0.5 * lo + 0.5 * hi
        stuck = jnp.logical_or(mid <= lo, mid >= hi)
        if interpolate:
            guess = lo + (hi - lo) * ((clo - topk_f) / (clo - chi))
            mid = jnp.where(jnp.logical_and(guess > lo, guess < hi), guess, mid)
        cnt = count_ge(mid)
        ge_k = cnt >= topk_f
        exact = cnt == topk_f
        moving = jnp.logical_and(active, jnp.logical_not(stuck))
        ended = jnp.logical_and(active, stuck)
        thr = jnp.where(ended, lo, jnp.where(jnp.logical_and(moving, exact), mid, thr))
        tie = jnp.logical_or(tie, jnp.logical_and(ended, clo > topk_f))
        up = jnp.logical_and(moving, ge_k)
        down = jnp.logical_and(moving, jnp.logical_not(ge_k))
        return (jnp.where(up, mid, lo), jnp.where(down, mid, hi), thr,
                jnp.logical_and(moving, jnp.logical_not(exact)), jnp.where(up, cnt, clo), tie,
                jnp.where(down, cnt, chi))

    def bisect_cond(c):
        return jnp.logical_and(c[0] > 0.0, c[1] < 128)

    def bisect_body(c):
        st = (state_ref[0:1, :], state_ref[1:2, :], state_ref[2:3, :], state_ref[3:4, :] > 0.0,
              state_ref[4:5, :], state_ref[5:6, :] > 0.0, state_ref[6:7, :])
        for step in range(BISECT_UNROLL):
            st = bisect_step(st, interpolate=step % 2 == 1)
        active_f = st[3].astype(F32)
        state_ref[0:1, :] = st[0]
        state_ref[1:2, :] = st[1]
        state_ref[2:3, :] = st[2]
        state_ref[3:4, :] = active_f
        state_ref[4:5, :] = st[4]
        state_ref[5:6, :] = st[5].astype(F32)
        state_ref[6:7, :] = st[6]
        return jnp.max(active_f), c[1] + 1

    lax.while_loop(bisect_cond, bisect_body, (jnp.max(active0), jnp.int32(0)))
    thr = state_ref[2:3, :]

    @pl.when(jnp.max(state_ref[5:6, :]) > 0.0)
    def _():
        tie_row = state_ref[5:6, :] > 0.0

        def gt_blk(kb, acc):
            r0 = pl.multiple_of(kb * kb_sz, kb_sz)
            return acc + jnp.sum((st_ref[pl.ds(r0, kb_sz), :] > thr).astype(F32), axis=0, keepdims=True)

        need = topk_f - lax.fori_loop(0, nkb, gt_blk, jnp.zeros((1, tq), F32))
        tri = (lax.broadcasted_iota(jnp.int32, (kb_sz, kb_sz), 0)
               >= lax.broadcasted_iota(jnp.int32, (kb_sz, kb_sz), 1)).astype(BF16)

        def fix_blk(kb, run):
            r0 = pl.multiple_of(kb * kb_sz, kb_sz)
            blk = st_ref[pl.ds(r0, kb_sz), :]
            eq = jnp.logical_and(blk == thr, tie_row)
            eq_f = eq.astype(F32)
            rank = _dot(tri, eq_f.astype(BF16)) - eq_f + run
            st_ref[pl.ds(r0, kb_sz), :] = jnp.where(jnp.logical_and(eq, rank >= need), -jnp.inf, blk)
            return run + jnp.sum(eq_f, axis=0, keepdims=True)

        lax.fori_loop(0, nkb, fix_blk, jnp.zeros((1, tq), F32))

    red = lambda a, op: op(a.reshape(2, kb_sz // 16, 8, a.shape[-1]), axis=1)
    gw = GROUP * tq
    qs = [jnp.concatenate(
        [q_ref[0, :, (GROUP * j + g) * HEAD_DIM:(GROUP * j + g + 1) * HEAD_DIM] for g in range(GROUP)],
        axis=0) for j in range(N_KV_HEADS)]

    def qk_blk(kb, m8):
        r0 = pl.multiple_of(kb * kb_sz, kb_sz)
        bias = jnp.where(st_ref[pl.ds(r0, kb_sz), :] >= thr, 0.0, NEG_BIAS)
        parts = []
        for j in range(N_KV_HEADS):
            s = _dot_nt(kb_ref[0, pl.ds(r0, kb_sz), j * HEAD_DIM:(j + 1) * HEAD_DIM], qs[j])
            for g in range(GROUP):
                sg = s[:, g * tq:(g + 1) * tq] + bias
                sbuf_ref[kb, :, j * gw + g * tq:j * gw + (g + 1) * tq] = sg
                parts.append(red(sg, jnp.max))
        return jnp.maximum(m8, jnp.concatenate(parts, axis=-1))

    m8 = lax.fori_loop(0, nkb, qk_blk, jnp.full((2, 8, N_KV_HEADS * gw), NEG_BIAS, F32))
    m = jnp.max(jnp.max(m8, axis=0), axis=0, keepdims=True)
    acc_ref[...] = jnp.zeros(acc_ref.shape, F32)

    def pv_blk(kb, l8):
        r0 = pl.multiple_of(kb * kb_sz, kb_sz)
        sums = []
        for j in range(N_KV_HEADS):
            cols = slice(j * gw, (j + 1) * gw)
            p = jnp.exp2(sbuf_ref[kb, :, cols] - m[:, cols])
            acc_ref[j] += lax.dot_general(vb_ref[0, pl.ds(r0, kb_sz), j * HEAD_DIM:(j + 1) * HEAD_DIM],
                                          p.astype(BF16), (((0,), (0,)), ((), ())), preferred_element_type=F32)
            sums.append(red(p, jnp.sum))
        return l8 + jnp.concatenate(sums, axis=-1)

    l8 = lax.fori_loop(0, nkb, pv_blk, jnp.zeros((2, 8, N_KV_HEADS * gw), F32))
    denom = jnp.sum(jnp.sum(l8, axis=0), axis=0, keepdims=True)
    for j in range(N_KV_HEADS):
        o = acc_ref[j] / denom[:, j * gw:(j + 1) * gw]
        for g in range(GROUP):
            hd = GROUP * j + g
            o_ref[0, :, hd * HEAD_DIM:(hd + 1) * HEAD_DIM] = o[:, g * tq:(g + 1) * tq].T.astype(o_ref.dtype)


def _attend(q, iq, iwt, kb, vb, ik2, *, tq_real, q0, topk):
    b, t_q, _ = q.shape
    l_keys = kb.shape[1]
    n_q = t_q // Q_TILE
    n_kb = l_keys // KEY_BLOCK
    qspec = pl.BlockSpec((1, Q_TILE, ATTN_W), lambda bi, i: (bi, i, 0))
    kspec = pl.BlockSpec((1, l_keys, KV_W), lambda bi, i: (bi, 0, 0))
    return pl.pallas_call(
        functools.partial(_attend_kernel, tq_real=tq_real, q0=q0, topk=topk),
        grid=(b, n_q),
        in_specs=[qspec, qspec, pl.BlockSpec((1, N_IDX_HEADS, Q_TILE), lambda bi, i: (bi, 0, i)),
                  kspec, kspec, kspec],
        out_specs=qspec,
        out_shape=jax.ShapeDtypeStruct((b, t_q, ATTN_W), F32),
        scratch_shapes=[
            pltpu.VMEM((l_keys, Q_TILE), F32),
            pltpu.VMEM((n_kb, KEY_BLOCK, N_HEADS * Q_TILE), F32),
            pltpu.VMEM((8, Q_TILE), F32),
            pltpu.VMEM((N_KV_HEADS, HEAD_DIM, GROUP * Q_TILE), F32),
        ],
        compiler_params=pltpu.CompilerParams(dimension_semantics=("arbitrary", "arbitrary"),
                                             vmem_limit_bytes=VMEM_LIMIT),
        name="attend",
    )(q, iq, iwt, kb, vb, ik2)


def _proj_b_kernel(x_ref, g_ref, w_ref, convw_ref, convb_ref, poolw_ref, pscale_ref, liftb_ref, liftc_ref,
                   chist_ref, phist_ref, zbc_ref, cstate_ref, pstate_ref, cin_ext, pin_ext,
                   *, tm, pos0):
    j = pl.program_id(1)

    @pl.when(j == 0)
    def _():
        cin_ext[0:CONV_PAD, :] = chist_ref[0]
        pin_ext[0:POOL_PAD, :] = phist_ref[0]

    @pl.when(j > 0)
    def _():
        cin_ext[0:CONV_PAD, :] = cin_ext[tm:tm + CONV_PAD, :]
        pin_ext[0:POOL_PAD, :] = pin_ext[tm:tm + POOL_PAD, :]

    h = _rms_h(x_ref[...], g_ref[...])

    y = _dot(h, w_ref[:, B_U:B_PIN])
    u = y[:, 0:CONV_W]
    b_gate = y[:, CONV_W:2 * CONV_W]
    c_gate = y[:, 2 * CONV_W:3 * CONV_W]
    gate_b = y[:, 3 * CONV_W:4 * CONV_W]
    cin = c_gate * u
    cin_ext[CONV_PAD:CONV_PAD + tm, :] = cin
    conv = (cin_ext[CONV_PAD - 2:CONV_PAD - 2 + tm, :] * convw_ref[0:1, :]
            + cin_ext[CONV_PAD - 1:CONV_PAD - 1 + tm, :] * convw_ref[1:2, :]
            + cin * convw_ref[2:3, :] + convb_ref[...])
    y_b = b_gate * conv * _silu(gate_b)
    zb = _dot(y_b.astype(BF16), liftb_ref[...])

    y = _dot(h, w_ref[:, B_PIN:B_MB])
    p_in = y[:, 0:POOL_W]
    gate_c = y[:, POOL_W:2 * POOL_W]
    pin_ext[POOL_PAD:POOL_PAD + tm, :] = p_in
    pos = pos0 + j * tm + lax.broadcasted_iota(jnp.int32, (tm, 1), 0)
    yc_parts = []
    for gi, win in enumerate(POOL_WINDOWS):
        sl = slice(gi * POOL_GROUP, (gi + 1) * POOL_GROUP)
        tot = p_in[:, sl]
        for back in range(1, win):
            tot = tot + pin_ext[POOL_PAD - back:POOL_PAD - back + tm, sl]
        cnt = jnp.minimum(win, pos + 1).astype(F32)
        d = tot / cnt - p_in[:, sl]
        mixed = _dot(d.astype(BF16), poolw_ref[gi])
        yc_parts.append(mixed * pscale_ref[:, sl] * _silu(gate_c[:, sl]))
    y_c = jnp.concatenate(yc_parts, axis=-1)
    zc = _dot(y_c.astype(BF16), liftc_ref[...])

    m_b = _dot(h, w_ref[:, B_MB:B_MC])
    m_c = _dot(h, w_ref[:, B_MC:B_END])
    zbc_ref[...] = jax.nn.sigmoid(m_b) * zb + jax.nn.sigmoid(m_c) * zc

    @pl.when(j == pl.num_programs(1) - 1)
    def _():
        cstate_ref[0] = cin_ext[tm:tm + CONV_PAD, :]
        pstate_ref[0] = pin_ext[tm:tm + POOL_PAD, :]


def _proj_b(x, g, w_b, conv_w, conv_b, pool_w, pool_scale, lift_b, lift_c, chist, phist, *, seq_len, tm, pos0):
    m = x.shape[0]
    n_seq = m // seq_len
    n_t = seq_len // tm
    out_shape = (
        jax.ShapeDtypeStruct((m, D_MODEL), F32),
        jax.ShapeDtypeStruct((n_seq, CONV_PAD, CONV_W), F32),
        jax.ShapeDtypeStruct((n_seq, POOL_PAD, POOL_W), F32),
    )
    hist = lambda r, w: pl.BlockSpec((1, r, w), lambda s, j: (s, 0, 0))
    return pl.pallas_call(
        functools.partial(_proj_b_kernel, tm=tm, pos0=pos0),
        grid=(n_seq, n_t),
        in_specs=[pl.BlockSpec((tm, D_MODEL), lambda s, j: (s * n_t + j, 0)),
                  _const_spec((1, D_MODEL)), _const_spec((D_MODEL, B_END)),
                  _const_spec((CONV_K, CONV_W)), _const_spec((1, CONV_W)),
                  _const_spec((len(POOL_WINDOWS), POOL_GROUP, POOL_GROUP)), _const_spec((1, POOL_W)),
                  _const_spec((CONV_W, D_MODEL)), _const_spec((POOL_W, D_MODEL)),
                  hist(CONV_PAD, CONV_W), hist(POOL_PAD, POOL_W)],
        out_specs=(pl.BlockSpec((tm, D_MODEL), lambda s, j: (s * n_t + j, 0)),
                   hist(CONV_PAD, CONV_W), hist(POOL_PAD, POOL_W)),
        out_shape=out_shape,
        scratch_shapes=[pltpu.VMEM((tm + CONV_PAD, CONV_W), F32),
                        pltpu.VMEM((tm + POOL_PAD, POOL_W), F32)],
        compiler_params=pltpu.CompilerParams(dimension_semantics=("arbitrary", "arbitrary"),
                                             vmem_limit_bytes=VMEM_LIMIT),
        name="proj_b",
    )(x, g, w_b, conv_w, conv_b, pool_w, pool_scale, lift_b, lift_c, chist, phist)


def _merge_kernel(x_ref, g_ref, wma_ref, attn_ref, ga_ref, lifta_ref, zbc_ref, wout_ref, fg_ref,
                  *out_refs, final):
    x = x_ref[...]
    h = _rms_h(x, g_ref[...])
    y_a = attn_ref[...] * ga_ref[...].astype(F32)
    z = jax.nn.sigmoid(_dot(h, wma_ref[...])) * _dot(y_a.astype(BF16), lifta_ref[...]) + zbc_ref[...]
    out = x + _dot(z.astype(BF16), wout_ref[...])
    out_refs[0][...] = out
    if final:
        out_refs[1][...] = out * lax.rsqrt(jnp.mean(out * out, axis=-1, keepdims=True) + EPS) * fg_ref[...]


def _merge(x, g, w_ma, attn, ga, lift_a, zbc, w_out, final_g, *, tm, final):
    m = x.shape[0]
    row = lambda w: pl.BlockSpec((tm, w), lambda i: (i, 0))
    n_out = 2 if final else 1
    return pl.pallas_call(
        functools.partial(_merge_kernel, final=final),
        grid=(m // tm,),
        in_specs=[row(D_MODEL), _const_spec((1, D_MODEL)), _const_spec((D_MODEL, D_MODEL)),
                  row(ATTN_W), row(ATTN_W), _const_spec((ATTN_W, D_MODEL)), row(D_MODEL),
                  _const_spec((D_MODEL, D_MODEL)), _const_spec((1, D_MODEL))],
        out_specs=tuple(row(D_MODEL) for _ in range(n_out)),
        out_shape=tuple(jax.ShapeDtypeStruct((m, D_MODEL), F32) for _ in range(n_out)),
        compiler_params=pltpu.CompilerParams(dimension_semantics=("arbitrary",),
                                             vmem_limit_bytes=VMEM_LIMIT),
        name="merge",
    )(x, g, w_ma, attn, ga, lift_a, zbc, w_out, final_g)


def _rope_tables(pos):
    def tab(half, reps):
        inv = ROPE_THETA ** (-jnp.arange(half, dtype=F32) / half)
        ang = pos.astype(F32)[:, None] * inv[None, :]
        cos, sin = jnp.cos(ang), jnp.sin(ang)
        return (jnp.tile(jnp.concatenate([cos, cos], axis=-1), (1, reps)),
                jnp.tile(jnp.concatenate([-sin, sin], axis=-1), (1, reps)))
    c128, s128 = tab(HEAD_DIM // 2, 1)
    c64, s64 = tab(IDX_DIM // 2, LANES // IDX_DIM)
    return c128, s128, c64, s64


def _layer(x, tabs, lw, hist, cache, *, n_seq, seq_len, pos0, topk, tm_tok, tm_seq, final, final_g):
    m = n_seq * seq_len
    q, k, v, kb, vb, iq, ik, ik2, iw, ga = _proj_a(x, lw["g"], lw["w_a"], tabs, tm_tok)

    def seq(a):
        return a.reshape(n_seq, seq_len, a.shape[-1])

    iwt = jnp.swapaxes(seq(iw), 1, 2)
    q3, iq3, kb3, vb3, ik23 = seq(q), seq(iq), seq(kb), seq(vb), seq(ik2)
    if cache is not None:
        ck, cv, cik2 = cache
        l_all = ck.shape[1] + seq_len
        pad_keys = -l_all % KEY_BLOCK
        cat = lambda c, n: jnp.concatenate(
            [c.astype(BF16), n, jnp.zeros((n_seq, pad_keys, n.shape[-1]), BF16)], axis=1)
        kb3, vb3, ik23 = cat(ck, kb3), cat(cv, vb3), cat(cik2, ik23)
    pad_q = -seq_len % Q_TILE
    if pad_q:
        q3 = jnp.pad(q3, ((0, 0), (0, pad_q), (0, 0)))
        iq3 = jnp.pad(iq3, ((0, 0), (0, pad_q), (0, 0)))
        iwt = jnp.pad(iwt, ((0, 0), (0, 0), (0, pad_q)))
    attn = _attend(q3, iq3, iwt, kb3, vb3, ik23, tq_real=min(seq_len, Q_TILE), q0=pos0, topk=topk)
    attn = attn[:, :seq_len].reshape(m, ATTN_W)

    zbc, cstate, pstate = _proj_b(x, lw["g"], lw["w_b"], lw["conv_w"], lw["conv_b"], lw["pool_w"],
                                  lw["pool_scale"], lw["lift_b"], lw["lift_c"], hist[0], hist[1],
                                  seq_len=seq_len, tm=tm_seq, pos0=pos0)
    outs = _merge(x, lw["g"], lw["w_ma"], attn, ga, lw["lift_a"], zbc, lw["w_out"], final_g,
                  tm=tm_tok, final=final)
    states = (k.reshape(n_seq, seq_len, N_KV_HEADS, HEAD_DIM), v.reshape(n_seq, seq_len, N_KV_HEADS, HEAD_DIM),
              ik.reshape(n_seq, seq_len, IDX_DIM), cstate[:, CONV_PAD - (CONV_K - 1):], pstate[:, POOL_PAD - POOL_HIST:])
    return outs, states


def kernel(x_prompt, x_sample, cache_k, cache_v, cache_idx_k, state_conv, state_pool, norm_g, w_in, conv_w,
           conv_b, pool_w, pool_scale, lift_a, lift_b, lift_c, w_out, final_norm_g):
    batch, seq, _ = x_prompt.shape
    dec_batch, dec_seq, _ = x_sample.shape
    depth = w_in.shape[0]
    past_len = cache_k.shape[2]
    topk_prompt = min(MAX_TOPK, seq // 4)
    topk_sample = min(MAX_TOPK, (past_len + dec_seq) // 4)

    tm_p, tm_s = 256, 256
    tabs_p = _rope_tables(jnp.arange(seq, dtype=jnp.int32))
    tabs_s = _rope_tables(past_len + (jnp.arange(tm_s, dtype=jnp.int32) % dec_seq))
    final_g = final_norm_g.reshape(1, D_MODEL)

    hp = x_prompt.reshape(batch * seq, D_MODEL)
    hs = x_sample.reshape(dec_batch * dec_seq, D_MODEL)
    zero_hist = (jnp.zeros((batch, CONV_PAD, CONV_W), F32), jnp.zeros((batch, POOL_PAD, POOL_W), F32))
    p_states, s_states = [], []
    y_p = y_s = None
    for l in range(depth):
        wl = w_in[l]
        lw = {
            "g": norm_g[l].reshape(1, D_MODEL),
            "w_a": jnp.concatenate([wl[:, O_Q:O_GA], jnp.zeros((D_MODEL, A_GA - A_IKW - IDX_DIM - N_IDX_HEADS), F32),
                                    wl[:, O_GA:O_U]], axis=1).astype(BF16),
            "w_b": jnp.concatenate([wl[:, O_U:O_MA], wl[:, O_MB:O_END]], axis=1).astype(BF16),
            "w_ma": wl[:, O_MA:O_MB].astype(BF16),
            "conv_w": conv_w[l], "conv_b": conv_b[l].reshape(1, CONV_W),
            "pool_w": pool_w[l].astype(BF16), "pool_scale": pool_scale[l].reshape(1, POOL_W),
            "lift_a": lift_a[l].astype(BF16), "lift_b": lift_b[l].astype(BF16), "lift_c": lift_c[l].astype(BF16),
            "w_out": w_out[l].astype(BF16),
        }
        final = l == depth - 1
        outs_p, st_p = _layer(hp, tabs_p, lw, zero_hist, None, n_seq=batch, seq_len=seq, pos0=0,
                              topk=topk_prompt, tm_tok=tm_p, tm_seq=tm_p, final=final, final_g=final_g)
        cik = cache_idx_k[l].astype(BF16)
        cik2 = jnp.concatenate([cik, jnp.zeros_like(cik), jnp.zeros_like(cik), cik], axis=-1)
        cache = (cache_k[l].reshape(dec_batch, past_len, KV_W), cache_v[l].reshape(dec_batch, past_len, KV_W), cik2)
        hist_s = (jnp.pad(state_conv[l], ((0, 0), (CONV_PAD - (CONV_K - 1), 0), (0, 0))),
                  jnp.pad(state_pool[l], ((0, 0), (POOL_PAD - POOL_HIST, 0), (0, 0))))
        outs_s, st_s = _layer(hs, tabs_s, lw, hist_s, cache, n_seq=dec_batch, seq_len=dec_seq, pos0=past_len,
                              topk=topk_sample, tm_tok=tm_s, tm_seq=dec_seq, final=final, final_g=final_g)
        hp, hs = outs_p[0], outs_s[0]
        if final:
            y_p, y_s = outs_p[1], outs_s[1]
        p_states.append(st_p)
        s_states.append(st_s)

    stack = lambda sts, i: jnp.stack([st[i] for st in sts])
    return (y_p.reshape(batch, seq, D_MODEL), y_s.reshape(dec_batch, dec_seq, D_MODEL),
            stack(p_states, 0), stack(p_states, 1), stack(p_states, 2), stack(p_states, 3), stack(p_states, 4),
            stack(s_states, 0), stack(s_states, 1), stack(s_states, 2), stack(s_states, 3), stack(s_states, 4))
```

```python
import functools

import jax
import jax.numpy as jnp
from jax import lax
from jax.experimental import pallas as pl
from jax.experimental.pallas import tpu as pltpu

D_MODEL = 2048
CHUNK = 64
N_HEADS = 8
HEAD_DIM = 128
N_KV_HEADS = 2
GROUP = N_HEADS // N_KV_HEADS
ATTN_W = N_HEADS * HEAD_DIM
KV_W = N_KV_HEADS * HEAD_DIM
N_IDX_HEADS = 16
IDX_DIM = 64
IDX_Q_W = N_IDX_HEADS * IDX_DIM
MAX_TOPK = 256
CONV_W = 512
CONV_K = 3
POOL_W = 512
POOL_WINDOWS = (2, 4, 8, 16)
POOL_GROUP = 128
POOL_HIST = 15
ROPE_THETA = 10000.0
EPS = 1e-6

LANES = 128
CONV_PAD = 8
POOL_PAD = 16
KEY_BLOCK = 512
Q_TILE_PROMPT = 256
Q_TILE_SAMPLE = 128
LOGIT_LANES = 1024
NEG_BIAS = -1e30
COUNT_CHAINS = 4
BISECT_UNROLL = 4
LOG2E = 1.4426950408889634
F32_LOWEST = -3.0e38
VMEM_LIMIT = 56 * 1024 * 1024

F32 = jnp.float32
BF16 = jnp.bfloat16

_SIZES = (ATTN_W, KV_W, KV_W, IDX_Q_W, IDX_DIM, N_IDX_HEADS, ATTN_W,
          CONV_W, CONV_W, CONV_W, CONV_W, POOL_W, POOL_W, D_MODEL, D_MODEL, D_MODEL)
_OFFS = [0]
for _s in _SIZES:
    _OFFS.append(_OFFS[-1] + _s)
(O_Q, O_K, O_V, O_IQ, O_IK, O_IW, O_GA, O_U, O_BG, O_CG, O_GB, O_PIN, O_GC, O_MA, O_MB, O_MC, O_END) = _OFFS

A_Q, A_K, A_V, A_IQ, A_IKW, A_GA, A_END = 0, 1024, 1280, 1536, 2560, 2688, 3712
B_U, B_BG, B_CG, B_GB, B_PIN, B_GC, B_MB, B_MC, B_END = 0, 512, 1024, 1536, 2048, 2560, 3072, 5120, 7168


def _dot(a, b):
    return jnp.dot(a, b, preferred_element_type=F32)


def _dot_nt(a, b):
    return lax.dot_general(a, b, (((1,), (1,)), ((), ())), preferred_element_type=F32)


def _rms_h(x, g):
    h = x * lax.rsqrt(jnp.mean(x * x, axis=-1, keepdims=True) + EPS) * g
    return h.astype(BF16)


def _silu(x):
    return x * jax.nn.sigmoid(x)


def _const_spec(shape):
    nd = len(shape)
    return pl.BlockSpec(shape, lambda *_: (0,) * nd, pipeline_mode=pl.Buffered(1))


def _proj_a_kernel(x_ref, g_ref, w_ref, c128_ref, s128_ref, c64_ref, s64_ref,
                   q_ref, k_ref, v_ref, kb_ref, vb_ref, iq_ref, ik_ref, ik2_ref, iw_ref, ga_ref):
    h = _rms_h(x_ref[...], g_ref[...])
    c128 = c128_ref[...]
    s128 = s128_ref[...]
    c64 = c64_ref[...]
    s64 = s64_ref[...]
    lane = lax.broadcasted_iota(jnp.int32, (1, LANES), 1)
    first_half64 = (lane % IDX_DIM) < (IDX_DIM // 2)

    def rope128(y):
        return y * c128 + pltpu.roll(y, HEAD_DIM // 2, 1) * s128

    def rope64(y):
        partner = jnp.where(first_half64, pltpu.roll(y, LANES - IDX_DIM // 2, 1),
                            pltpu.roll(y, IDX_DIM // 2, 1))
        return y * c64 + partner * s64

    y = _dot(h, w_ref[:, A_Q:A_K])
    for hd in range(N_HEADS):
        sl = slice(hd * HEAD_DIM, (hd + 1) * HEAD_DIM)
        q_ref[:, sl] = (rope128(y[:, sl]) * (HEAD_DIM ** -0.5 * LOG2E)).astype(BF16)

    y = _dot(h, w_ref[:, A_K:A_IQ])
    for hd in range(N_KV_HEADS):
        sl = slice(hd * HEAD_DIM, (hd + 1) * HEAD_DIM)
        kr = rope128(y[:, sl])
        k_ref[:, hd, :] = kr
        kb_ref[:, sl] = kr.astype(BF16)
        v_ref[:, hd, :] = y[:, KV_W + hd * HEAD_DIM:KV_W + (hd + 1) * HEAD_DIM]
    vb_ref[...] = y[:, KV_W:].astype(BF16)

    y = _dot(h, w_ref[:, A_IQ:A_IKW])
    for c in range(IDX_Q_W // LANES):
        sl = slice(c * LANES, (c + 1) * LANES)
        iq_ref[:, sl] = rope64(y[:, sl]).astype(BF16)

    y = _dot(h, w_ref[:, A_IKW:A_GA])
    ikr = rope64(y)
    ik_ref[...] = ikr[:, :IDX_DIM]
    ikz = jnp.where(lane < IDX_DIM, ikr, 0.0)
    ik2_ref[:, :LANES] = ikz.astype(BF16)
    ik2_ref[:, LANES:] = pltpu.roll(ikz, IDX_DIM, 1).astype(BF16)
    iw_ref[...] = y[:, IDX_DIM:IDX_DIM + N_IDX_HEADS] * ((IDX_DIM ** -0.5) * (N_IDX_HEADS ** -0.5))

    ga_ref[...] = _silu(_dot(h, w_ref[:, A_GA:A_END])).astype(BF16)


def _proj_a(x, g, w_a, tabs, tm):
    m = x.shape[0]
    c128, s128, c64, s64 = tabs
    n_pt = c128.shape[0] // tm
    row = lambda w: pl.BlockSpec((tm, w), lambda i: (i, 0))
    tab = pl.BlockSpec((tm, LANES), lambda i: (i % n_pt, 0))
    out_shape = (
        jax.ShapeDtypeStruct((m, ATTN_W), BF16),
        jax.ShapeDtypeStruct((m, N_KV_HEADS, HEAD_DIM), F32),
        jax.ShapeDtypeStruct((m, N_KV_HEADS, HEAD_DIM), F32),
        jax.ShapeDtypeStruct((m, KV_W), BF16),
        jax.ShapeDtypeStruct((m, KV_W), BF16),
        jax.ShapeDtypeStruct((m, IDX_Q_W), BF16),
        jax.ShapeDtypeStruct((m, IDX_DIM), F32),
        jax.ShapeDtypeStruct((m, 2 * LANES), BF16),
        jax.ShapeDtypeStruct((m, N_IDX_HEADS), F32),
        jax.ShapeDtypeStruct((m, ATTN_W), BF16),
    )
    kv_heads = pl.BlockSpec((tm, N_KV_HEADS, HEAD_DIM), lambda i: (i, 0, 0))
    out_specs = (row(ATTN_W), kv_heads, kv_heads, row(KV_W), row(KV_W), row(IDX_Q_W),
                 row(IDX_DIM), row(2 * LANES), row(N_IDX_HEADS), row(ATTN_W))
    return pl.pallas_call(
        _proj_a_kernel,
        grid=(m // tm,),
        in_specs=[row(D_MODEL), _const_spec((1, D_MODEL)), _const_spec((D_MODEL, A_END)),
                  tab, tab, tab, tab],
        out_specs=out_specs,
        out_shape=out_shape,
        compiler_params=pltpu.CompilerParams(dimension_semantics=("arbitrary",),
                                             vmem_limit_bytes=VMEM_LIMIT),
        name="proj_a",
    )(x, g, w_a, c128, s128, c64, s64)


def _attend_kernel(q_ref, iq_ref, iwt_ref, kb_ref, vb_ref, ik2_ref, o_ref,
                   st_ref, sbuf_ref, state_ref, acc_ref,
                   *, tq, tq_real, q0, topk):
    kb_sz = KEY_BLOCK
    i = pl.program_id(1)
    qpos0 = q0 + i * tq_real
    n_keys = ((qpos0 + tq_real - 1) // CHUNK + 1) * CHUNK
    nkb = (n_keys + kb_sz - 1) // kb_sz

    lane_q = lax.broadcasted_iota(jnp.int32, (1, tq), 1)
    n_adm_i = ((qpos0 + lane_q) // CHUNK + 1) * CHUNK
    iw = iwt_ref[0]

    def score_blk(kb, carry):
        mn, mx = carry
        r0 = pl.multiple_of(kb * kb_sz, kb_sz)
        ik = ik2_ref[0, pl.ds(r0, kb_sz), :]
        ik_even = ik[:, :LANES]
        ik_odd = ik[:, LANES:]
        acc = jnp.zeros((kb_sz, tq), F32)
        for p in range(N_IDX_HEADS // 2):
            iq_pair = iq_ref[0, :, p * LANES:(p + 1) * LANES]
            d0 = _dot_nt(ik_even, iq_pair)
            d1 = _dot_nt(ik_odd, iq_pair)
            acc = acc + jnp.maximum(d0, 0.0) * iw[2 * p:2 * p + 1]
            acc = acc + jnp.maximum(d1, 0.0) * iw[2 * p + 1:2 * p + 2]
        key = r0 + lax.broadcasted_iota(jnp.int32, (kb_sz, 1), 0)
        masked = jnp.where(key < n_adm_i, acc, -jnp.inf)
        st_ref[pl.ds(r0, kb_sz), :] = masked
        mx = jnp.maximum(mx, jnp.max(masked, axis=0, keepdims=True))
        mn = jnp.minimum(mn, jnp.min(acc, axis=0, keepdims=True))
        return mn, mx

    mn, mx = lax.fori_loop(0, nkb, score_blk,
                           (jnp.full((1, tq), jnp.inf, F32), jnp.full((1, tq), -jnp.inf, F32)))

    n_adm = n_adm_i.astype(F32)
    topk_f = float(topk)
    active0 = jnp.logical_and(n_adm > topk_f, lane_q < tq_real).astype(F32)
    state_ref[0:1, :] = mn
    state_ref[1:2, :] = mx + jnp.maximum(jnp.abs(mx), 1e-30)
    state_ref[2:3, :] = jnp.full((1, tq), F32_LOWEST, F32)
    state_ref[3:4, :] = active0
    state_ref[4:5, :] = n_adm
    state_ref[5:6, :] = jnp.zeros((1, tq), F32)
    state_ref[6:7, :] = jnp.zeros((1, tq), F32)

    def count_ge(mid):
        def cnt_blk(kb, acc):
            r0 = pl.multiple_of(kb * kb_sz, kb_sz)
            ge = (st_ref[pl.ds(r0, kb_sz), :] >= mid).astype(F32)
            return acc + jnp.sum(ge.reshape(COUNT_CHAINS, kb_sz // (8 * COUNT_CHAINS), 8, tq), axis=1)

        acc = lax.fori_loop(0, nkb, cnt_blk, jnp.zeros((COUNT_CHAINS, 8, tq), F32))
        return jnp.sum(jnp.sum(acc, axis=0), axis=0, keepdims=True)

    def bisect_step(st, interpolate):
        lo, hi, thr, active, clo, tie, chi = st
        mid = 0.5 * lo + 0.5 * hi
        stuck = jnp.logical_or(mid <= lo, mid >= hi)
        if interpolate:
            guess = lo + (hi - lo) * ((clo - topk_f) / (clo - chi))
            mid = jnp.where(jnp.logical_and(guess > lo, guess < hi), guess, mid)
        cnt = count_ge(mid)
        ge_k = cnt >= topk_f
        exact = cnt == topk_f
        moving = jnp.logical_and(active, jnp.logical_not(stuck))
        ended = jnp.logical_and(active, stuck)
        thr = jnp.where(ended, lo, jnp.where(jnp.logical_and(moving, exact), mid, thr))
        tie = jnp.logical_or(tie, jnp.logical_and(ended, clo > topk_f))
        up = jnp.logical_and(moving, ge_k)
        down = jnp.logical_and(moving, jnp.logical_not(ge_k))
        return (jnp.where(up, mid, lo), jnp.where(down, mid, hi), thr,
                jnp.logical_and(moving, jnp.logical_not(exact)), jnp.where(up, cnt, clo), tie,
                jnp.where(down, cnt, chi))

    def bisect_cond(c):
        return jnp.logical_and(c[0] > 0.0, c[1] < 128)

    def bisect_body(c):
        st = (state_ref[0:1, :], state_ref[1:2, :], state_ref[2:3, :], state_ref[3:4, :] > 0.0,
              state_ref[4:5, :], state_ref[5:6, :] > 0.0, state_ref[6:7, :])
        for step in range(BISECT_UNROLL):
            st = bisect_step(st, interpolate=step % 2 == 1)
        active_f = st[3].astype(F32)
        state_ref[0:1, :] = st[0]
        state_ref[1:2, :] = st[1]
        state_ref[2:3, :] = st[2]
        state_ref[3:4, :] = active_f
        state_ref[4:5, :] = st[4]
        state_ref[5:6, :] = st[5].astype(F32)
        state_ref[6:7, :] = st[6]
        return jnp.max(active_f), c[1] + 1

    lax.while_loop(bisect_cond, bisect_body, (jnp.max(active0), jnp.int32(0)))
    thr = state_ref[2:3, :]

    @pl.when(jnp.max(state_ref[5:6, :]) > 0.0)
    def _():
        tie_row = state_ref[5:6, :] > 0.0

        def gt_blk(kb, acc):
            r0 = pl.multiple_of(kb * kb_sz, kb_sz)
            return acc + jnp.sum((st_ref[pl.ds(r0, kb_sz), :] > thr).astype(F32), axis=0, keepdims=True)

        need = topk_f - lax.fori_loop(0, nkb, gt_blk, jnp.zeros((1, tq), F32))
        tri = (lax.broadcasted_iota(jnp.int32, (kb_sz, kb_sz), 0)
               >= lax.broadcasted_iota(jnp.int32, (kb_sz, kb_sz), 1)).astype(BF16)

        def fix_blk(kb, run):
            r0 = pl.multiple_of(kb * kb_sz, kb_sz)
            blk = st_ref[pl.ds(r0, kb_sz), :]
            eq = jnp.logical_and(blk == thr, tie_row)
            eq_f = eq.astype(F32)
            rank = _dot(tri, eq_f.astype(BF16)) - eq_f + run
            st_ref[pl.ds(r0, kb_sz), :] = jnp.where(jnp.logical_and(eq, rank >= need), -jnp.inf, blk)
            return run + jnp.sum(eq_f, axis=0, keepdims=True)

        lax.fori_loop(0, nkb, fix_blk, jnp.zeros((1, tq), F32))

    red = lambda a, op: op(a.reshape(2, kb_sz // 16, 8, a.shape[-1]), axis=1)
    gw = GROUP * tq
    heads_per_trip = LOGIT_LANES // gw
    for j0 in range(0, N_KV_HEADS, heads_per_trip):
        trip_heads = range(j0, j0 + heads_per_trip)
        qs = {j: jnp.concatenate(
            [q_ref[0, :, (GROUP * j + g) * HEAD_DIM:(GROUP * j + g + 1) * HEAD_DIM] for g in range(GROUP)],
            axis=0) for j in trip_heads}

        def qk_blk(kb, m8):
            r0 = pl.multiple_of(kb * kb_sz, kb_sz)
            bias = jnp.where(st_ref[pl.ds(r0, kb_sz), :] >= thr, 0.0, NEG_BIAS)
            parts = []
            for j in trip_heads:
                s = _dot_nt(kb_ref[0, pl.ds(r0, kb_sz), j * HEAD_DIM:(j + 1) * HEAD_DIM], qs[j])
                for g in range(GROUP):
                    sg = s[:, g * tq:(g + 1) * tq] + bias
                    c0 = (j - j0) * gw + g * tq
                    sbuf_ref[kb, :, c0:c0 + tq] = sg
                    parts.append(red(sg, jnp.max))
            return jnp.maximum(m8, jnp.concatenate(parts, axis=-1))

        m8 = lax.fori_loop(0, nkb, qk_blk, jnp.full((2, 8, LOGIT_LANES), NEG_BIAS, F32))
        m = jnp.max(jnp.max(m8, axis=0), axis=0, keepdims=True)
        acc_ref[...] = jnp.zeros(acc_ref.shape, F32)

        def pv_blk(kb, l8):
            r0 = pl.multiple_of(kb * kb_sz, kb_sz)
            sums = []
            for j in trip_heads:
                cols = slice((j - j0) * gw, (j - j0 + 1) * gw)
                p = jnp.exp2(sbuf_ref[kb, :, cols] - m[:, cols])
                acc_ref[:, cols] += lax.dot_general(
                    vb_ref[0, pl.ds(r0, kb_sz), j * HEAD_DIM:(j + 1) * HEAD_DIM], p.astype(BF16),
                    (((0,), (0,)), ((), ())), preferred_element_type=F32)
                sums.append(red(p, jnp.sum))
            return l8 + jnp.concatenate(sums, axis=-1)

        l8 = lax.fori_loop(0, nkb, pv_blk, jnp.zeros((2, 8, LOGIT_LANES), F32))
        o = acc_ref[...] / jnp.sum(jnp.sum(l8, axis=0), axis=0, keepdims=True)
        for j in trip_heads:
            for g in range(GROUP):
                hd = GROUP * j + g
                c0 = (j - j0) * gw + g * tq
                o_ref[0, :, hd * HEAD_DIM:(hd + 1) * HEAD_DIM] = o[:, c0:c0 + tq].T.astype(o_ref.dtype)


def _attend(q, iq, iwt, kb, vb, ik2, *, tq, tq_real, q0, topk):
    b, t_q, _ = q.shape
    l_keys = kb.shape[1]
    n_q = t_q // tq
    n_kb = l_keys // KEY_BLOCK
    qspec = pl.BlockSpec((1, tq, ATTN_W), lambda bi, i: (bi, i, 0))
    kspec = pl.BlockSpec((1, l_keys, KV_W), lambda bi, i: (bi, 0, 0))
    return pl.pallas_call(
        functools.partial(_attend_kernel, tq=tq, tq_real=tq_real, q0=q0, topk=topk),
        grid=(b, n_q),
        in_specs=[qspec, qspec, pl.BlockSpec((1, N_IDX_HEADS, tq), lambda bi, i: (bi, 0, i)),
                  kspec, kspec, kspec],
        out_specs=qspec,
        out_shape=jax.ShapeDtypeStruct((b, t_q, ATTN_W), F32),
        scratch_shapes=[
            pltpu.VMEM((l_keys, tq), F32),
            pltpu.VMEM((n_kb, KEY_BLOCK, LOGIT_LANES), F32),
            pltpu.VMEM((8, tq), F32),
            pltpu.VMEM((HEAD_DIM, LOGIT_LANES), F32),
        ],
        compiler_params=pltpu.CompilerParams(dimension_semantics=("arbitrary", "arbitrary"),
                                             vmem_limit_bytes=VMEM_LIMIT),
        name="attend",
    )(q, iq, iwt, kb, vb, ik2)


def _pack_cache_kernel(c_ref, o_ref, *, n_src):
    j = pl.program_id(1)

    @pl.when(j < n_src)
    def _():
        for hd in range(N_KV_HEADS):
            o_ref[0, :, hd * HEAD_DIM:(hd + 1) * HEAD_DIM] = c_ref[0, :, hd, :].astype(BF16)

    @pl.when(j >= n_src)
    def _():
        o_ref[...] = jnp.zeros(o_ref.shape, BF16)


def _pack_cache(c, l_pad):
    b, past = c.shape[:2]
    n_src = past // KEY_BLOCK
    return pl.pallas_call(
        functools.partial(_pack_cache_kernel, n_src=n_src),
        grid=(b, l_pad // KEY_BLOCK),
        in_specs=[pl.BlockSpec((1, KEY_BLOCK, N_KV_HEADS, HEAD_DIM),
                               lambda bi, j: (bi, jnp.minimum(j, n_src - 1), 0, 0))],
        out_specs=pl.BlockSpec((1, KEY_BLOCK, KV_W), lambda bi, j: (bi, j, 0)),
        out_shape=jax.ShapeDtypeStruct((b, l_pad, KV_W), BF16),
        compiler_params=pltpu.CompilerParams(dimension_semantics=("arbitrary", "arbitrary")),
        name="pack_cache",
    )(c)


def _proj_b_kernel(x_ref, g_ref, w_ref, convw_ref, convb_ref, poolw_ref, pscale_ref, liftb_ref, liftc_ref,
                   chist_ref, phist_ref, zbc_ref, cstate_ref, pstate_ref, cin_ext, pin_ext,
                   *, n_sub, ls, pos0):
    j = pl.program_id(1)

    @pl.when(j == 0)
    def _():
        cin_ext[:, 0:CONV_PAD, :] = chist_ref[...]
        pin_ext[:, 0:POOL_PAD, :] = phist_ref[...]

    @pl.when(j > 0)
    def _():
        cin_ext[:, 0:CONV_PAD, :] = cin_ext[:, ls:ls + CONV_PAD, :]
        pin_ext[:, 0:POOL_PAD, :] = pin_ext[:, ls:ls + POOL_PAD, :]

    h = _rms_h(x_ref[...], g_ref[...])
    rows = lambda a: jnp.concatenate(a, axis=0) if n_sub > 1 else a[0]

    y = _dot(h, w_ref[:, B_U:B_PIN])
    u = y[:, 0:CONV_W]
    b_gate = y[:, CONV_W:2 * CONV_W]
    c_gate = y[:, 2 * CONV_W:3 * CONV_W]
    gate_b = y[:, 3 * CONV_W:4 * CONV_W]
    cin = c_gate * u
    for s in range(n_sub):
        cin_ext[s, CONV_PAD:CONV_PAD + ls, :] = cin[s * ls:(s + 1) * ls]
    conv = (rows([cin_ext[s, CONV_PAD - 2:CONV_PAD - 2 + ls, :] for s in range(n_sub)]) * convw_ref[0:1, :]
            + rows([cin_ext[s, CONV_PAD - 1:CONV_PAD - 1 + ls, :] for s in range(n_sub)]) * convw_ref[1:2, :]
            + cin * convw_ref[2:3, :] + convb_ref[...])
    y_b = b_gate * conv * _silu(gate_b)
    zb = _dot(y_b.astype(BF16), liftb_ref[...])

    y = _dot(h, w_ref[:, B_PIN:B_MB])
    p_in = y[:, 0:POOL_W]
    gate_c = y[:, POOL_W:2 * POOL_W]
    for s in range(n_sub):
        pin_ext[s, POOL_PAD:POOL_PAD + ls, :] = p_in[s * ls:(s + 1) * ls]
    pos = rows([pos0 + j * ls + lax.broadcasted_iota(jnp.int32, (ls, 1), 0)] * n_sub)
    yc_parts = []
    for gi, win in enumerate(POOL_WINDOWS):
        sl = slice(gi * POOL_GROUP, (gi + 1) * POOL_GROUP)
        tot = p_in[:, sl]
        for back in range(1, win):
            tot = tot + rows([pin_ext[s, POOL_PAD - back:POOL_PAD - back + ls, sl] for s in range(n_sub)])
        cnt = jnp.minimum(win, pos + 1).astype(F32)
        d = tot / cnt - p_in[:, sl]
        mixed = _dot(d.astype(BF16), poolw_ref[gi])
        yc_parts.append(mixed * pscale_ref[:, sl] * _silu(gate_c[:, sl]))
    y_c = jnp.concatenate(yc_parts, axis=-1)
    zc = _dot(y_c.astype(BF16), liftc_ref[...])

    m_b = _dot(h, w_ref[:, B_MB:B_MC])
    m_c = _dot(h, w_ref[:, B_MC:B_END])
    zbc_ref[...] = jax.nn.sigmoid(m_b) * zb + jax.nn.sigmoid(m_c) * zc

    @pl.when(j == pl.num_programs(1) - 1)
    def _():
        cstate_ref[...] = cin_ext[:, ls:ls + CONV_PAD, :]
        pstate_ref[...] = pin_ext[:, ls:ls + POOL_PAD, :]


def _proj_b(x, g, w_b, conv_w, conv_b, pool_w, pool_scale, lift_b, lift_c, chist, phist, *, seq_len, tm, pos0):
    m = x.shape[0]
    n_seq = m // seq_len
    n_sub, ls = (1, tm) if tm <= seq_len else (tm // seq_len, seq_len)
    n_t = seq_len // ls
    out_shape = (
        jax.ShapeDtypeStruct((m, D_MODEL), F32),
        jax.ShapeDtypeStruct((n_seq, CONV_PAD, CONV_W), F32),
        jax.ShapeDtypeStruct((n_seq, POOL_PAD, POOL_W), F32),
    )
    hist = lambda r, w: pl.BlockSpec((n_sub, r, w), lambda s, j: (s, 0, 0))
    return pl.pallas_call(
        functools.partial(_proj_b_kernel, n_sub=n_sub, ls=ls, pos0=pos0),
        grid=(n_seq // n_sub, n_t),
        in_specs=[pl.BlockSpec((tm, D_MODEL), lambda s, j: (s * n_t + j, 0)),
                  _const_spec((1, D_MODEL)), _const_spec((D_MODEL, B_END)),
                  _const_spec((CONV_K, CONV_W)), _const_spec((1, CONV_W)),
                  _const_spec((len(POOL_WINDOWS), POOL_GROUP, POOL_GROUP)), _const_spec((1, POOL_W)),
                  _const_spec((CONV_W, D_MODEL)), _const_spec((POOL_W, D_MODEL)),
                  hist(CONV_PAD, CONV_W), hist(POOL_PAD, POOL_W)],
        out_specs=(pl.BlockSpec((tm, D_MODEL), lambda s, j: (s * n_t + j, 0)),
                   hist(CONV_PAD, CONV_W), hist(POOL_PAD, POOL_W)),
        out_shape=out_shape,
        scratch_shapes=[pltpu.VMEM((n_sub, ls + CONV_PAD, CONV_W), F32),
                        pltpu.VMEM((n_sub, ls + POOL_PAD, POOL_W), F32)],
        compiler_params=pltpu.CompilerParams(dimension_semantics=("arbitrary", "arbitrary"),
                                             vmem_limit_bytes=VMEM_LIMIT),
        name="proj_b",
    )(x, g, w_b, conv_w, conv_b, pool_w, pool_scale, lift_b, lift_c, chist, phist)


def _merge_kernel(x_ref, g_ref, wma_ref, attn_ref, ga_ref, lifta_ref, zbc_ref, wout_ref, fg_ref,
                  *out_refs, final):
    x = x_ref[...]
    h = _rms_h(x, g_ref[...])
    y_a = attn_ref[...] * ga_ref[...].astype(F32)
    z = jax.nn.sigmoid(_dot(h, wma_ref[...])) * _dot(y_a.astype(BF16), lifta_ref[...]) + zbc_ref[...]
    out = x + _dot(z.astype(BF16), wout_ref[...])
    out_refs[0][...] = out
    if final:
        out_refs[1][...] = out * lax.rsqrt(jnp.mean(out * out, axis=-1, keepdims=True) + EPS) * fg_ref[...]


def _merge(x, g, w_ma, attn, ga, lift_a, zbc, w_out, final_g, *, tm, final):
    m = x.shape[0]
    row = lambda w: pl.BlockSpec((tm, w), lambda i: (i, 0))
    n_out = 2 if final else 1
    return pl.pallas_call(
        functools.partial(_merge_kernel, final=final),
        grid=(m // tm,),
        in_specs=[row(D_MODEL), _const_spec((1, D_MODEL)), _const_spec((D_MODEL, D_MODEL)),
                  row(ATTN_W), row(ATTN_W), _const_spec((ATTN_W, D_MODEL)), row(D_MODEL),
                  _const_spec((D_MODEL, D_MODEL)), _const_spec((1, D_MODEL))],
        out_specs=tuple(row(D_MODEL) for _ in range(n_out)),
        out_shape=tuple(jax.ShapeDtypeStruct((m, D_MODEL), F32) for _ in range(n_out)),
        compiler_params=pltpu.CompilerParams(dimension_semantics=("arbitrary",),
                                             vmem_limit_bytes=VMEM_LIMIT),
        name="merge",
    )(x, g, w_ma, attn, ga, lift_a, zbc, w_out, final_g)


def _rope_tables(pos):
    def tab(half, reps):
        inv = ROPE_THETA ** (-jnp.arange(half, dtype=F32) / half)
        ang = pos.astype(F32)[:, None] * inv[None, :]
        cos, sin = jnp.cos(ang), jnp.sin(ang)
        return (jnp.tile(jnp.concatenate([cos, cos], axis=-1), (1, reps)),
                jnp.tile(jnp.concatenate([-sin, sin], axis=-1), (1, reps)))
    c128, s128 = tab(HEAD_DIM // 2, 1)
    c64, s64 = tab(IDX_DIM // 2, LANES // IDX_DIM)
    return c128, s128, c64, s64


def _layer(x, tabs, lw, hist, cache, *, n_seq, seq_len, pos0, topk, tm_tok, tm_seq, tq, final, final_g):
    m = n_seq * seq_len
    q, k, v, kb, vb, iq, ik, ik2, iw, ga = _proj_a(x, lw["g"], lw["w_a"], tabs, tm_tok)

    def seq(a):
        return a.reshape(n_seq, seq_len, a.shape[-1])

    iwt = jnp.swapaxes(seq(iw), 1, 2)
    q3, iq3, kb3, vb3, ik23 = seq(q), seq(iq), seq(kb), seq(vb), seq(ik2)
    if cache is not None:
        ck, cv, cik2 = cache
        past = ck.shape[1]
        l_pad = past + seq_len + (-(past + seq_len) % KEY_BLOCK)
        kb3 = lax.dynamic_update_slice(_pack_cache(ck, l_pad), kb3, (0, past, 0))
        vb3 = lax.dynamic_update_slice(_pack_cache(cv, l_pad), vb3, (0, past, 0))
        ik23 = jnp.concatenate([cik2, ik23, jnp.zeros((n_seq, l_pad - past - seq_len, 2 * LANES), BF16)], axis=1)
    pad_q = -seq_len % tq
    if pad_q:
        q3 = jnp.pad(q3, ((0, 0), (0, pad_q), (0, 0)))
        iq3 = jnp.pad(iq3, ((0, 0), (0, pad_q), (0, 0)))
        iwt = jnp.pad(iwt, ((0, 0), (0, 0), (0, pad_q)))
    attn = _attend(q3, iq3, iwt, kb3, vb3, ik23, tq=tq, tq_real=min(seq_len, tq), q0=pos0, topk=topk)
    attn = attn[:, :seq_len].reshape(m, ATTN_W)

    zbc, cstate, pstate = _proj_b(x, lw["g"], lw["w_b"], lw["conv_w"], lw["conv_b"], lw["pool_w"],
                                  lw["pool_scale"], lw["lift_b"], lw["lift_c"], hist[0], hist[1],
                                  seq_len=seq_len, tm=tm_seq, pos0=pos0)
    outs = _merge(x, lw["g"], lw["w_ma"], attn, ga, lw["lift_a"], zbc, lw["w_out"], final_g,
                  tm=tm_tok, final=final)
    states = (k.reshape(n_seq, seq_len, N_KV_HEADS, HEAD_DIM), v.reshape(n_seq, seq_len, N_KV_HEADS, HEAD_DIM),
              ik.reshape(n_seq, seq_len, IDX_DIM), cstate[:, CONV_PAD - (CONV_K - 1):], pstate[:, POOL_PAD - POOL_HIST:])
    return outs, states


def kernel(x_prompt, x_sample, cache_k, cache_v, cache_idx_k, state_conv, state_pool, norm_g, w_in, conv_w,
           conv_b, pool_w, pool_scale, lift_a, lift_b, lift_c, w_out, final_norm_g):
    batch, seq, _ = x_prompt.shape
    dec_batch, dec_seq, _ = x_sample.shape
    depth = w_in.shape[0]
    past_len = cache_k.shape[2]
    topk_prompt = min(MAX_TOPK, seq // 4)
    topk_sample = min(MAX_TOPK, (past_len + dec_seq) // 4)

    tm_p, tm_s = 256, 256
    tabs_p = _rope_tables(jnp.arange(seq, dtype=jnp.int32))
    tabs_s = _rope_tables(past_len + (jnp.arange(tm_s, dtype=jnp.int32) % dec_seq))
    final_g = final_norm_g.reshape(1, D_MODEL)

    hp = x_prompt.reshape(batch * seq, D_MODEL)
    hs = x_sample.reshape(dec_batch * dec_seq, D_MODEL)
    zero_hist = (jnp.zeros((batch, CONV_PAD, CONV_W), F32), jnp.zeros((batch, POOL_PAD, POOL_W), F32))
    p_states, s_states = [], []
    y_p = y_s = None
    for l in range(depth):
        wl = w_in[l]
        lw = {
            "g": norm_g[l].reshape(1, D_MODEL),
            "w_a": jnp.concatenate([wl[:, O_Q:O_GA], jnp.zeros((D_MODEL, A_GA - A_IKW - IDX_DIM - N_IDX_HEADS), F32),
                                    wl[:, O_GA:O_U]], axis=1).astype(BF16),
            "w_b": jnp.concatenate([wl[:, O_U:O_MA], wl[:, O_MB:O_END]], axis=1).astype(BF16),
            "w_ma": wl[:, O_MA:O_MB].astype(BF16),
            "conv_w": conv_w[l], "conv_b": conv_b[l].reshape(1, CONV_W),
            "pool_w": pool_w[l].astype(BF16), "pool_scale": pool_scale[l].reshape(1, POOL_W),
            "lift_a": lift_a[l].astype(BF16), "lift_b": lift_b[l].astype(BF16), "lift_c": lift_c[l].astype(BF16),
            "w_out": w_out[l].astype(BF16),
        }
        final = l == depth - 1
        outs_p, st_p = _layer(hp, tabs_p, lw, zero_hist, None, n_seq=batch, seq_len=seq, pos0=0,
                              topk=topk_prompt, tm_tok=tm_p, tm_seq=tm_p, tq=Q_TILE_PROMPT, final=final,
                              final_g=final_g)
        cik = cache_idx_k[l].astype(BF16)
        cik2 = jnp.concatenate([cik, jnp.zeros_like(cik), jnp.zeros_like(cik), cik], axis=-1)
        cache = (cache_k[l], cache_v[l], cik2)
        hist_s = (jnp.pad(state_conv[l], ((0, 0), (CONV_PAD - (CONV_K - 1), 0), (0, 0))),
                  jnp.pad(state_pool[l], ((0, 0), (POOL_PAD - POOL_HIST, 0), (0, 0))))
        outs_s, st_s = _layer(hs, tabs_s, lw, hist_s, cache, n_seq=dec_batch, seq_len=dec_seq, pos0=past_len,
                              topk=topk_sample, tm_tok=tm_s, tm_seq=tm_s, tq=Q_TILE_SAMPLE, final=final,
                              final_g=final_g)
        hp, hs = outs_p[0], outs_s[0]
        if final:
            y_p, y_s = outs_p[1], outs_s[1]
        p_states.append(st_p)
        s_states.append(st_s)

    stack = lambda sts, i: jnp.stack([st[i] for st in sts])
    return (y_p.reshape(batch, seq, D_MODEL), y_s.reshape(dec_batch, dec_seq, D_MODEL),
            stack(p_states, 0), stack(p_states, 1), stack(p_states, 2), stack(p_states, 3), stack(p_states, 4),
            stack(s_states, 0), stack(s_states, 1), stack(s_states, 2), stack(s_states, 3), stack(s_states, 4))
```

```python
import functools

import jax
import jax.numpy as jnp
from jax import lax
from jax.experimental import pallas as pl
from jax.experimental.pallas import tpu as pltpu

D_MODEL = 2048
CHUNK = 64
N_HEADS = 8
HEAD_DIM = 128
N_KV_HEADS = 2
GROUP = N_HEADS // N_KV_HEADS
ATTN_W = N_HEADS * HEAD_DIM
KV_W = N_KV_HEADS * HEAD_DIM
N_IDX_HEADS = 16
IDX_DIM = 64
IDX_Q_W = N_IDX_HEADS * IDX_DIM
MAX_TOPK = 256
CONV_W = 512
CONV_K = 3
POOL_W = 512
POOL_WINDOWS = (2, 4, 8, 16)
POOL_GROUP = 128
POOL_HIST = 15
ROPE_THETA = 10000.0
EPS = 1e-6

LANES = 128
CONV_PAD = 8
POOL_PAD = 16
KEY_BLOCK = 512
PACK_ROWS = 1024
Q_TILE_PROMPT = 256
Q_TILE_SAMPLE = 128
LOGIT_LANES = 1024
NEG_BIAS = -1e30
COUNT_CHAINS = 4
BISECT_UNROLL = 4
BISECT_UNCHECKED = 3
LOG2E = 1.4426950408889634
F32_LOWEST = -3.0e38
VMEM_LIMIT = 56 * 1024 * 1024

F32 = jnp.float32
BF16 = jnp.bfloat16

_SIZES = (ATTN_W, KV_W, KV_W, IDX_Q_W, IDX_DIM, N_IDX_HEADS, ATTN_W,
          CONV_W, CONV_W, CONV_W, CONV_W, POOL_W, POOL_W, D_MODEL, D_MODEL, D_MODEL)
_OFFS = [0]
for _s in _SIZES:
    _OFFS.append(_OFFS[-1] + _s)
(O_Q, O_K, O_V, O_IQ, O_IK, O_IW, O_GA, O_U, O_BG, O_CG, O_GB, O_PIN, O_GC, O_MA, O_MB, O_MC, O_END) = _OFFS

A_Q, A_K, A_V, A_IQ, A_IKW, A_GA, A_END = 0, 1024, 1280, 1536, 2560, 2688, 3712
B_U, B_BG, B_CG, B_GB, B_PIN, B_GC, B_MB, B_MC, B_END = 0, 512, 1024, 1536, 2048, 2560, 3072, 5120, 7168


def _dot(a, b):
    return jnp.dot(a, b, preferred_element_type=F32)


def _dot_nt(a, b):
    return lax.dot_general(a, b, (((1,), (1,)), ((), ())), preferred_element_type=F32)


def _rms_h(x, g):
    h = x * lax.rsqrt(jnp.mean(x * x, axis=-1, keepdims=True) + EPS) * g
    return h.astype(BF16)


def _silu(x):
    return x * jax.nn.sigmoid(x)


def _const_spec(shape):
    nd = len(shape)
    return pl.BlockSpec(shape, lambda *_: (0,) * nd, pipeline_mode=pl.Buffered(1))


def _proj_a_kernel(x_ref, g_ref, w_ref, c128_ref, s128_ref, c64_ref, s64_ref,
                   q_ref, k_ref, v_ref, kb_ref, vb_ref, iq_ref, ik_ref, ik2_ref, iw_ref, ga_ref):
    h = _rms_h(x_ref[...], g_ref[...])
    c128 = c128_ref[...]
    s128 = s128_ref[...]
    c64 = c64_ref[...]
    s64 = s64_ref[...]
    lane = lax.broadcasted_iota(jnp.int32, (1, LANES), 1)
    first_half64 = (lane % IDX_DIM) < (IDX_DIM // 2)

    def rope128(y):
        return y * c128 + pltpu.roll(y, HEAD_DIM // 2, 1) * s128

    def rope64(y):
        partner = jnp.where(first_half64, pltpu.roll(y, LANES - IDX_DIM // 2, 1),
                            pltpu.roll(y, IDX_DIM // 2, 1))
        return y * c64 + partner * s64

    y = _dot(h, w_ref[:, A_Q:A_K])
    for hd in range(N_HEADS):
        sl = slice(hd * HEAD_DIM, (hd + 1) * HEAD_DIM)
        q_ref[:, sl] = (rope128(y[:, sl]) * (HEAD_DIM ** -0.5 * LOG2E)).astype(BF16)

    y = _dot(h, w_ref[:, A_K:A_IQ])
    for hd in range(N_KV_HEADS):
        sl = slice(hd * HEAD_DIM, (hd + 1) * HEAD_DIM)
        kr = rope128(y[:, sl])
        k_ref[:, hd, :] = kr
        kb_ref[:, sl] = kr.astype(BF16)
        v_ref[:, hd, :] = y[:, KV_W + hd * HEAD_DIM:KV_W + (hd + 1) * HEAD_DIM]
    vb_ref[...] = y[:, KV_W:].astype(BF16)

    y = _dot(h, w_ref[:, A_IQ:A_IKW])
    for c in range(IDX_Q_W // LANES):
        sl = slice(c * LANES, (c + 1) * LANES)
        iq_ref[:, sl] = rope64(y[:, sl]).astype(BF16)

    y = _dot(h, w_ref[:, A_IKW:A_GA])
    ikr = rope64(y)
    ik_ref[...] = ikr[:, :IDX_DIM]
    ikz = jnp.where(lane < IDX_DIM, ikr, 0.0)
    ik2_ref[:, :LANES] = ikz.astype(BF16)
    ik2_ref[:, LANES:] = pltpu.roll(ikz, IDX_DIM, 1).astype(BF16)
    iw_ref[...] = y[:, IDX_DIM:IDX_DIM + N_IDX_HEADS] * ((IDX_DIM ** -0.5) * (N_IDX_HEADS ** -0.5))

    ga_ref[...] = _silu(_dot(h, w_ref[:, A_GA:A_END])).astype(BF16)


def _proj_a(x, g, w_a, tabs, tm):
    m = x.shape[0]
    c128, s128, c64, s64 = tabs
    n_pt = c128.shape[0] // tm
    row = lambda w: pl.BlockSpec((tm, w), lambda i: (i, 0))
    tab = pl.BlockSpec((tm, LANES), lambda i: (i % n_pt, 0))
    out_shape = (
        jax.ShapeDtypeStruct((m, ATTN_W), BF16),
        jax.ShapeDtypeStruct((m, N_KV_HEADS, HEAD_DIM), F32),
        jax.ShapeDtypeStruct((m, N_KV_HEADS, HEAD_DIM), F32),
        jax.ShapeDtypeStruct((m, KV_W), BF16),
        jax.ShapeDtypeStruct((m, KV_W), BF16),
        jax.ShapeDtypeStruct((m, IDX_Q_W), BF16),
        jax.ShapeDtypeStruct((m, IDX_DIM), F32),
        jax.ShapeDtypeStruct((m, 2 * LANES), BF16),
        jax.ShapeDtypeStruct((m, N_IDX_HEADS), F32),
        jax.ShapeDtypeStruct((m, ATTN_W), BF16),
    )
    kv_heads = pl.BlockSpec((tm, N_KV_HEADS, HEAD_DIM), lambda i: (i, 0, 0))
    out_specs = (row(ATTN_W), kv_heads, kv_heads, row(KV_W), row(KV_W), row(IDX_Q_W),
                 row(IDX_DIM), row(2 * LANES), row(N_IDX_HEADS), row(ATTN_W))
    return pl.pallas_call(
        _proj_a_kernel,
        grid=(m // tm,),
        in_specs=[row(D_MODEL), _const_spec((1, D_MODEL)), _const_spec((D_MODEL, A_END)),
                  tab, tab, tab, tab],
        out_specs=out_specs,
        out_shape=out_shape,
        compiler_params=pltpu.CompilerParams(dimension_semantics=("arbitrary",),
                                             vmem_limit_bytes=VMEM_LIMIT),
        name="proj_a",
    )(x, g, w_a, c128, s128, c64, s64)


def _paired_blocks(n_blocks, trip):
    def pair(t, carry):
        trip(2 * t, 2)
        return carry

    lax.fori_loop(0, n_blocks // 2, pair, 0)

    @pl.when(n_blocks % 2 == 1)
    def _():
        trip(n_blocks - 1, 1)


def _attend_kernel(q_ref, iq_ref, iwt_ref, kb_ref, vb_ref, ik2_ref, o_ref,
                   st_ref, sbuf_ref, state_ref, acc_ref, stat_ref,
                   *, tq, tq_real, q0, topk):
    kb_sz = KEY_BLOCK
    i = pl.program_id(1)
    qpos0 = q0 + i * tq_real
    n_keys = ((qpos0 + tq_real - 1) // CHUNK + 1) * CHUNK
    nkb = (n_keys + kb_sz - 1) // kb_sz

    lane_q = lax.broadcasted_iota(jnp.int32, (1, tq), 1)
    n_adm_i = ((qpos0 + lane_q) // CHUNK + 1) * CHUNK
    iw = iwt_ref[0]

    state_ref[0:1, :] = jnp.full((1, tq), jnp.inf, F32)
    state_ref[1:2, :] = jnp.full((1, tq), -jnp.inf, F32)

    def score_trip(kb0, n_blk):
        mn, mx = state_ref[0:1, :], state_ref[1:2, :]
        for kb in [kb0 + d for d in range(n_blk)]:
            r0 = pl.multiple_of(kb * kb_sz, kb_sz)
            ik = ik2_ref[0, pl.ds(r0, kb_sz), :]
            ik_even = ik[:, :LANES]
            ik_odd = ik[:, LANES:]
            acc = jnp.zeros((kb_sz, tq), F32)
            for p in range(N_IDX_HEADS // 2):
                iq_pair = iq_ref[0, :, p * LANES:(p + 1) * LANES]
                d0 = _dot_nt(ik_even, iq_pair)
                d1 = _dot_nt(ik_odd, iq_pair)
                acc = acc + jnp.maximum(d0, 0.0) * iw[2 * p:2 * p + 1]
                acc = acc + jnp.maximum(d1, 0.0) * iw[2 * p + 1:2 * p + 2]
            key = r0 + lax.broadcasted_iota(jnp.int32, (kb_sz, 1), 0)
            masked = jnp.where(key < n_adm_i, acc, -jnp.inf)
            st_ref[pl.ds(r0, kb_sz), :] = masked
            mx = jnp.maximum(mx, jnp.max(masked, axis=0, keepdims=True))
            mn = jnp.minimum(mn, jnp.min(acc, axis=0, keepdims=True))
        state_ref[0:1, :] = mn
        state_ref[1:2, :] = mx

    _paired_blocks(nkb, score_trip)
    mn, mx = state_ref[0:1, :], state_ref[1:2, :]

    n_adm = n_adm_i.astype(F32)
    topk_f = float(topk)
    active0 = jnp.logical_and(n_adm > topk_f, lane_q < tq_real).astype(F32)
    state_ref[0:1, :] = mn
    state_ref[1:2, :] = mx + jnp.maximum(jnp.abs(mx), 1e-30)
    state_ref[2:3, :] = jnp.full((1, tq), F32_LOWEST, F32)
    state_ref[3:4, :] = active0
    state_ref[4:5, :] = n_adm
    state_ref[5:6, :] = jnp.zeros((1, tq), F32)

    def count_ge(mid):
        def cnt_blk(kb, acc):
            r0 = pl.multiple_of(kb * kb_sz, kb_sz)
            ge = (st_ref[pl.ds(r0, kb_sz), :] >= mid).astype(F32)
            return acc + jnp.sum(ge.reshape(COUNT_CHAINS, kb_sz // (8 * COUNT_CHAINS), 8, tq), axis=1)

        acc = lax.fori_loop(0, nkb, cnt_blk, jnp.zeros((COUNT_CHAINS, 8, tq), F32))
        return jnp.sum(jnp.sum(acc, axis=0), axis=0, keepdims=True)

    def bisect_step(st):
        lo, hi, thr, active, clo, tie = st
        mid = 0.5 * lo + 0.5 * hi
        stuck = jnp.logical_or(mid <= lo, mid >= hi)
        cnt = count_ge(mid)
        ge_k = cnt >= topk_f
        exact = cnt == topk_f
        moving = jnp.logical_and(active, jnp.logical_not(stuck))
        ended = jnp.logical_and(active, stuck)
        thr = jnp.where(ended, lo, jnp.where(jnp.logical_and(moving, exact), mid, thr))
        tie = jnp.logical_or(tie, jnp.logical_and(ended, clo > topk_f))
        up = jnp.logical_and(moving, ge_k)
        down = jnp.logical_and(moving, jnp.logical_not(ge_k))
        return (jnp.where(up, mid, lo), jnp.where(down, mid, hi), thr,
                jnp.logical_and(moving, jnp.logical_not(exact)), jnp.where(up, cnt, clo), tie)

    def bisect_steps():
        st = (state_ref[0:1, :], state_ref[1:2, :], state_ref[2:3, :], state_ref[3:4, :] > 0.0,
              state_ref[4:5, :], state_ref[5:6, :] > 0.0)
        for _ in range(BISECT_UNROLL):
            st = bisect_step(st)
        active_f = st[3].astype(F32)
        state_ref[0:1, :] = st[0]
        state_ref[1:2, :] = st[1]
        state_ref[2:3, :] = st[2]
        state_ref[3:4, :] = active_f
        state_ref[4:5, :] = st[4]
        state_ref[5:6, :] = st[5].astype(F32)
        return active_f

    def unchecked(i, carry):
        bisect_steps()
        return carry

    lax.fori_loop(0, BISECT_UNCHECKED, unchecked, 0)
    lax.while_loop(lambda c: jnp.logical_and(c[0] > 0.0, c[1] < 128),
                   lambda c: (jnp.max(bisect_steps()), c[1] + 1),
                   (jnp.max(state_ref[3:4, :]), jnp.int32(0)))
    thr = state_ref[2:3, :]

    @pl.when(jnp.max(state_ref[5:6, :]) > 0.0)
    def _():
        tie_row = state_ref[5:6, :] > 0.0

        def gt_blk(kb, acc):
            r0 = pl.multiple_of(kb * kb_sz, kb_sz)
            return acc + jnp.sum((st_ref[pl.ds(r0, kb_sz), :] > thr).astype(F32), axis=0, keepdims=True)

        need = topk_f - lax.fori_loop(0, nkb, gt_blk, jnp.zeros((1, tq), F32))
        tri = (lax.broadcasted_iota(jnp.int32, (kb_sz, kb_sz), 0)
               >= lax.broadcasted_iota(jnp.int32, (kb_sz, kb_sz), 1)).astype(BF16)

        def fix_blk(kb, run):
            r0 = pl.multiple_of(kb * kb_sz, kb_sz)
            blk = st_ref[pl.ds(r0, kb_sz), :]
            eq = jnp.logical_and(blk == thr, tie_row)
            eq_f = eq.astype(F32)
            rank = _dot(tri, eq_f.astype(BF16)) - eq_f + run
            st_ref[pl.ds(r0, kb_sz), :] = jnp.where(jnp.logical_and(eq, rank >= need), -jnp.inf, blk)
            return run + jnp.sum(eq_f, axis=0, keepdims=True)

        lax.fori_loop(0, nkb, fix_blk, jnp.zeros((1, tq), F32))

    red = lambda a, op: op(a.reshape(2, kb_sz // 16, 8, a.shape[-1]), axis=1)
    gw = GROUP * tq
    heads_per_trip = LOGIT_LANES // gw
    for j0 in range(0, N_KV_HEADS, heads_per_trip):
        trip_heads = range(j0, j0 + heads_per_trip)
        qs = {j: jnp.concatenate(
            [q_ref[0, :, (GROUP * j + g) * HEAD_DIM:(GROUP * j + g + 1) * HEAD_DIM] for g in range(GROUP)],
            axis=0) for j in trip_heads}

        def qk_trip(kb0, n_blk):
            m8 = stat_ref[...]
            for kb in [kb0 + d for d in range(n_blk)]:
                r0 = pl.multiple_of(kb * kb_sz, kb_sz)
                bias = jnp.where(st_ref[pl.ds(r0, kb_sz), :] >= thr, 0.0, NEG_BIAS)
                parts = []
                for j in trip_heads:
                    s = _dot_nt(kb_ref[0, pl.ds(r0, kb_sz), j * HEAD_DIM:(j + 1) * HEAD_DIM], qs[j])
                    for g in range(GROUP):
                        sg = s[:, g * tq:(g + 1) * tq] + bias
                        c0 = (j - j0) * gw + g * tq
                        sbuf_ref[kb, :, c0:c0 + tq] = sg
                        parts.append(red(sg, jnp.max))
                m8 = jnp.maximum(m8, jnp.concatenate(parts, axis=-1))
            stat_ref[...] = m8

        stat_ref[...] = jnp.full(stat_ref.shape, NEG_BIAS, F32)
        _paired_blocks(nkb, qk_trip)
        m = jnp.max(jnp.max(stat_ref[...], axis=0), axis=0, keepdims=True)
        stat_ref[...] = jnp.zeros(stat_ref.shape, F32)
        acc_ref[...] = jnp.zeros(acc_ref.shape, F32)

        def pv_trip(kb0, n_blk):
            sums = []
            for j in trip_heads:
                cols = slice((j - j0) * gw, (j - j0 + 1) * gw)
                pv, psum = 0.0, 0.0
                for kb in [kb0 + d for d in range(n_blk)]:
                    r0 = pl.multiple_of(kb * kb_sz, kb_sz)
                    p = jnp.exp2(sbuf_ref[kb, :, cols] - m[:, cols])
                    pv = pv + lax.dot_general(vb_ref[0, pl.ds(r0, kb_sz), j * HEAD_DIM:(j + 1) * HEAD_DIM],
                                              p.astype(BF16), (((0,), (0,)), ((), ())),
                                              preferred_element_type=F32)
                    psum = psum + red(p, jnp.sum)
                acc_ref[:, cols] += pv
                sums.append(psum)
            stat_ref[...] += jnp.concatenate(sums, axis=-1)

        _paired_blocks(nkb, pv_trip)
        o = acc_ref[...] / jnp.sum(jnp.sum(stat_ref[...], axis=0), axis=0, keepdims=True)
        for j in trip_heads:
            for g in range(GROUP):
                hd = GROUP * j + g
                c0 = (j - j0) * gw + g * tq
                o_ref[0, :, hd * HEAD_DIM:(hd + 1) * HEAD_DIM] = o[:, c0:c0 + tq].T.astype(o_ref.dtype)


def _attend(q, iq, iwt, kb, vb, ik2, *, tq, tq_real, q0, topk):
    b, t_q, _ = q.shape
    l_keys = kb.shape[1]
    n_q = t_q // tq
    n_kb = l_keys // KEY_BLOCK
    qspec = pl.BlockSpec((1, tq, ATTN_W), lambda bi, i: (bi, i, 0))
    kspec = pl.BlockSpec((1, l_keys, KV_W), lambda bi, i: (bi, 0, 0))
    return pl.pallas_call(
        functools.partial(_attend_kernel, tq=tq, tq_real=tq_real, q0=q0, topk=topk),
        grid=(b, n_q),
        in_specs=[qspec, qspec, pl.BlockSpec((1, N_IDX_HEADS, tq), lambda bi, i: (bi, 0, i)),
                  kspec, kspec, kspec],
        out_specs=qspec,
        out_shape=jax.ShapeDtypeStruct((b, t_q, ATTN_W), F32),
        scratch_shapes=[
            pltpu.VMEM((l_keys, tq), F32),
            pltpu.VMEM((n_kb, KEY_BLOCK, LOGIT_LANES), F32),
            pltpu.VMEM((8, tq), F32),
            pltpu.VMEM((HEAD_DIM, LOGIT_LANES), F32),
            pltpu.VMEM((2, 8, LOGIT_LANES), F32),
        ],
        compiler_params=pltpu.CompilerParams(dimension_semantics=("arbitrary", "arbitrary"),
                                             vmem_limit_bytes=VMEM_LIMIT),
        name="attend",
    )(q, iq, iwt, kb, vb, ik2)


def _pack_cache_kernel(c_ref, o_ref, *, n_src):
    j = pl.program_id(1)

    @pl.when(j < n_src)
    def _():
        for hd in range(N_KV_HEADS):
            o_ref[0, :, hd * HEAD_DIM:(hd + 1) * HEAD_DIM] = c_ref[0, 0, :, hd, :].astype(BF16)

    @pl.when(j >= n_src)
    def _():
        o_ref[...] = jnp.zeros(o_ref.shape, BF16)


def _pack_cache(c, layer, l_pad):
    _, b, past = c.shape[:3]
    n_src = past // PACK_ROWS
    return pl.pallas_call(
        functools.partial(_pack_cache_kernel, n_src=n_src),
        grid=(b, l_pad // PACK_ROWS),
        in_specs=[pl.BlockSpec((1, 1, PACK_ROWS, N_KV_HEADS, HEAD_DIM),
                               lambda bi, j: (layer, bi, jnp.minimum(j, n_src - 1), 0, 0))],
        out_specs=pl.BlockSpec((1, PACK_ROWS, KV_W), lambda bi, j: (bi, j, 0)),
        out_shape=jax.ShapeDtypeStruct((b, l_pad, KV_W), BF16),
        compiler_params=pltpu.CompilerParams(dimension_semantics=("arbitrary", "arbitrary"),
                                             vmem_limit_bytes=VMEM_LIMIT),
        name="pack_cache",
    )(c)


def _proj_b_kernel(x_ref, g_ref, w_ref, convw_ref, convb_ref, poolw_ref, pscale_ref, liftb_ref, liftc_ref,
                   chist_ref, phist_ref, zbc_ref, cstate_ref, pstate_ref, cin_ext, pin_ext,
                   *, n_sub, ls, pos0):
    j = pl.program_id(1)

    @pl.when(j == 0)
    def _():
        cin_ext[:, 0:CONV_PAD, :] = chist_ref[...]
        pin_ext[:, 0:POOL_PAD, :] = phist_ref[...]

    @pl.when(j > 0)
    def _():
        cin_ext[:, 0:CONV_PAD, :] = cin_ext[:, ls:ls + CONV_PAD, :]
        pin_ext[:, 0:POOL_PAD, :] = pin_ext[:, ls:ls + POOL_PAD, :]

    h = _rms_h(x_ref[...], g_ref[...])
    rows = lambda a: jnp.concatenate(a, axis=0) if n_sub > 1 else a[0]

    y = _dot(h, w_ref[:, B_U:B_PIN])
    u = y[:, 0:CONV_W]
    b_gate = y[:, CONV_W:2 * CONV_W]
    c_gate = y[:, 2 * CONV_W:3 * CONV_W]
    gate_b = y[:, 3 * CONV_W:4 * CONV_W]
    cin = c_gate * u
    for s in range(n_sub):
        cin_ext[s, CONV_PAD:CONV_PAD + ls, :] = cin[s * ls:(s + 1) * ls]
    conv = (rows([cin_ext[s, CONV_PAD - 2:CONV_PAD - 2 + ls, :] for s in range(n_sub)]) * convw_ref[0:1, :]
            + rows([cin_ext[s, CONV_PAD - 1:CONV_PAD - 1 + ls, :] for s in range(n_sub)]) * convw_ref[1:2, :]
            + cin * convw_ref[2:3, :] + convb_ref[...])
    y_b = b_gate * conv * _silu(gate_b)
    zb = _dot(y_b.astype(BF16), liftb_ref[...])

    y = _dot(h, w_ref[:, B_PIN:B_MB])
    p_in = y[:, 0:POOL_W]
    gate_c = y[:, POOL_W:2 * POOL_W]
    for s in range(n_sub):
        pin_ext[s, POOL_PAD:POOL_PAD + ls, :] = p_in[s * ls:(s + 1) * ls]
    pos = rows([pos0 + j * ls + lax.broadcasted_iota(jnp.int32, (ls, 1), 0)] * n_sub)
    yc_parts = []
    for gi, win in enumerate(POOL_WINDOWS):
        sl = slice(gi * POOL_GROUP, (gi + 1) * POOL_GROUP)
        tot = p_in[:, sl]
        for back in range(1, win):
            tot = tot + rows([pin_ext[s, POOL_PAD - back:POOL_PAD - back + ls, sl] for s in range(n_sub)])
        cnt = jnp.minimum(win, pos + 1).astype(F32)
        d = tot / cnt - p_in[:, sl]
        mixed = _dot(d.astype(BF16), poolw_ref[gi])
        yc_parts.append(mixed * pscale_ref[:, sl] * _silu(gate_c[:, sl]))
    y_c = jnp.concatenate(yc_parts, axis=-1)
    zc = _dot(y_c.astype(BF16), liftc_ref[...])

    m_b = _dot(h, w_ref[:, B_MB:B_MC])
    m_c = _dot(h, w_ref[:, B_MC:B_END])
    zbc_ref[...] = jax.nn.sigmoid(m_b) * zb + jax.nn.sigmoid(m_c) * zc

    @pl.when(j == pl.num_programs(1) - 1)
    def _():
        cstate_ref[...] = cin_ext[:, ls:ls + CONV_PAD, :]
        pstate_ref[...] = pin_ext[:, ls:ls + POOL_PAD, :]


def _proj_b(x, g, w_b, conv_w, conv_b, pool_w, pool_scale, lift_b, lift_c, chist, phist, *, seq_len, tm, pos0):
    m = x.shape[0]
    n_seq = m // seq_len
    n_sub, ls = (1, tm) if tm <= seq_len else (tm // seq_len, seq_len)
    n_t = seq_len // ls
    out_shape = (
        jax.ShapeDtypeStruct((m, D_MODEL), F32),
        jax.ShapeDtypeStruct((n_seq, CONV_PAD, CONV_W), F32),
        jax.ShapeDtypeStruct((n_seq, POOL_PAD, POOL_W), F32),
    )
    hist = lambda r, w: pl.BlockSpec((n_sub, r, w), lambda s, j: (s, 0, 0))
    return pl.pallas_call(
        functools.partial(_proj_b_kernel, n_sub=n_sub, ls=ls, pos0=pos0),
        grid=(n_seq // n_sub, n_t),
        in_specs=[pl.BlockSpec((tm, D_MODEL), lambda s, j: (s * n_t + j, 0)),
                  _const_spec((1, D_MODEL)), _const_spec((D_MODEL, B_END)),
                  _const_spec((CONV_K, CONV_W)), _const_spec((1, CONV_W)),
                  _const_spec((len(POOL_WINDOWS), POOL_GROUP, POOL_GROUP)), _const_spec((1, POOL_W)),
                  _const_spec((CONV_W, D_MODEL)), _const_spec((POOL_W, D_MODEL)),
                  hist(CONV_PAD, CONV_W), hist(POOL_PAD, POOL_W)],
        out_specs=(pl.BlockSpec((tm, D_MODEL), lambda s, j: (s * n_t + j, 0)),
                   hist(CONV_PAD, CONV_W), hist(POOL_PAD, POOL_W)),
        out_shape=out_shape,
        scratch_shapes=[pltpu.VMEM((n_sub, ls + CONV_PAD, CONV_W), F32),
                        pltpu.VMEM((n_sub, ls + POOL_PAD, POOL_W), F32)],
        compiler_params=pltpu.CompilerParams(dimension_semantics=("arbitrary", "arbitrary"),
                                             vmem_limit_bytes=VMEM_LIMIT),
        name="proj_b",
    )(x, g, w_b, conv_w, conv_b, pool_w, pool_scale, lift_b, lift_c, chist, phist)


def _merge_kernel(x_ref, g_ref, wma_ref, attn_ref, ga_ref, lifta_ref, zbc_ref, wout_ref, fg_ref,
                  *out_refs, final):
    x = x_ref[...]
    h = _rms_h(x, g_ref[...])
    y_a = attn_ref[...] * ga_ref[...].astype(F32)
    z = jax.nn.sigmoid(_dot(h, wma_ref[...])) * _dot(y_a.astype(BF16), lifta_ref[...]) + zbc_ref[...]
    out = x + _dot(z.astype(BF16), wout_ref[...])
    out_refs[0][...] = out
    if final:
        out_refs[1][...] = out * lax.rsqrt(jnp.mean(out * out, axis=-1, keepdims=True) + EPS) * fg_ref[...]


def _merge(x, g, w_ma, attn, ga, lift_a, zbc, w_out, final_g, *, tm, final):
    m = x.shape[0]
    row = lambda w: pl.BlockSpec((tm, w), lambda i: (i, 0))
    n_out = 2 if final else 1
    return pl.pallas_call(
        functools.partial(_merge_kernel, final=final),
        grid=(m // tm,),
        in_specs=[row(D_MODEL), _const_spec((1, D_MODEL)), _const_spec((D_MODEL, D_MODEL)),
                  row(ATTN_W), row(ATTN_W), _const_spec((ATTN_W, D_MODEL)), row(D_MODEL),
                  _const_spec((D_MODEL, D_MODEL)), _const_spec((1, D_MODEL))],
        out_specs=tuple(row(D_MODEL) for _ in range(n_out)),
        out_shape=tuple(jax.ShapeDtypeStruct((m, D_MODEL), F32) for _ in range(n_out)),
        compiler_params=pltpu.CompilerParams(dimension_semantics=("arbitrary",),
                                             vmem_limit_bytes=VMEM_LIMIT),
        name="merge",
    )(x, g, w_ma, attn, ga, lift_a, zbc, w_out, final_g)


def _rope_tables(pos):
    def tab(half, reps):
        inv = ROPE_THETA ** (-jnp.arange(half, dtype=F32) / half)
        ang = pos.astype(F32)[:, None] * inv[None, :]
        cos, sin = jnp.cos(ang), jnp.sin(ang)
        return (jnp.tile(jnp.concatenate([cos, cos], axis=-1), (1, reps)),
                jnp.tile(jnp.concatenate([-sin, sin], axis=-1), (1, reps)))
    c128, s128 = tab(HEAD_DIM // 2, 1)
    c64, s64 = tab(IDX_DIM // 2, LANES // IDX_DIM)
    return c128, s128, c64, s64


def _layer(x, tabs, lw, hist, cache, *, n_seq, seq_len, pos0, topk, tm_tok, tm_seq, tq, final, final_g):
    m = n_seq * seq_len
    q, k, v, kb, vb, iq, ik, ik2, iw, ga = _proj_a(x, lw["g"], lw["w_a"], tabs, tm_tok)

    def seq(a):
        return a.reshape(n_seq, seq_len, a.shape[-1])

    iwt = jnp.swapaxes(seq(iw), 1, 2)
    q3, iq3, kb3, vb3, ik23 = seq(q), seq(iq), seq(kb), seq(vb), seq(ik2)
    if cache is not None:
        ck, cv, cik2, layer = cache
        past = ck.shape[2]
        l_pad = past + seq_len + (-(past + seq_len) % PACK_ROWS)
        kb3 = lax.dynamic_update_slice(_pack_cache(ck, layer, l_pad), kb3, (0, past, 0))
        vb3 = lax.dynamic_update_slice(_pack_cache(cv, layer, l_pad), vb3, (0, past, 0))
        ik23 = jnp.concatenate([cik2, ik23, jnp.zeros((n_seq, l_pad - past - seq_len, 2 * LANES), BF16)], axis=1)
    pad_q = -seq_len % tq
    if pad_q:
        q3 = jnp.pad(q3, ((0, 0), (0, pad_q), (0, 0)))
        iq3 = jnp.pad(iq3, ((0, 0), (0, pad_q), (0, 0)))
        iwt = jnp.pad(iwt, ((0, 0), (0, 0), (0, pad_q)))
    attn = _attend(q3, iq3, iwt, kb3, vb3, ik23, tq=tq, tq_real=min(seq_len, tq), q0=pos0, topk=topk)
    attn = attn[:, :seq_len].reshape(m, ATTN_W)

    zbc, cstate, pstate = _proj_b(x, lw["g"], lw["w_b"], lw["conv_w"], lw["conv_b"], lw["pool_w"],
                                  lw["pool_scale"], lw["lift_b"], lw["lift_c"], hist[0], hist[1],
                                  seq_len=seq_len, tm=tm_seq, pos0=pos0)
    outs = _merge(x, lw["g"], lw["w_ma"], attn, ga, lw["lift_a"], zbc, lw["w_out"], final_g,
                  tm=tm_tok, final=final)
    states = (k.reshape(n_seq, seq_len, N_KV_HEADS, HEAD_DIM), v.reshape(n_seq, seq_len, N_KV_HEADS, HEAD_DIM),
              ik.reshape(n_seq, seq_len, IDX_DIM), cstate[:, CONV_PAD - (CONV_K - 1):], pstate[:, POOL_PAD - POOL_HIST:])
    return outs, states


def kernel(x_prompt, x_sample, cache_k, cache_v, cache_idx_k, state_conv, state_pool, norm_g, w_in, conv_w,
           conv_b, pool_w, pool_scale, lift_a, lift_b, lift_c, w_out, final_norm_g):
    batch, seq, _ = x_prompt.shape
    dec_batch, dec_seq, _ = x_sample.shape
    depth = w_in.shape[0]
    past_len = cache_k.shape[2]
    topk_prompt = min(MAX_TOPK, seq // 4)
    topk_sample = min(MAX_TOPK, (past_len + dec_seq) // 4)

    tm_p, tm_s = 256, 256
    tabs_p = _rope_tables(jnp.arange(seq, dtype=jnp.int32))
    tabs_s = _rope_tables(past_len + (jnp.arange(tm_s, dtype=jnp.int32) % dec_seq))
    final_g = final_norm_g.reshape(1, D_MODEL)

    hp = x_prompt.reshape(batch * seq, D_MODEL)
    hs = x_sample.reshape(dec_batch * dec_seq, D_MODEL)
    zero_hist = (jnp.zeros((batch, CONV_PAD, CONV_W), F32), jnp.zeros((batch, POOL_PAD, POOL_W), F32))
    p_states, s_states = [], []
    y_p = y_s = None
    for l in range(depth):
        wl = w_in[l]
        lw = {
            "g": norm_g[l].reshape(1, D_MODEL),
            "w_a": jnp.concatenate([wl[:, O_Q:O_GA], jnp.zeros((D_MODEL, A_GA - A_IKW - IDX_DIM - N_IDX_HEADS), F32),
                                    wl[:, O_GA:O_U]], axis=1).astype(BF16),
            "w_b": jnp.concatenate([wl[:, O_U:O_MA], wl[:, O_MB:O_END]], axis=1).astype(BF16),
            "w_ma": wl[:, O_MA:O_MB].astype(BF16),
            "conv_w": conv_w[l], "conv_b": conv_b[l].reshape(1, CONV_W),
            "pool_w": pool_w[l].astype(BF16), "pool_scale": pool_scale[l].reshape(1, POOL_W),
            "lift_a": lift_a[l].astype(BF16), "lift_b": lift_b[l].astype(BF16), "lift_c": lift_c[l].astype(BF16),
            "w_out": w_out[l].astype(BF16),
        }
        final = l == depth - 1
        outs_p, st_p = _layer(hp, tabs_p, lw, zero_hist, None, n_seq=batch, seq_len=seq, pos0=0,
                              topk=topk_prompt, tm_tok=tm_p, tm_seq=tm_p, tq=Q_TILE_PROMPT, final=final,
                              final_g=final_g)
        cik = cache_idx_k[l].astype(BF16)
        cik2 = jnp.concatenate([cik, jnp.zeros_like(cik), jnp.zeros_like(cik), cik], axis=-1)
        cache = (cache_k, cache_v, cik2, l)
        hist_s = (jnp.pad(state_conv[l], ((0, 0), (CONV_PAD - (CONV_K - 1), 0), (0, 0))),
                  jnp.pad(state_pool[l], ((0, 0), (POOL_PAD - POOL_HIST, 0), (0, 0))))
        outs_s, st_s = _layer(hs, tabs_s, lw, hist_s, cache, n_seq=dec_batch, seq_len=dec_seq, pos0=past_len,
                              topk=topk_sample, tm_tok=tm_s, tm_seq=tm_s, tq=Q_TILE_SAMPLE, final=final,
                              final_g=final_g)
        hp, hs = outs_p[0], outs_s[0]
        if final:
            y_p, y_s = outs_p[1], outs_s[1]
        p_states.append(st_p)
        s_states.append(st_s)

    stack = lambda sts, i: jnp.stack([st[i] for st in sts])
    return (y_p.reshape(batch, seq, D_MODEL), y_s.reshape(dec_batch, dec_seq, D_MODEL),
            stack(p_states, 0), stack(p_states, 1), stack(p_states, 2), stack(p_states, 3), stack(p_states, 4),
            stack(s_states, 0), stack(s_states, 1), stack(s_states, 2), stack(s_states, 3), stack(s_states, 4))
```

```python
import functools

import jax
import jax.numpy as jnp
from jax import lax
from jax.experimental import pallas as pl
from jax.experimental.pallas import tpu as pltpu

D_MODEL = 2048
CHUNK = 64
N_HEADS = 8
HEAD_DIM = 128
N_KV_HEADS = 2
GROUP = N_HEADS // N_KV_HEADS
ATTN_W = N_HEADS * HEAD_DIM
KV_W = N_KV_HEADS * HEAD_DIM
N_IDX_HEADS = 16
IDX_DIM = 64
IDX_Q_W = N_IDX_HEADS * IDX_DIM
MAX_TOPK = 256
CONV_W = 512
CONV_K = 3
POOL_W = 512
POOL_WINDOWS = (2, 4, 8, 16)
POOL_GROUP = 128
POOL_HIST = 15
ROPE_THETA = 10000.0
EPS = 1e-6

LANES = 128
CONV_PAD = 8
POOL_PAD = 16
KEY_BLOCK = 512
PACK_ROWS = 1024
Q_TILE_PROMPT = 256
Q_TILE_SAMPLE = 128
LOGIT_LANES = 1024
NEG_BIAS = -1e30
COUNT_CHAINS = 4
BISECT_UNROLL = 4
BISECT_UNCHECKED = 3
LOG2E = 1.4426950408889634
F32_LOWEST = -3.0e38
VMEM_LIMIT = 56 * 1024 * 1024

F32 = jnp.float32
BF16 = jnp.bfloat16

_SIZES = (ATTN_W, KV_W, KV_W, IDX_Q_W, IDX_DIM, N_IDX_HEADS, ATTN_W,
          CONV_W, CONV_W, CONV_W, CONV_W, POOL_W, POOL_W, D_MODEL, D_MODEL, D_MODEL)
_OFFS = [0]
for _s in _SIZES:
    _OFFS.append(_OFFS[-1] + _s)
(O_Q, O_K, O_V, O_IQ, O_IK, O_IW, O_GA, O_U, O_BG, O_CG, O_GB, O_PIN, O_GC, O_MA, O_MB, O_MC, O_END) = _OFFS

A_Q, A_K, A_V, A_IQ, A_IKW, A_GA, A_END = 0, 1024, 1280, 1536, 2560, 2688, 3712
B_U, B_BG, B_CG, B_GB, B_PIN, B_GC, B_MB, B_MC, B_END = 0, 512, 1024, 1536, 2048, 2560, 3072, 5120, 7168


def _dot(a, b):
    return jnp.dot(a, b, preferred_element_type=F32)


def _dot_nt(a, b):
    return lax.dot_general(a, b, (((1,), (1,)), ((), ())), preferred_element_type=F32)


def _rms_h(x, g):
    h = x * lax.rsqrt(jnp.mean(x * x, axis=-1, keepdims=True) + EPS) * g
    return h.astype(BF16)


def _silu(x):
    return x * jax.nn.sigmoid(x)


def _const_spec(shape):
    nd = len(shape)
    return pl.BlockSpec(shape, lambda *_: (0,) * nd, pipeline_mode=pl.Buffered(1))


def _proj_a_kernel(x_ref, g_ref, w_ref, c128_ref, s128_ref, c64_ref, s64_ref,
                   q_ref, k_ref, v_ref, kb_ref, vb_ref, iq_ref, ik_ref, ik2_ref, iw_ref, ga_ref):
    h = _rms_h(x_ref[...], g_ref[...])
    c128 = c128_ref[...]
    s128 = s128_ref[...]
    c64 = c64_ref[...]
    s64 = s64_ref[...]
    lane = lax.broadcasted_iota(jnp.int32, (1, LANES), 1)
    first_half64 = (lane % IDX_DIM) < (IDX_DIM // 2)

    def rope128(y):
        return y * c128 + pltpu.roll(y, HEAD_DIM // 2, 1) * s128

    def rope64(y):
        partner = jnp.where(first_half64, pltpu.roll(y, LANES - IDX_DIM // 2, 1),
                            pltpu.roll(y, IDX_DIM // 2, 1))
        return y * c64 + partner * s64

    y = _dot(h, w_ref[:, A_Q:A_K])
    for hd in range(N_HEADS):
        sl = slice(hd * HEAD_DIM, (hd + 1) * HEAD_DIM)
        q_ref[:, sl] = (rope128(y[:, sl]) * (HEAD_DIM ** -0.5 * LOG2E)).astype(BF16)

    y = _dot(h, w_ref[:, A_K:A_IQ])
    for hd in range(N_KV_HEADS):
        sl = slice(hd * HEAD_DIM, (hd + 1) * HEAD_DIM)
        kr = rope128(y[:, sl])
        k_ref[:, hd, :] = kr
        kb_ref[:, sl] = kr.astype(BF16)
        v_ref[:, hd, :] = y[:, KV_W + hd * HEAD_DIM:KV_W + (hd + 1) * HEAD_DIM]
    vb_ref[...] = y[:, KV_W:].astype(BF16)

    y = _dot(h, w_ref[:, A_IQ:A_IKW])
    for c in range(IDX_Q_W // LANES):
        sl = slice(c * LANES, (c + 1) * LANES)
        iq_ref[:, sl] = rope64(y[:, sl]).astype(BF16)

    y = _dot(h, w_ref[:, A_IKW:A_GA])
    ikr = rope64(y)
    ik_ref[...] = ikr[:, :IDX_DIM]
    ikz = jnp.where(lane < IDX_DIM, ikr, 0.0)
    ik2_ref[:, :LANES] = ikz.astype(BF16)
    ik2_ref[:, LANES:] = pltpu.roll(ikz, IDX_DIM, 1).astype(BF16)
    iw_ref[...] = y[:, IDX_DIM:IDX_DIM + N_IDX_HEADS] * ((IDX_DIM ** -0.5) * (N_IDX_HEADS ** -0.5))

    ga_ref[...] = _silu(_dot(h, w_ref[:, A_GA:A_END])).astype(BF16)


def _proj_a(x, g, w_a, tabs, tm):
    m = x.shape[0]
    c128, s128, c64, s64 = tabs
    n_pt = c128.shape[0] // tm
    row = lambda w: pl.BlockSpec((tm, w), lambda i: (i, 0))
    tab = pl.BlockSpec((tm, LANES), lambda i: (i % n_pt, 0))
    out_shape = (
        jax.ShapeDtypeStruct((m, ATTN_W), BF16),
        jax.ShapeDtypeStruct((m, N_KV_HEADS, HEAD_DIM), F32),
        jax.ShapeDtypeStruct((m, N_KV_HEADS, HEAD_DIM), F32),
        jax.ShapeDtypeStruct((m, KV_W), BF16),
        jax.ShapeDtypeStruct((m, KV_W), BF16),
        jax.ShapeDtypeStruct((m, IDX_Q_W), BF16),
        jax.ShapeDtypeStruct((m, IDX_DIM), F32),
        jax.ShapeDtypeStruct((m, 2 * LANES), BF16),
        jax.ShapeDtypeStruct((m, N_IDX_HEADS), F32),
        jax.ShapeDtypeStruct((m, ATTN_W), BF16),
    )
    kv_heads = pl.BlockSpec((tm, N_KV_HEADS, HEAD_DIM), lambda i: (i, 0, 0))
    out_specs = (row(ATTN_W), kv_heads, kv_heads, row(KV_W), row(KV_W), row(IDX_Q_W),
                 row(IDX_DIM), row(2 * LANES), row(N_IDX_HEADS), row(ATTN_W))
    return pl.pallas_call(
        _proj_a_kernel,
        grid=(m // tm,),
        in_specs=[row(D_MODEL), _const_spec((1, D_MODEL)), _const_spec((D_MODEL, A_END)),
                  tab, tab, tab, tab],
        out_specs=out_specs,
        out_shape=out_shape,
        compiler_params=pltpu.CompilerParams(dimension_semantics=("arbitrary",),
                                             vmem_limit_bytes=VMEM_LIMIT),
        name="proj_a",
    )(x, g, w_a, c128, s128, c64, s64)


def _paired_blocks(n_blocks, trip):
    def pair(t, carry):
        trip(2 * t, 2)
        return carry

    lax.fori_loop(0, n_blocks // 2, pair, 0)

    @pl.when(n_blocks % 2 == 1)
    def _():
        trip(n_blocks - 1, 1)


def _attend_kernel(q_ref, iq_ref, iwt_ref, kb_ref, vb_ref, ik2_ref, o_ref,
                   st_ref, sbuf_ref, state_ref, acc_ref, stat_ref,
                   *, tq, tq_real, q0, topk):
    kb_sz = KEY_BLOCK
    i = pl.program_id(1)
    qpos0 = q0 + i * tq_real
    n_keys = ((qpos0 + tq_real - 1) // CHUNK + 1) * CHUNK
    nkb = (n_keys + kb_sz - 1) // kb_sz

    lane_q = lax.broadcasted_iota(jnp.int32, (1, tq), 1)
    n_adm_i = ((qpos0 + lane_q) // CHUNK + 1) * CHUNK
    iw = iwt_ref[0]

    state_ref[0:1, :] = jnp.full((1, tq), jnp.inf, F32)
    state_ref[1:2, :] = jnp.full((1, tq), -jnp.inf, F32)

    def score_trip(kb0, n_blk):
        mn, mx = state_ref[0:1, :], state_ref[1:2, :]
        for kb in [kb0 + d for d in range(n_blk)]:
            r0 = pl.multiple_of(kb * kb_sz, kb_sz)
            ik = ik2_ref[0, pl.ds(r0, kb_sz), :]
            ik_even = ik[:, :LANES]
            ik_odd = ik[:, LANES:]
            acc = jnp.zeros((kb_sz, tq), F32)
            for p in range(N_IDX_HEADS // 2):
                iq_pair = iq_ref[0, :, p * LANES:(p + 1) * LANES]
                d0 = _dot_nt(ik_even, iq_pair)
                d1 = _dot_nt(ik_odd, iq_pair)
                acc = acc + jnp.maximum(d0, 0.0) * iw[2 * p:2 * p + 1]
                acc = acc + jnp.maximum(d1, 0.0) * iw[2 * p + 1:2 * p + 2]
            key = r0 + lax.broadcasted_iota(jnp.int32, (kb_sz, 1), 0)
            masked = jnp.where(key < n_adm_i, acc, -jnp.inf)
            st_ref[pl.ds(r0, kb_sz), :] = masked
            mx = jnp.maximum(mx, jnp.max(masked, axis=0, keepdims=True))
            mn = jnp.minimum(mn, jnp.min(acc, axis=0, keepdims=True))
        state_ref[0:1, :] = mn
        state_ref[1:2, :] = mx

    _paired_blocks(nkb, score_trip)
    mn, mx = state_ref[0:1, :], state_ref[1:2, :]

    n_adm = n_adm_i.astype(F32)
    topk_f = float(topk)
    active0 = jnp.logical_and(n_adm > topk_f, lane_q < tq_real).astype(F32)
    state_ref[0:1, :] = mn
    state_ref[1:2, :] = mx + jnp.maximum(jnp.abs(mx), 1e-30)
    state_ref[2:3, :] = jnp.full((1, tq), F32_LOWEST, F32)
    state_ref[3:4, :] = active0
    state_ref[4:5, :] = n_adm
    state_ref[5:6, :] = jnp.zeros((1, tq), F32)
    state_ref[6:7, :] = jnp.zeros((1, tq), F32)

    def count_ge(mid):
        def cnt_blk(kb, acc):
            r0 = pl.multiple_of(kb * kb_sz, kb_sz)
            ge = (st_ref[pl.ds(r0, kb_sz), :] >= mid).astype(F32)
            return acc + jnp.sum(ge.reshape(COUNT_CHAINS, kb_sz // (8 * COUNT_CHAINS), 8, tq), axis=1)

        acc = lax.fori_loop(0, nkb, cnt_blk, jnp.zeros((COUNT_CHAINS, 8, tq), F32))
        return jnp.sum(jnp.sum(acc, axis=0), axis=0, keepdims=True)

    def bisect_step(st):
        lo, hi, thr, active, clo, tie, pend = st
        mid = 0.5 * lo + 0.5 * hi
        stuck = jnp.logical_or(mid <= lo, mid >= hi)
        cnt = count_ge(mid)
        ge_k = cnt >= topk_f
        exact = cnt == topk_f
        moving = jnp.logical_and(active, jnp.logical_not(stuck))
        ended = jnp.logical_and(active, stuck)
        thr = jnp.where(ended, lo, jnp.where(jnp.logical_and(moving, exact), mid, thr))
        tie = jnp.logical_or(tie, jnp.logical_and(ended, clo > topk_f))
        up = jnp.logical_and(moving, ge_k)
        down = jnp.logical_and(moving, jnp.logical_not(ge_k))
        near = jnp.logical_and(moving, cnt == topk_f - 1.0)
        done = jnp.logical_or(exact, near)
        return (jnp.where(up, mid, lo), jnp.where(down, mid, hi), thr,
                jnp.logical_and(moving, jnp.logical_not(done)), jnp.where(up, cnt, clo), tie,
                jnp.logical_or(pend, near))

    def bisect_steps():
        st = (state_ref[0:1, :], state_ref[1:2, :], state_ref[2:3, :], state_ref[3:4, :] > 0.0,
              state_ref[4:5, :], state_ref[5:6, :] > 0.0, state_ref[6:7, :] > 0.0)
        for _ in range(BISECT_UNROLL):
            st = bisect_step(st)
        active_f = st[3].astype(F32)
        state_ref[0:1, :] = st[0]
        state_ref[1:2, :] = st[1]
        state_ref[2:3, :] = st[2]
        state_ref[3:4, :] = active_f
        state_ref[4:5, :] = st[4]
        state_ref[5:6, :] = st[5].astype(F32)
        state_ref[6:7, :] = st[6].astype(F32)
        return active_f

    def unchecked(i, carry):
        bisect_steps()
        return carry

    lax.fori_loop(0, BISECT_UNCHECKED, unchecked, 0)
    lax.while_loop(lambda c: jnp.logical_and(c[0] > 0.0, c[1] < 128),
                   lambda c: (jnp.max(bisect_steps()), c[1] + 1),
                   (jnp.max(state_ref[3:4, :]), jnp.int32(0)))

    @pl.when(jnp.max(state_ref[6:7, :]) > 0.0)
    def _():
        pend = state_ref[6:7, :] > 0.0
        hi = state_ref[1:2, :]

        def below_blk(kb, acc):
            r0 = pl.multiple_of(kb * kb_sz, kb_sz)
            blk = st_ref[pl.ds(r0, kb_sz), :]
            below = jnp.where(blk < hi, blk, -jnp.inf)
            return jnp.maximum(acc, jnp.max(below.reshape(COUNT_CHAINS, kb_sz // (8 * COUNT_CHAINS), 8, tq), axis=1))

        top = lax.fori_loop(0, nkb, below_blk, jnp.full((COUNT_CHAINS, 8, tq), -jnp.inf, F32))
        thr_p = jnp.where(pend, jnp.max(jnp.max(top, axis=0), axis=0, keepdims=True), state_ref[2:3, :])
        state_ref[2:3, :] = thr_p
        dup = jnp.logical_and(pend, count_ge(thr_p) > topk_f)
        state_ref[5:6, :] = jnp.maximum(state_ref[5:6, :], dup.astype(F32))

    thr = state_ref[2:3, :]

    @pl.when(jnp.max(state_ref[5:6, :]) > 0.0)
    def _():
        tie_row = state_ref[5:6, :] > 0.0

        def gt_blk(kb, acc):
            r0 = pl.multiple_of(kb * kb_sz, kb_sz)
            return acc + jnp.sum((st_ref[pl.ds(r0, kb_sz), :] > thr).astype(F32), axis=0, keepdims=True)

        need = topk_f - lax.fori_loop(0, nkb, gt_blk, jnp.zeros((1, tq), F32))
        tri = (lax.broadcasted_iota(jnp.int32, (kb_sz, kb_sz), 0)
               >= lax.broadcasted_iota(jnp.int32, (kb_sz, kb_sz), 1)).astype(BF16)

        def fix_blk(kb, run):
            r0 = pl.multiple_of(kb * kb_sz, kb_sz)
            blk = st_ref[pl.ds(r0, kb_sz), :]
            eq = jnp.logical_and(blk == thr, tie_row)
            eq_f = eq.astype(F32)
            rank = _dot(tri, eq_f.astype(BF16)) - eq_f + run
            st_ref[pl.ds(r0, kb_sz), :] = jnp.where(jnp.logical_and(eq, rank >= need), -jnp.inf, blk)
            return run + jnp.sum(eq_f, axis=0, keepdims=True)

        lax.fori_loop(0, nkb, fix_blk, jnp.zeros((1, tq), F32))

    red = lambda a, op: op(a.reshape(2, kb_sz // 16, 8, a.shape[-1]), axis=1)
    gw = GROUP * tq
    heads_per_trip = LOGIT_LANES // gw
    for j0 in range(0, N_KV_HEADS, heads_per_trip):
        trip_heads = range(j0, j0 + heads_per_trip)
        qs = {j: jnp.concatenate(
            [q_ref[0, :, (GROUP * j + g) * HEAD_DIM:(GROUP * j + g + 1) * HEAD_DIM] for g in range(GROUP)],
            axis=0) for j in trip_heads}

        def qk_trip(kb0, n_blk):
            m8 = stat_ref[...]
            for kb in [kb0 + d for d in range(n_blk)]:
                r0 = pl.multiple_of(kb * kb_sz, kb_sz)
                bias = jnp.where(st_ref[pl.ds(r0, kb_sz), :] >= thr, 0.0, NEG_BIAS)
                parts = []
                for j in trip_heads:
                    s = _dot_nt(kb_ref[0, pl.ds(r0, kb_sz), j * HEAD_DIM:(j + 1) * HEAD_DIM], qs[j])
                    for g in range(GROUP):
                        sg = s[:, g * tq:(g + 1) * tq] + bias
                        c0 = (j - j0) * gw + g * tq
                        sbuf_ref[kb, :, c0:c0 + tq] = sg
                        parts.append(red(sg, jnp.max))
                m8 = jnp.maximum(m8, jnp.concatenate(parts, axis=-1))
            stat_ref[...] = m8

        stat_ref[...] = jnp.full(stat_ref.shape, NEG_BIAS, F32)
        _paired_blocks(nkb, qk_trip)
        m = jnp.max(jnp.max(stat_ref[...], axis=0), axis=0, keepdims=True)
        stat_ref[...] = jnp.zeros(stat_ref.shape, F32)
        acc_ref[...] = jnp.zeros(acc_ref.shape, F32)

        def pv_trip(kb0, n_blk):
            sums = []
            for j in trip_heads:
                cols = slice((j - j0) * gw, (j - j0 + 1) * gw)
                pv, psum = 0.0, 0.0
                for kb in [kb0 + d for d in range(n_blk)]:
                    r0 = pl.multiple_of(kb * kb_sz, kb_sz)
                    p = jnp.exp2(sbuf_ref[kb, :, cols] - m[:, cols])
                    pv = pv + lax.dot_general(vb_ref[0, pl.ds(r0, kb_sz), j * HEAD_DIM:(j + 1) * HEAD_DIM],
                                              p.astype(BF16), (((0,), (0,)), ((), ())),
                                              preferred_element_type=F32)
                    psum = psum + red(p, jnp.sum)
                acc_ref[:, cols] += pv
                sums.append(psum)
            stat_ref[...] += jnp.concatenate(sums, axis=-1)

        _paired_blocks(nkb, pv_trip)
        o = acc_ref[...] / jnp.sum(jnp.sum(stat_ref[...], axis=0), axis=0, keepdims=True)
        for j in trip_heads:
            for g in range(GROUP):
                hd = GROUP * j + g
                c0 = (j - j0) * gw + g * tq
                o_ref[0, :, hd * HEAD_DIM:(hd + 1) * HEAD_DIM] = o[:, c0:c0 + tq].T.astype(o_ref.dtype)


def _attend(q, iq, iwt, kb, vb, ik2, *, tq, tq_real, q0, topk):
    b, t_q, _ = q.shape
    l_keys = kb.shape[1]
    n_q = t_q // tq
    n_kb = l_keys // KEY_BLOCK
    qspec = pl.BlockSpec((1, tq, ATTN_W), lambda bi, i: (bi, i, 0))
    kspec = pl.BlockSpec((1, l_keys, KV_W), lambda bi, i: (bi, 0, 0))
    return pl.pallas_call(
        functools.partial(_attend_kernel, tq=tq, tq_real=tq_real, q0=q0, topk=topk),
        grid=(b, n_q),
        in_specs=[qspec, qspec, pl.BlockSpec((1, N_IDX_HEADS, tq), lambda bi, i: (bi, 0, i)),
                  kspec, kspec, kspec],
        out_specs=qspec,
        out_shape=jax.ShapeDtypeStruct((b, t_q, ATTN_W), F32),
        scratch_shapes=[
            pltpu.VMEM((l_keys, tq), F32),
            pltpu.VMEM((n_kb, KEY_BLOCK, LOGIT_LANES), F32),
            pltpu.VMEM((8, tq), F32),
            pltpu.VMEM((HEAD_DIM, LOGIT_LANES), F32),
            pltpu.VMEM((2, 8, LOGIT_LANES), F32),
        ],
        compiler_params=pltpu.CompilerParams(dimension_semantics=("arbitrary", "arbitrary"),
                                             vmem_limit_bytes=VMEM_LIMIT),
        name="attend",
    )(q, iq, iwt, kb, vb, ik2)


def _pack_cache_kernel(c_ref, o_ref, *, n_src):
    j = pl.program_id(1)

    @pl.when(j < n_src)
    def _():
        for hd in range(N_KV_HEADS):
            o_ref[0, :, hd * HEAD_DIM:(hd + 1) * HEAD_DIM] = c_ref[0, 0, :, hd, :].astype(BF16)

    @pl.when(j >= n_src)
    def _():
        o_ref[...] = jnp.zeros(o_ref.shape, BF16)


def _pack_cache(c, layer, l_pad):
    _, b, past = c.shape[:3]
    n_src = past // PACK_ROWS
    return pl.pallas_call(
        functools.partial(_pack_cache_kernel, n_src=n_src),
        grid=(b, l_pad // PACK_ROWS),
        in_specs=[pl.BlockSpec((1, 1, PACK_ROWS, N_KV_HEADS, HEAD_DIM),
                               lambda bi, j: (layer, bi, jnp.minimum(j, n_src - 1), 0, 0))],
        out_specs=pl.BlockSpec((1, PACK_ROWS, KV_W), lambda bi, j: (bi, j, 0)),
        out_shape=jax.ShapeDtypeStruct((b, l_pad, KV_W), BF16),
        compiler_params=pltpu.CompilerParams(dimension_semantics=("arbitrary", "arbitrary"),
                                             vmem_limit_bytes=VMEM_LIMIT),
        name="pack_cache",
    )(c)


def _proj_b_kernel(x_ref, g_ref, w_ref, convw_ref, convb_ref, poolw_ref, pscale_ref, liftb_ref, liftc_ref,
                   chist_ref, phist_ref, zbc_ref, cstate_ref, pstate_ref, cin_ext, pin_ext,
                   *, n_sub, ls, pos0):
    j = pl.program_id(1)

    @pl.when(j == 0)
    def _():
        cin_ext[:, 0:CONV_PAD, :] = chist_ref[...]
        pin_ext[:, 0:POOL_PAD, :] = phist_ref[...]

    @pl.when(j > 0)
    def _():
        cin_ext[:, 0:CONV_PAD, :] = cin_ext[:, ls:ls + CONV_PAD, :]
        pin_ext[:, 0:POOL_PAD, :] = pin_ext[:, ls:ls + POOL_PAD, :]

    h = _rms_h(x_ref[...], g_ref[...])
    rows = lambda a: jnp.concatenate(a, axis=0) if n_sub > 1 else a[0]

    y = _dot(h, w_ref[:, B_U:B_PIN])
    u = y[:, 0:CONV_W]
    b_gate = y[:, CONV_W:2 * CONV_W]
    c_gate = y[:, 2 * CONV_W:3 * CONV_W]
    gate_b = y[:, 3 * CONV_W:4 * CONV_W]
    cin = c_gate * u
    for s in range(n_sub):
        cin_ext[s, CONV_PAD:CONV_PAD + ls, :] = cin[s * ls:(s + 1) * ls]
    conv = (rows([cin_ext[s, CONV_PAD - 2:CONV_PAD - 2 + ls, :] for s in range(n_sub)]) * convw_ref[0:1, :]
            + rows([cin_ext[s, CONV_PAD - 1:CONV_PAD - 1 + ls, :] for s in range(n_sub)]) * convw_ref[1:2, :]
            + cin * convw_ref[2:3, :] + convb_ref[...])
    y_b = b_gate * conv * _silu(gate_b)
    zb = _dot(y_b.astype(BF16), liftb_ref[...])

    y = _dot(h, w_ref[:, B_PIN:B_MB])
    p_in = y[:, 0:POOL_W]
    gate_c = y[:, POOL_W:2 * POOL_W]
    for s in range(n_sub):
        pin_ext[s, POOL_PAD:POOL_PAD + ls, :] = p_in[s * ls:(s + 1) * ls]
    pos = rows([pos0 + j * ls + lax.broadcasted_iota(jnp.int32, (ls, 1), 0)] * n_sub)
    yc_parts = []
    for gi, win in enumerate(POOL_WINDOWS):
        sl = slice(gi * POOL_GROUP, (gi + 1) * POOL_GROUP)
        tot = p_in[:, sl]
        for back in range(1, win):
            tot = tot + rows([pin_ext[s, POOL_PAD - back:POOL_PAD - back + ls, sl] for s in range(n_sub)])
        cnt = jnp.minimum(win, pos + 1).astype(F32)
        d = tot / cnt - p_in[:, sl]
        mixed = _dot(d.astype(BF16), poolw_ref[gi])
        yc_parts.append(mixed * pscale_ref[:, sl] * _silu(gate_c[:, sl]))
    y_c = jnp.concatenate(yc_parts, axis=-1)
    zc = _dot(y_c.astype(BF16), liftc_ref[...])

    m_b = _dot(h, w_ref[:, B_MB:B_MC])
    m_c = _dot(h, w_ref[:, B_MC:B_END])
    zbc_ref[...] = jax.nn.sigmoid(m_b) * zb + jax.nn.sigmoid(m_c) * zc

    @pl.when(j == pl.num_programs(1) - 1)
    def _():
        cstate_ref[...] = cin_ext[:, ls:ls + CONV_PAD, :]
        pstate_ref[...] = pin_ext[:, ls:ls + POOL_PAD, :]


def _proj_b(x, g, w_b, conv_w, conv_b, pool_w, pool_scale, lift_b, lift_c, chist, phist, *, seq_len, tm, pos0):
    m = x.shape[0]
    n_seq = m // seq_len
    n_sub, ls = (1, tm) if tm <= seq_len else (tm // seq_len, seq_len)
    n_t = seq_len // ls
    out_shape = (
        jax.ShapeDtypeStruct((m, D_MODEL), F32),
        jax.ShapeDtypeStruct((n_seq, CONV_PAD, CONV_W), F32),
        jax.ShapeDtypeStruct((n_seq, POOL_PAD, POOL_W), F32),
    )
    hist = lambda r, w: pl.BlockSpec((n_sub, r, w), lambda s, j: (s, 0, 0))
    return pl.pallas_call(
        functools.partial(_proj_b_kernel, n_sub=n_sub, ls=ls, pos0=pos0),
        grid=(n_seq // n_sub, n_t),
        in_specs=[pl.BlockSpec((tm, D_MODEL), lambda s, j: (s * n_t + j, 0)),
                  _const_spec((1, D_MODEL)), _const_spec((D_MODEL, B_END)),
                  _const_spec((CONV_K, CONV_W)), _const_spec((1, CONV_W)),
                  _const_spec((len(POOL_WINDOWS), POOL_GROUP, POOL_GROUP)), _const_spec((1, POOL_W)),
                  _const_spec((CONV_W, D_MODEL)), _const_spec((POOL_W, D_MODEL)),
                  hist(CONV_PAD, CONV_W), hist(POOL_PAD, POOL_W)],
        out_specs=(pl.BlockSpec((tm, D_MODEL), lambda s, j: (s * n_t + j, 0)),
                   hist(CONV_PAD, CONV_W), hist(POOL_PAD, POOL_W)),
        out_shape=out_shape,
        scratch_shapes=[pltpu.VMEM((n_sub, ls + CONV_PAD, CONV_W), F32),
                        pltpu.VMEM((n_sub, ls + POOL_PAD, POOL_W), F32)],
        compiler_params=pltpu.CompilerParams(dimension_semantics=("arbitrary", "arbitrary"),
                                             vmem_limit_bytes=VMEM_LIMIT),
        name="proj_b",
    )(x, g, w_b, conv_w, conv_b, pool_w, pool_scale, lift_b, lift_c, chist, phist)


def _merge_kernel(x_ref, g_ref, wma_ref, attn_ref, ga_ref, lifta_ref, zbc_ref, wout_ref, fg_ref,
                  *out_refs, final):
    x = x_ref[...]
    h = _rms_h(x, g_ref[...])
    y_a = attn_ref[...] * ga_ref[...].astype(F32)
    z = jax.nn.sigmoid(_dot(h, wma_ref[...])) * _dot(y_a.astype(BF16), lifta_ref[...]) + zbc_ref[...]
    out = x + _dot(z.astype(BF16), wout_ref[...])
    out_refs[0][...] = out
    if final:
        out_refs[1][...] = out * lax.rsqrt(jnp.mean(out * out, axis=-1, keepdims=True) + EPS) * fg_ref[...]


def _merge(x, g, w_ma, attn, ga, lift_a, zbc, w_out, final_g, *, tm, final):
    m = x.shape[0]
    row = lambda w: pl.BlockSpec((tm, w), lambda i: (i, 0))
    n_out = 2 if final else 1
    return pl.pallas_call(
        functools.partial(_merge_kernel, final=final),
        grid=(m // tm,),
        in_specs=[row(D_MODEL), _const_spec((1, D_MODEL)), _const_spec((D_MODEL, D_MODEL)),
                  row(ATTN_W), row(ATTN_W), _const_spec((ATTN_W, D_MODEL)), row(D_MODEL),
                  _const_spec((D_MODEL, D_MODEL)), _const_spec((1, D_MODEL))],
        out_specs=tuple(row(D_MODEL) for _ in range(n_out)),
        out_shape=tuple(jax.ShapeDtypeStruct((m, D_MODEL), F32) for _ in range(n_out)),
        compiler_params=pltpu.CompilerParams(dimension_semantics=("arbitrary",),
                                             vmem_limit_bytes=VMEM_LIMIT),
        name="merge",
    )(x, g, w_ma, attn, ga, lift_a, zbc, w_out, final_g)


def _rope_tables(pos):
    def tab(half, reps):
        inv = ROPE_THETA ** (-jnp.arange(half, dtype=F32) / half)
        ang = pos.astype(F32)[:, None] * inv[None, :]
        cos, sin = jnp.cos(ang), jnp.sin(ang)
        return (jnp.tile(jnp.concatenate([cos, cos], axis=-1), (1, reps)),
                jnp.tile(jnp.concatenate([-sin, sin], axis=-1), (1, reps)))
    c128, s128 = tab(HEAD_DIM // 2, 1)
    c64, s64 = tab(IDX_DIM // 2, LANES // IDX_DIM)
    return c128, s128, c64, s64


def _layer(x, tabs, lw, hist, cache, *, n_seq, seq_len, pos0, topk, tm_tok, tm_seq, tq, final, final_g):
    m = n_seq * seq_len
    q, k, v, kb, vb, iq, ik, ik2, iw, ga = _proj_a(x, lw["g"], lw["w_a"], tabs, tm_tok)

    def seq(a):
        return a.reshape(n_seq, seq_len, a.shape[-1])

    iwt = jnp.swapaxes(seq(iw), 1, 2)
    q3, iq3, kb3, vb3, ik23 = seq(q), seq(iq), seq(kb), seq(vb), seq(ik2)
    if cache is not None:
        ck, cv, cik2, layer = cache
        past = ck.shape[2]
        l_pad = past + seq_len + (-(past + seq_len) % PACK_ROWS)
        kb3 = lax.dynamic_update_slice(_pack_cache(ck, layer, l_pad), kb3, (0, past, 0))
        vb3 = lax.dynamic_update_slice(_pack_cache(cv, layer, l_pad), vb3, (0, past, 0))
        ik23 = jnp.concatenate([cik2, ik23, jnp.zeros((n_seq, l_pad - past - seq_len, 2 * LANES), BF16)], axis=1)
    pad_q = -seq_len % tq
    if pad_q:
        q3 = jnp.pad(q3, ((0, 0), (0, pad_q), (0, 0)))
        iq3 = jnp.pad(iq3, ((0, 0), (0, pad_q), (0, 0)))
        iwt = jnp.pad(iwt, ((0, 0), (0, 0), (0, pad_q)))
    attn = _attend(q3, iq3, iwt, kb3, vb3, ik23, tq=tq, tq_real=min(seq_len, tq), q0=pos0, topk=topk)
    attn = attn[:, :seq_len].reshape(m, ATTN_W)

    zbc, cstate, pstate = _proj_b(x, lw["g"], lw["w_b"], lw["conv_w"], lw["conv_b"], lw["pool_w"],
                                  lw["pool_scale"], lw["lift_b"], lw["lift_c"], hist[0], hist[1],
                                  seq_len=seq_len, tm=tm_seq, pos0=pos0)
    outs = _merge(x, lw["g"], lw["w_ma"], attn, ga, lw["lift_a"], zbc, lw["w_out"], final_g,
                  tm=tm_tok, final=final)
    states = (k.reshape(n_seq, seq_len, N_KV_HEADS, HEAD_DIM), v.reshape(n_seq, seq_len, N_KV_HEADS, HEAD_DIM),
              ik.reshape(n_seq, seq_len, IDX_DIM), cstate[:, CONV_PAD - (CONV_K - 1):], pstate[:, POOL_PAD - POOL_HIST:])
    return outs, states


def kernel(x_prompt, x_sample, cache_k, cache_v, cache_idx_k, state_conv, state_pool, norm_g, w_in, conv_w,
           conv_b, pool_w, pool_scale, lift_a, lift_b, lift_c, w_out, final_norm_g):
    batch, seq, _ = x_prompt.shape
    dec_batch, dec_seq, _ = x_sample.shape
    depth = w_in.shape[0]
    past_len = cache_k.shape[2]
    topk_prompt = min(MAX_TOPK, seq // 4)
    topk_sample = min(MAX_TOPK, (past_len + dec_seq) // 4)

    tm_p, tm_s = 256, 256
    tabs_p = _rope_tables(jnp.arange(seq, dtype=jnp.int32))
    tabs_s = _rope_tables(past_len + (jnp.arange(tm_s, dtype=jnp.int32) % dec_seq))
    final_g = final_norm_g.reshape(1, D_MODEL)

    hp = x_prompt.reshape(batch * seq, D_MODEL)
    hs = x_sample.reshape(dec_batch * dec_seq, D_MODEL)
    zero_hist = (jnp.zeros((batch, CONV_PAD, CONV_W), F32), jnp.zeros((batch, POOL_PAD, POOL_W), F32))
    p_states, s_states = [], []
    y_p = y_s = None
    for l in range(depth):
        wl = w_in[l]
        lw = {
            "g": norm_g[l].reshape(1, D_MODEL),
            "w_a": jnp.concatenate([wl[:, O_Q:O_GA], jnp.zeros((D_MODEL, A_GA - A_IKW - IDX_DIM - N_IDX_HEADS), F32),
                                    wl[:, O_GA:O_U]], axis=1).astype(BF16),
            "w_b": jnp.concatenate([wl[:, O_U:O_MA], wl[:, O_MB:O_END]], axis=1).astype(BF16),
            "w_ma": wl[:, O_MA:O_MB].astype(BF16),
            "conv_w": conv_w[l], "conv_b": conv_b[l].reshape(1, CONV_W),
            "pool_w": pool_w[l].astype(BF16), "pool_scale": pool_scale[l].reshape(1, POOL_W),
            "lift_a": lift_a[l].astype(BF16), "lift_b": lift_b[l].astype(BF16), "lift_c": lift_c[l].astype(BF16),
            "w_out": w_out[l].astype(BF16),
        }
        final = l == depth - 1
        outs_p, st_p = _layer(hp, tabs_p, lw, zero_hist, None, n_seq=batch, seq_len=seq, pos0=0,
                              topk=topk_prompt, tm_tok=tm_p, tm_seq=tm_p, tq=Q_TILE_PROMPT, final=final,
                              final_g=final_g)
        cik = cache_idx_k[l].astype(BF16)
        cik2 = jnp.concatenate([cik, jnp.zeros_like(cik), jnp.zeros_like(cik), cik], axis=-1)
        cache = (cache_k, cache_v, cik2, l)
        hist_s = (jnp.pad(state_conv[l], ((0, 0), (CONV_PAD - (CONV_K - 1), 0), (0, 0))),
                  jnp.pad(state_pool[l], ((0, 0), (POOL_PAD - POOL_HIST, 0), (0, 0))))
        outs_s, st_s = _layer(hs, tabs_s, lw, hist_s, cache, n_seq=dec_batch, seq_len=dec_seq, pos0=past_len,
                              topk=topk_sample, tm_tok=tm_s, tm_seq=tm_s, tq=Q_TILE_SAMPLE, final=final,
                              final_g=final_g)
        hp, hs = outs_p[0], outs_s[0]
        if final:
            y_p, y_s = outs_p[1], outs_s[1]
        p_states.append(st_p)
        s_states.append(st_s)

    stack = lambda sts, i: jnp.stack([st[i] for st in sts])
    return (y_p.reshape(batch, seq, D_MODEL), y_s.reshape(dec_batch, dec_seq, D_MODEL),
            stack(p_states, 0), stack(p_states, 1), stack(p_states, 2), stack(p_states, 3), stack(p_states, 4),
            stack(s_states, 0), stack(s_states, 1), stack(s_states, 2), stack(s_states, 3), stack(s_states, 4))
```

```python
import functools

import jax
import jax.numpy as jnp
from jax import lax
from jax.experimental import pallas as pl
from jax.experimental.pallas import tpu as pltpu

D_MODEL = 2048
CHUNK = 64
N_HEADS = 8
HEAD_DIM = 128
N_KV_HEADS = 2
GROUP = N_HEADS // N_KV_HEADS
ATTN_W = N_HEADS * HEAD_DIM
KV_W = N_KV_HEADS * HEAD_DIM
N_IDX_HEADS = 16
IDX_DIM = 64
IDX_Q_W = N_IDX_HEADS * IDX_DIM
MAX_TOPK = 256
CONV_W = 512
CONV_K = 3
POOL_W = 512
POOL_WINDOWS = (2, 4, 8, 16)
POOL_GROUP = 128
POOL_HIST = 15
ROPE_THETA = 10000.0
EPS = 1e-6

LANES = 128
CONV_PAD = 8
POOL_PAD = 16
KEY_BLOCK = 512
PACK_ROWS = 1024
Q_TILE_PROMPT = 256
Q_TILE_SAMPLE = 128
LOGIT_LANES = 1024
NEG_BIAS = -1e30
COUNT_CHAINS = 4
BISECT_UNROLL = 4
BISECT_UNCHECKED = 3
LOG2E = 1.4426950408889634
F32_LOWEST = -3.0e38
VMEM_LIMIT = 56 * 1024 * 1024

F32 = jnp.float32
BF16 = jnp.bfloat16

_SIZES = (ATTN_W, KV_W, KV_W, IDX_Q_W, IDX_DIM, N_IDX_HEADS, ATTN_W,
          CONV_W, CONV_W, CONV_W, CONV_W, POOL_W, POOL_W, D_MODEL, D_MODEL, D_MODEL)
_OFFS = [0]
for _s in _SIZES:
    _OFFS.append(_OFFS[-1] + _s)
(O_Q, O_K, O_V, O_IQ, O_IK, O_IW, O_GA, O_U, O_BG, O_CG, O_GB, O_PIN, O_GC, O_MA, O_MB, O_MC, O_END) = _OFFS

A_Q, A_K, A_V, A_IQ, A_IKW, A_GA, A_END = 0, 1024, 1280, 1536, 2560, 2688, 3712
B_U, B_BG, B_CG, B_GB, B_PIN, B_GC, B_MB, B_MC, B_END = 0, 512, 1024, 1536, 2048, 2560, 3072, 5120, 7168


def _dot(a, b):
    return jnp.dot(a, b, preferred_element_type=F32)


def _dot_nt(a, b):
    return lax.dot_general(a, b, (((1,), (1,)), ((), ())), preferred_element_type=F32)


def _rms_h(x, g):
    h = x * lax.rsqrt(jnp.mean(x * x, axis=-1, keepdims=True) + EPS) * g
    return h.astype(BF16)


def _silu(x):
    return x * jax.nn.sigmoid(x)


def _const_spec(shape):
    nd = len(shape)
    return pl.BlockSpec(shape, lambda *_: (0,) * nd, pipeline_mode=pl.Buffered(1))


def _proj_a_kernel(x_ref, g_ref, w_ref, c128_ref, s128_ref, c64_ref, s64_ref,
                   q_ref, k_ref, v_ref, kb_ref, vb_ref, iq_ref, ik_ref, ik2_ref, iw_ref, ga_ref):
    h = _rms_h(x_ref[...], g_ref[...])
    c128 = c128_ref[...]
    s128 = s128_ref[...]
    c64 = c64_ref[...]
    s64 = s64_ref[...]
    lane = lax.broadcasted_iota(jnp.int32, (1, LANES), 1)
    first_half64 = (lane % IDX_DIM) < (IDX_DIM // 2)

    def rope128(y):
        return y * c128 + pltpu.roll(y, HEAD_DIM // 2, 1) * s128

    def rope64(y):
        partner = jnp.where(first_half64, pltpu.roll(y, LANES - IDX_DIM // 2, 1),
                            pltpu.roll(y, IDX_DIM // 2, 1))
        return y * c64 + partner * s64

    y = _dot(h, w_ref[:, A_Q:A_K])
    for hd in range(N_HEADS):
        sl = slice(hd * HEAD_DIM, (hd + 1) * HEAD_DIM)
        q_ref[:, sl] = (rope128(y[:, sl]) * (HEAD_DIM ** -0.5 * LOG2E)).astype(BF16)

    y = _dot(h, w_ref[:, A_K:A_IQ])
    for hd in range(N_KV_HEADS):
        sl = slice(hd * HEAD_DIM, (hd + 1) * HEAD_DIM)
        kr = rope128(y[:, sl])
        rows_hd = pl.ds(hd, y.shape[0], stride=N_KV_HEADS)
        k_ref[rows_hd, :] = kr
        kb_ref[:, sl] = kr.astype(BF16)
        v_ref[rows_hd, :] = y[:, KV_W + hd * HEAD_DIM:KV_W + (hd + 1) * HEAD_DIM]
    vb_ref[...] = y[:, KV_W:].astype(BF16)

    y = _dot(h, w_ref[:, A_IQ:A_IKW])
    for c in range(IDX_Q_W // LANES):
        sl = slice(c * LANES, (c + 1) * LANES)
        iq_ref[:, sl] = rope64(y[:, sl]).astype(BF16)

    y = _dot(h, w_ref[:, A_IKW:A_GA])
    ikr = rope64(y)
    ik_ref[...] = ikr[:, :IDX_DIM]
    ikz = jnp.where(lane < IDX_DIM, ikr, 0.0)
    ik2_ref[:, :LANES] = ikz.astype(BF16)
    ik2_ref[:, LANES:] = pltpu.roll(ikz, IDX_DIM, 1).astype(BF16)
    iw_ref[...] = y[:, IDX_DIM:IDX_DIM + N_IDX_HEADS] * ((IDX_DIM ** -0.5) * (N_IDX_HEADS ** -0.5))

    ga_ref[...] = _silu(_dot(h, w_ref[:, A_GA:A_END])).astype(BF16)


def _proj_a(x, g, w_a, tabs, tm):
    m = x.shape[0]
    c128, s128, c64, s64 = tabs
    n_pt = c128.shape[0] // tm
    row = lambda w: pl.BlockSpec((tm, w), lambda i: (i, 0))
    tab = pl.BlockSpec((tm, LANES), lambda i: (i % n_pt, 0))
    out_shape = (
        jax.ShapeDtypeStruct((m, ATTN_W), BF16),
        jax.ShapeDtypeStruct((m * N_KV_HEADS, HEAD_DIM), F32),
        jax.ShapeDtypeStruct((m * N_KV_HEADS, HEAD_DIM), F32),
        jax.ShapeDtypeStruct((m, KV_W), BF16),
        jax.ShapeDtypeStruct((m, KV_W), BF16),
        jax.ShapeDtypeStruct((m, IDX_Q_W), BF16),
        jax.ShapeDtypeStruct((m, IDX_DIM), F32),
        jax.ShapeDtypeStruct((m, 2 * LANES), BF16),
        jax.ShapeDtypeStruct((m, N_IDX_HEADS), F32),
        jax.ShapeDtypeStruct((m, ATTN_W), BF16),
    )
    kv_heads = pl.BlockSpec((tm * N_KV_HEADS, HEAD_DIM), lambda i: (i, 0))
    out_specs = (row(ATTN_W), kv_heads, kv_heads, row(KV_W), row(KV_W), row(IDX_Q_W),
                 row(IDX_DIM), row(2 * LANES), row(N_IDX_HEADS), row(ATTN_W))
    return pl.pallas_call(
        _proj_a_kernel,
        grid=(m // tm,),
        in_specs=[row(D_MODEL), _const_spec((1, D_MODEL)), _const_spec((D_MODEL, A_END)),
                  tab, tab, tab, tab],
        out_specs=out_specs,
        out_shape=out_shape,
        compiler_params=pltpu.CompilerParams(dimension_semantics=("arbitrary",),
                                             vmem_limit_bytes=VMEM_LIMIT),
        name="proj_a",
    )(x, g, w_a, c128, s128, c64, s64)


def _paired_blocks(n_blocks, trip):
    def pair(t, carry):
        trip(2 * t, 2)
        return carry

    lax.fori_loop(0, n_blocks // 2, pair, 0)

    @pl.when(n_blocks % 2 == 1)
    def _():
        trip(n_blocks - 1, 1)


def _attend_kernel(q_ref, iq_ref, iwt_ref, kb_ref, vb_ref, ik2_ref, o_ref,
                   st_ref, sbuf_ref, state_ref, acc_ref, stat_ref,
                   *, tq, tq_real, q0, topk):
    kb_sz = KEY_BLOCK
    i = pl.program_id(1)
    qpos0 = q0 + i * tq_real
    n_keys = ((qpos0 + tq_real - 1) // CHUNK + 1) * CHUNK
    nkb = (n_keys + kb_sz - 1) // kb_sz

    lane_q = lax.broadcasted_iota(jnp.int32, (1, tq), 1)
    n_adm_i = ((qpos0 + lane_q) // CHUNK + 1) * CHUNK
    iw = iwt_ref[0]

    state_ref[0:1, :] = jnp.full((1, tq), jnp.inf, F32)
    state_ref[1:2, :] = jnp.full((1, tq), -jnp.inf, F32)

    def score_trip(kb0, n_blk):
        mn, mx = state_ref[0:1, :], state_ref[1:2, :]
        for kb in [kb0 + d for d in range(n_blk)]:
            r0 = pl.multiple_of(kb * kb_sz, kb_sz)
            ik = ik2_ref[0, pl.ds(r0, kb_sz), :]
            ik_even = ik[:, :LANES]
            ik_odd = ik[:, LANES:]
            acc = jnp.zeros((kb_sz, tq), F32)
            for p in range(N_IDX_HEADS // 2):
                iq_pair = iq_ref[0, :, p * LANES:(p + 1) * LANES]
                d0 = _dot_nt(ik_even, iq_pair)
                d1 = _dot_nt(ik_odd, iq_pair)
                acc = acc + jnp.maximum(d0, 0.0) * iw[2 * p:2 * p + 1]
                acc = acc + jnp.maximum(d1, 0.0) * iw[2 * p + 1:2 * p + 2]
            key = r0 + lax.broadcasted_iota(jnp.int32, (kb_sz, 1), 0)
            masked = jnp.where(key < n_adm_i, acc, -jnp.inf)
            st_ref[pl.ds(r0, kb_sz), :] = masked
            mx = jnp.maximum(mx, jnp.max(masked, axis=0, keepdims=True))
            mn = jnp.minimum(mn, jnp.min(acc, axis=0, keepdims=True))
        state_ref[0:1, :] = mn
        state_ref[1:2, :] = mx

    _paired_blocks(nkb, score_trip)
    mn, mx = state_ref[0:1, :], state_ref[1:2, :]

    n_adm = n_adm_i.astype(F32)
    topk_f = float(topk)
    active0 = jnp.logical_and(n_adm > topk_f, lane_q < tq_real).astype(F32)
    state_ref[0:1, :] = mn
    state_ref[1:2, :] = mx + jnp.maximum(jnp.abs(mx), 1e-30)
    state_ref[2:3, :] = jnp.full((1, tq), F32_LOWEST, F32)
    state_ref[3:4, :] = active0
    state_ref[4:5, :] = n_adm
    state_ref[5:6, :] = jnp.zeros((1, tq), F32)
    state_ref[6:7, :] = jnp.zeros((1, tq), F32)

    def count_ge(mid):
        def cnt_blk(kb, acc):
            r0 = pl.multiple_of(kb * kb_sz, kb_sz)
            ge = (st_ref[pl.ds(r0, kb_sz), :] >= mid).astype(F32)
            return acc + jnp.sum(ge.reshape(COUNT_CHAINS, kb_sz // (8 * COUNT_CHAINS), 8, tq), axis=1)

        acc = lax.fori_loop(0, nkb, cnt_blk, jnp.zeros((COUNT_CHAINS, 8, tq), F32))
        return jnp.sum(jnp.sum(acc, axis=0), axis=0, keepdims=True)

    def bisect_step(st):
        lo, hi, thr, active, clo, tie, pend = st
        mid = 0.5 * lo + 0.5 * hi
        stuck = jnp.logical_or(mid <= lo, mid >= hi)
        cnt = count_ge(mid)
        ge_k = cnt >= topk_f
        exact = cnt == topk_f
        moving = jnp.logical_and(active, jnp.logical_not(stuck))
        ended = jnp.logical_and(active, stuck)
        thr = jnp.where(ended, lo, jnp.where(jnp.logical_and(moving, exact), mid, thr))
        tie = jnp.logical_or(tie, jnp.logical_and(ended, clo > topk_f))
        up = jnp.logical_and(moving, ge_k)
        down = jnp.logical_and(moving, jnp.logical_not(ge_k))
        near = jnp.logical_and(moving, cnt == topk_f - 1.0)
        done = jnp.logical_or(exact, near)
        return (jnp.where(up, mid, lo), jnp.where(down, mid, hi), thr,
                jnp.logical_and(moving, jnp.logical_not(done)), jnp.where(up, cnt, clo), tie,
                jnp.logical_or(pend, near))

    def bisect_steps():
        st = (state_ref[0:1, :], state_ref[1:2, :], state_ref[2:3, :], state_ref[3:4, :] > 0.0,
              state_ref[4:5, :], state_ref[5:6, :] > 0.0, state_ref[6:7, :] > 0.0)
        for _ in range(BISECT_UNROLL):
            st = bisect_step(st)
        active_f = st[3].astype(F32)
        state_ref[0:1, :] = st[0]
        state_ref[1:2, :] = st[1]
        state_ref[2:3, :] = st[2]
        state_ref[3:4, :] = active_f
        state_ref[4:5, :] = st[4]
        state_ref[5:6, :] = st[5].astype(F32)
        state_ref[6:7, :] = st[6].astype(F32)
        return active_f

    def unchecked(i, carry):
        bisect_steps()
        return carry

    lax.fori_loop(0, BISECT_UNCHECKED, unchecked, 0)
    lax.while_loop(lambda c: jnp.logical_and(c[0] > 0.0, c[1] < 128),
                   lambda c: (jnp.max(bisect_steps()), c[1] + 1),
                   (jnp.max(state_ref[3:4, :]), jnp.int32(0)))

    @pl.when(jnp.max(state_ref[6:7, :]) > 0.0)
    def _():
        pend = state_ref[6:7, :] > 0.0
        hi = state_ref[1:2, :]

        def below_blk(kb, acc):
            r0 = pl.multiple_of(kb * kb_sz, kb_sz)
            blk = st_ref[pl.ds(r0, kb_sz), :]
            below = jnp.where(blk < hi, blk, -jnp.inf)
            return jnp.maximum(acc, jnp.max(below.reshape(COUNT_CHAINS, kb_sz // (8 * COUNT_CHAINS), 8, tq), axis=1))

        top = lax.fori_loop(0, nkb, below_blk, jnp.full((COUNT_CHAINS, 8, tq), -jnp.inf, F32))
        thr_p = jnp.where(pend, jnp.max(jnp.max(top, axis=0), axis=0, keepdims=True), state_ref[2:3, :])
        state_ref[2:3, :] = thr_p
        dup = jnp.logical_and(pend, count_ge(thr_p) > topk_f)
        state_ref[5:6, :] = jnp.maximum(state_ref[5:6, :], dup.astype(F32))

    thr = state_ref[2:3, :]

    @pl.when(jnp.max(state_ref[5:6, :]) > 0.0)
    def _():
        tie_row = state_ref[5:6, :] > 0.0

        def gt_blk(kb, acc):
            r0 = pl.multiple_of(kb * kb_sz, kb_sz)
            return acc + jnp.sum((st_ref[pl.ds(r0, kb_sz), :] > thr).astype(F32), axis=0, keepdims=True)

        need = topk_f - lax.fori_loop(0, nkb, gt_blk, jnp.zeros((1, tq), F32))
        tri = (lax.broadcasted_iota(jnp.int32, (kb_sz, kb_sz), 0)
               >= lax.broadcasted_iota(jnp.int32, (kb_sz, kb_sz), 1)).astype(BF16)

        def fix_blk(kb, run):
            r0 = pl.multiple_of(kb * kb_sz, kb_sz)
            blk = st_ref[pl.ds(r0, kb_sz), :]
            eq = jnp.logical_and(blk == thr, tie_row)
            eq_f = eq.astype(F32)
            rank = _dot(tri, eq_f.astype(BF16)) - eq_f + run
            st_ref[pl.ds(r0, kb_sz), :] = jnp.where(jnp.logical_and(eq, rank >= need), -jnp.inf, blk)
            return run + jnp.sum(eq_f, axis=0, keepdims=True)

        lax.fori_loop(0, nkb, fix_blk, jnp.zeros((1, tq), F32))

    red = lambda a, op: op(a.reshape(2, kb_sz // 16, 8, a.shape[-1]), axis=1)
    gw = GROUP * tq
    heads_per_trip = LOGIT_LANES // gw
    for j0 in range(0, N_KV_HEADS, heads_per_trip):
        trip_heads = range(j0, j0 + heads_per_trip)
        qs = {j: jnp.concatenate(
            [q_ref[0, :, (GROUP * j + g) * HEAD_DIM:(GROUP * j + g + 1) * HEAD_DIM] for g in range(GROUP)],
            axis=0) for j in trip_heads}

        def qk_trip(kb0, n_blk):
            m8 = stat_ref[...]
            for kb in [kb0 + d for d in range(n_blk)]:
                r0 = pl.multiple_of(kb * kb_sz, kb_sz)
                bias = jnp.where(st_ref[pl.ds(r0, kb_sz), :] >= thr, 0.0, NEG_BIAS)
                parts = []
                for j in trip_heads:
                    s = _dot_nt(kb_ref[0, pl.ds(r0, kb_sz), j * HEAD_DIM:(j + 1) * HEAD_DIM], qs[j])
                    for g in range(GROUP):
                        sg = s[:, g * tq:(g + 1) * tq] + bias
                        c0 = (j - j0) * gw + g * tq
                        sbuf_ref[kb, :, c0:c0 + tq] = sg
                        parts.append(red(sg, jnp.max))
                m8 = jnp.maximum(m8, jnp.concatenate(parts, axis=-1))
            stat_ref[...] = m8

        stat_ref[...] = jnp.full(stat_ref.shape, NEG_BIAS, F32)
        _paired_blocks(nkb, qk_trip)
        m = jnp.max(jnp.max(stat_ref[...], axis=0), axis=0, keepdims=True)
        stat_ref[...] = jnp.zeros(stat_ref.shape, F32)
        acc_ref[...] = jnp.zeros(acc_ref.shape, F32)

        def pv_trip(kb0, n_blk):
            sums = []
            for j in trip_heads:
                cols = slice((j - j0) * gw, (j - j0 + 1) * gw)
                pv, psum = 0.0, 0.0
                for kb in [kb0 + d for d in range(n_blk)]:
                    r0 = pl.multiple_of(kb * kb_sz, kb_sz)
                    p = jnp.exp2(sbuf_ref[kb, :, cols] - m[:, cols])
                    pv = pv + lax.dot_general(vb_ref[0, pl.ds(r0, kb_sz), j * HEAD_DIM:(j + 1) * HEAD_DIM],
                                              p.astype(BF16), (((0,), (0,)), ((), ())),
                                              preferred_element_type=F32)
                    psum = psum + red(p, jnp.sum)
                acc_ref[:, cols] += pv
                sums.append(psum)
            stat_ref[...] += jnp.concatenate(sums, axis=-1)

        _paired_blocks(nkb, pv_trip)
        o = acc_ref[...] / jnp.sum(jnp.sum(stat_ref[...], axis=0), axis=0, keepdims=True)
        for j in trip_heads:
            for g in range(GROUP):
                hd = GROUP * j + g
                c0 = (j - j0) * gw + g * tq
                o_ref[0, :, hd * HEAD_DIM:(hd + 1) * HEAD_DIM] = o[:, c0:c0 + tq].T.astype(o_ref.dtype)


def _attend(q, iq, iwt, kb, vb, ik2, *, tq, tq_real, q0, topk):
    b, t_q, _ = q.shape
    l_keys = kb.shape[1]
    n_q = t_q // tq
    n_kb = l_keys // KEY_BLOCK
    qspec = pl.BlockSpec((1, tq, ATTN_W), lambda bi, i: (bi, i, 0))
    kspec = pl.BlockSpec((1, l_keys, KV_W), lambda bi, i: (bi, 0, 0))
    return pl.pallas_call(
        functools.partial(_attend_kernel, tq=tq, tq_real=tq_real, q0=q0, topk=topk),
        grid=(b, n_q),
        in_specs=[qspec, qspec, pl.BlockSpec((1, N_IDX_HEADS, tq), lambda bi, i: (bi, 0, i)),
                  kspec, kspec, kspec],
        out_specs=qspec,
        out_shape=jax.ShapeDtypeStruct((b, t_q, ATTN_W), F32),
        scratch_shapes=[
            pltpu.VMEM((l_keys, tq), F32),
            pltpu.VMEM((n_kb, KEY_BLOCK, LOGIT_LANES), F32),
            pltpu.VMEM((8, tq), F32),
            pltpu.VMEM((HEAD_DIM, LOGIT_LANES), F32),
            pltpu.VMEM((2, 8, LOGIT_LANES), F32),
        ],
        compiler_params=pltpu.CompilerParams(dimension_semantics=("arbitrary", "arbitrary"),
                                             vmem_limit_bytes=VMEM_LIMIT),
        name="attend",
    )(q, iq, iwt, kb, vb, ik2)


def _pack_cache_kernel(c_ref, o_ref, *, n_src):
    j = pl.program_id(1)

    @pl.when(j < n_src)
    def _():
        for hd in range(N_KV_HEADS):
            rows_hd = pl.ds(hd, PACK_ROWS, stride=N_KV_HEADS)
            o_ref[0, :, hd * HEAD_DIM:(hd + 1) * HEAD_DIM] = c_ref[0, 0, rows_hd, :].astype(BF16)

    @pl.when(j >= n_src)
    def _():
        o_ref[...] = jnp.zeros(o_ref.shape, BF16)


def _pack_cache(c, layer, l_pad):
    depth, b, past = c.shape[:3]
    n_src = past // PACK_ROWS
    c = c.reshape(depth, b, past * N_KV_HEADS, HEAD_DIM)
    return pl.pallas_call(
        functools.partial(_pack_cache_kernel, n_src=n_src),
        grid=(b, l_pad // PACK_ROWS),
        in_specs=[pl.BlockSpec((1, 1, PACK_ROWS * N_KV_HEADS, HEAD_DIM),
                               lambda bi, j: (layer, bi, jnp.minimum(j, n_src - 1), 0))],
        out_specs=pl.BlockSpec((1, PACK_ROWS, KV_W), lambda bi, j: (bi, j, 0)),
        out_shape=jax.ShapeDtypeStruct((b, l_pad, KV_W), BF16),
        compiler_params=pltpu.CompilerParams(dimension_semantics=("arbitrary", "arbitrary"),
                                             vmem_limit_bytes=VMEM_LIMIT),
        name="pack_cache",
    )(c)


def _proj_b_kernel(x_ref, g_ref, w_ref, convw_ref, convb_ref, poolw_ref, pscale_ref, liftb_ref, liftc_ref,
                   chist_ref, phist_ref, zbc_ref, cstate_ref, pstate_ref, cin_ext, pin_ext,
                   *, n_sub, ls, pos0):
    j = pl.program_id(1)

    @pl.when(j == 0)
    def _():
        cin_ext[:, 0:CONV_PAD, :] = chist_ref[...]
        pin_ext[:, 0:POOL_PAD, :] = phist_ref[...]

    @pl.when(j > 0)
    def _():
        cin_ext[:, 0:CONV_PAD, :] = cin_ext[:, ls:ls + CONV_PAD, :]
        pin_ext[:, 0:POOL_PAD, :] = pin_ext[:, ls:ls + POOL_PAD, :]

    h = _rms_h(x_ref[...], g_ref[...])
    rows = lambda a: jnp.concatenate(a, axis=0) if n_sub > 1 else a[0]

    y = _dot(h, w_ref[:, B_U:B_PIN])
    u = y[:, 0:CONV_W]
    b_gate = y[:, CONV_W:2 * CONV_W]
    c_gate = y[:, 2 * CONV_W:3 * CONV_W]
    gate_b = y[:, 3 * CONV_W:4 * CONV_W]
    cin = c_gate * u
    for s in range(n_sub):
        cin_ext[s, CONV_PAD:CONV_PAD + ls, :] = cin[s * ls:(s + 1) * ls]
    conv = (rows([cin_ext[s, CONV_PAD - 2:CONV_PAD - 2 + ls, :] for s in range(n_sub)]) * convw_ref[0:1, :]
            + rows([cin_ext[s, CONV_PAD - 1:CONV_PAD - 1 + ls, :] for s in range(n_sub)]) * convw_ref[1:2, :]
            + cin * convw_ref[2:3, :] + convb_ref[...])
    y_b = b_gate * conv * _silu(gate_b)
    zb = _dot(y_b.astype(BF16), liftb_ref[...])

    y = _dot(h, w_ref[:, B_PIN:B_MB])
    p_in = y[:, 0:POOL_W]
    gate_c = y[:, POOL_W:2 * POOL_W]
    for s in range(n_sub):
        pin_ext[s, POOL_PAD:POOL_PAD + ls, :] = p_in[s * ls:(s + 1) * ls]
    pos = rows([pos0 + j * ls + lax.broadcasted_iota(jnp.int32, (ls, 1), 0)] * n_sub)
    yc_parts = []
    for gi, win in enumerate(POOL_WINDOWS):
        sl = slice(gi * POOL_GROUP, (gi + 1) * POOL_GROUP)
        tot = p_in[:, sl]
        for back in range(1, win):
            tot = tot + rows([pin_ext[s, POOL_PAD - back:POOL_PAD - back + ls, sl] for s in range(n_sub)])
        cnt = jnp.minimum(win, pos + 1).astype(F32)
        d = tot / cnt - p_in[:, sl]
        mixed = _dot(d.astype(BF16), poolw_ref[gi])
        yc_parts.append(mixed * pscale_ref[:, sl] * _silu(gate_c[:, sl]))
    y_c = jnp.concatenate(yc_parts, axis=-1)
    zc = _dot(y_c.astype(BF16), liftc_ref[...])

    m_b = _dot(h, w_ref[:, B_MB:B_MC])
    m_c = _dot(h, w_ref[:, B_MC:B_END])
    zbc_ref[...] = jax.nn.sigmoid(m_b) * zb + jax.nn.sigmoid(m_c) * zc

    @pl.when(j == pl.num_programs(1) - 1)
    def _():
        cstate_ref[...] = cin_ext[:, ls:ls + CONV_PAD, :]
        pstate_ref[...] = pin_ext[:, ls:ls + POOL_PAD, :]


def _proj_b(x, g, w_b, conv_w, conv_b, pool_w, pool_scale, lift_b, lift_c, chist, phist, *, seq_len, tm, pos0):
    m = x.shape[0]
    n_seq = m // seq_len
    n_sub, ls = (1, tm) if tm <= seq_len else (tm // seq_len, seq_len)
    n_t = seq_len // ls
    out_shape = (
        jax.ShapeDtypeStruct((m, D_MODEL), F32),
        jax.ShapeDtypeStruct((n_seq, CONV_PAD, CONV_W), F32),
        jax.ShapeDtypeStruct((n_seq, POOL_PAD, POOL_W), F32),
    )
    hist = lambda r, w: pl.BlockSpec((n_sub, r, w), lambda s, j: (s, 0, 0))
    return pl.pallas_call(
        functools.partial(_proj_b_kernel, n_sub=n_sub, ls=ls, pos0=pos0),
        grid=(n_seq // n_sub, n_t),
        in_specs=[pl.BlockSpec((tm, D_MODEL), lambda s, j: (s * n_t + j, 0)),
                  _const_spec((1, D_MODEL)), _const_spec((D_MODEL, B_END)),
                  _const_spec((CONV_K, CONV_W)), _const_spec((1, CONV_W)),
                  _const_spec((len(POOL_WINDOWS), POOL_GROUP, POOL_GROUP)), _const_spec((1, POOL_W)),
                  _const_spec((CONV_W, D_MODEL)), _const_spec((POOL_W, D_MODEL)),
                  hist(CONV_PAD, CONV_W), hist(POOL_PAD, POOL_W)],
        out_specs=(pl.BlockSpec((tm, D_MODEL), lambda s, j: (s * n_t + j, 0)),
                   hist(CONV_PAD, CONV_W), hist(POOL_PAD, POOL_W)),
        out_shape=out_shape,
        scratch_shapes=[pltpu.VMEM((n_sub, ls + CONV_PAD, CONV_W), F32),
                        pltpu.VMEM((n_sub, ls + POOL_PAD, POOL_W), F32)],
        compiler_params=pltpu.CompilerParams(dimension_semantics=("arbitrary", "arbitrary"),
                                             vmem_limit_bytes=VMEM_LIMIT),
        name="proj_b",
    )(x, g, w_b, conv_w, conv_b, pool_w, pool_scale, lift_b, lift_c, chist, phist)


def _merge_kernel(x_ref, g_ref, wma_ref, attn_ref, ga_ref, lifta_ref, zbc_ref, wout_ref, fg_ref,
                  out_ref, *, final):
    x = x_ref[...]
    h = _rms_h(x, g_ref[...])
    y_a = attn_ref[...] * ga_ref[...].astype(F32)
    z = jax.nn.sigmoid(_dot(h, wma_ref[...])) * _dot(y_a.astype(BF16), lifta_ref[...]) + zbc_ref[...]
    out = x + _dot(z.astype(BF16), wout_ref[...])
    if final:
        out = out * lax.rsqrt(jnp.mean(out * out, axis=-1, keepdims=True) + EPS) * fg_ref[...]
    out_ref[...] = out


def _merge(x, g, w_ma, attn, ga, lift_a, zbc, w_out, final_g, *, tm, final):
    m = x.shape[0]
    row = lambda w: pl.BlockSpec((tm, w), lambda i: (i, 0))
    return pl.pallas_call(
        functools.partial(_merge_kernel, final=final),
        grid=(m // tm,),
        in_specs=[row(D_MODEL), _const_spec((1, D_MODEL)), _const_spec((D_MODEL, D_MODEL)),
                  row(ATTN_W), row(ATTN_W), _const_spec((ATTN_W, D_MODEL)), row(D_MODEL),
                  _const_spec((D_MODEL, D_MODEL)), _const_spec((1, D_MODEL))],
        out_specs=row(D_MODEL),
        out_shape=jax.ShapeDtypeStruct((m, D_MODEL), F32),
        compiler_params=pltpu.CompilerParams(dimension_semantics=("arbitrary",),
                                             vmem_limit_bytes=VMEM_LIMIT),
        name="merge",
    )(x, g, w_ma, attn, ga, lift_a, zbc, w_out, final_g)


def _rope_tables(pos):
    def tab(half, reps):
        inv = ROPE_THETA ** (-jnp.arange(half, dtype=F32) / half)
        ang = pos.astype(F32)[:, None] * inv[None, :]
        cos, sin = jnp.cos(ang), jnp.sin(ang)
        return (jnp.tile(jnp.concatenate([cos, cos], axis=-1), (1, reps)),
                jnp.tile(jnp.concatenate([-sin, sin], axis=-1), (1, reps)))
    c128, s128 = tab(HEAD_DIM // 2, 1)
    c64, s64 = tab(IDX_DIM // 2, LANES // IDX_DIM)
    return c128, s128, c64, s64


def _layer(x, tabs, lw, hist, cache, *, n_seq, seq_len, pos0, topk, tm_tok, tm_seq, tq, final, final_g):
    m = n_seq * seq_len
    q, k, v, kb, vb, iq, ik, ik2, iw, ga = _proj_a(x, lw["g"], lw["w_a"], tabs, tm_tok)

    def seq(a):
        return a.reshape(n_seq, seq_len, a.shape[-1])

    iwt = jnp.swapaxes(seq(iw), 1, 2)
    q3, iq3, kb3, vb3, ik23 = seq(q), seq(iq), seq(kb), seq(vb), seq(ik2)
    if cache is not None:
        ck, cv, cik2, layer = cache
        past = ck.shape[2]
        l_pad = past + seq_len + (-(past + seq_len) % PACK_ROWS)
        kb3 = lax.dynamic_update_slice(_pack_cache(ck, layer, l_pad), kb3, (0, past, 0))
        vb3 = lax.dynamic_update_slice(_pack_cache(cv, layer, l_pad), vb3, (0, past, 0))
        ik23 = jnp.concatenate([cik2, ik23, jnp.zeros((n_seq, l_pad - past - seq_len, 2 * LANES), BF16)], axis=1)
    pad_q = -seq_len % tq
    if pad_q:
        q3 = jnp.pad(q3, ((0, 0), (0, pad_q), (0, 0)))
        iq3 = jnp.pad(iq3, ((0, 0), (0, pad_q), (0, 0)))
        iwt = jnp.pad(iwt, ((0, 0), (0, 0), (0, pad_q)))
    attn = _attend(q3, iq3, iwt, kb3, vb3, ik23, tq=tq, tq_real=min(seq_len, tq), q0=pos0, topk=topk)
    attn = attn[:, :seq_len].reshape(m, ATTN_W)

    zbc, cstate, pstate = _proj_b(x, lw["g"], lw["w_b"], lw["conv_w"], lw["conv_b"], lw["pool_w"],
                                  lw["pool_scale"], lw["lift_b"], lw["lift_c"], hist[0], hist[1],
                                  seq_len=seq_len, tm=tm_seq, pos0=pos0)
    outs = _merge(x, lw["g"], lw["w_ma"], attn, ga, lw["lift_a"], zbc, lw["w_out"], final_g,
                  tm=tm_tok, final=final)
    states = (k.reshape(n_seq, seq_len, N_KV_HEADS, HEAD_DIM), v.reshape(n_seq, seq_len, N_KV_HEADS, HEAD_DIM),
              ik.reshape(n_seq, seq_len, IDX_DIM), cstate[:, CONV_PAD - (CONV_K - 1):], pstate[:, POOL_PAD - POOL_HIST:])
    return outs, states


def kernel(x_prompt, x_sample, cache_k, cache_v, cache_idx_k, state_conv, state_pool, norm_g, w_in, conv_w,
           conv_b, pool_w, pool_scale, lift_a, lift_b, lift_c, w_out, final_norm_g):
    batch, seq, _ = x_prompt.shape
    dec_batch, dec_seq, _ = x_sample.shape
    depth = w_in.shape[0]
    past_len = cache_k.shape[2]
    topk_prompt = min(MAX_TOPK, seq // 4)
    topk_sample = min(MAX_TOPK, (past_len + dec_seq) // 4)

    tm_p, tm_s = 256, 256
    tabs_p = _rope_tables(jnp.arange(seq, dtype=jnp.int32))
    tabs_s = _rope_tables(past_len + (jnp.arange(tm_s, dtype=jnp.int32) % dec_seq))
    final_g = final_norm_g.reshape(1, D_MODEL)

    hp = x_prompt.reshape(batch * seq, D_MODEL)
    hs = x_sample.reshape(dec_batch * dec_seq, D_MODEL)
    zero_hist = (jnp.zeros((batch, CONV_PAD, CONV_W), F32), jnp.zeros((batch, POOL_PAD, POOL_W), F32))
    p_states, s_states = [], []
    for l in range(depth):
        wl = w_in[l]
        lw = {
            "g": norm_g[l].reshape(1, D_MODEL),
            "w_a": jnp.concatenate([wl[:, O_Q:O_GA], jnp.zeros((D_MODEL, A_GA - A_IKW - IDX_DIM - N_IDX_HEADS), F32),
                                    wl[:, O_GA:O_U]], axis=1).astype(BF16),
            "w_b": jnp.concatenate([wl[:, O_U:O_MA], wl[:, O_MB:O_END]], axis=1).astype(BF16),
            "w_ma": wl[:, O_MA:O_MB].astype(BF16),
            "conv_w": conv_w[l], "conv_b": conv_b[l].reshape(1, CONV_W),
            "pool_w": pool_w[l].astype(BF16), "pool_scale": pool_scale[l].reshape(1, POOL_W),
            "lift_a": lift_a[l].astype(BF16), "lift_b": lift_b[l].astype(BF16), "lift_c": lift_c[l].astype(BF16),
            "w_out": w_out[l].astype(BF16),
        }
        final = l == depth - 1
        outs_p, st_p = _layer(hp, tabs_p, lw, zero_hist, None, n_seq=batch, seq_len=seq, pos0=0,
                              topk=topk_prompt, tm_tok=tm_p, tm_seq=tm_p, tq=Q_TILE_PROMPT, final=final,
                              final_g=final_g)
        cik = cache_idx_k[l].astype(BF16)
        cik2 = jnp.concatenate([cik, jnp.zeros_like(cik), jnp.zeros_like(cik), cik], axis=-1)
        cache = (cache_k, cache_v, cik2, l)
        hist_s = (jnp.pad(state_conv[l], ((0, 0), (CONV_PAD - (CONV_K - 1), 0), (0, 0))),
                  jnp.pad(state_pool[l], ((0, 0), (POOL_PAD - POOL_HIST, 0), (0, 0))))
        outs_s, st_s = _layer(hs, tabs_s, lw, hist_s, cache, n_seq=dec_batch, seq_len=dec_seq, pos0=past_len,
                              topk=topk_sample, tm_tok=tm_s, tm_seq=tm_s, tq=Q_TILE_SAMPLE, final=final,
                              final_g=final_g)
        hp, hs = outs_p, outs_s
        p_states.append(st_p)
        s_states.append(st_s)

    stack = lambda sts, i: jnp.stack([st[i] for st in sts])
    return (hp.reshape(batch, seq, D_MODEL), hs.reshape(dec_batch, dec_seq, D_MODEL),
            stack(p_states, 0), stack(p_states, 1), stack(p_states, 2), stack(p_states, 3), stack(p_states, 4),
            stack(s_states, 0), stack(s_states, 1), stack(s_states, 2), stack(s_states, 3), stack(s_states, 4))
```

```python
import functools

import jax
import jax.numpy as jnp
from jax import lax
from jax.experimental import pallas as pl
from jax.experimental.pallas import tpu as pltpu

D_MODEL = 2048
CHUNK = 64
N_HEADS = 8
HEAD_DIM = 128
N_KV_HEADS = 2
GROUP = N_HEADS // N_KV_HEADS
ATTN_W = N_HEADS * HEAD_DIM
KV_W = N_KV_HEADS * HEAD_DIM
N_IDX_HEADS = 16
IDX_DIM = 64
IDX_Q_W = N_IDX_HEADS * IDX_DIM
MAX_TOPK = 256
CONV_W = 512
CONV_K = 3
POOL_W = 512
POOL_WINDOWS = (2, 4, 8, 16)
POOL_GROUP = 128
POOL_HIST = 15
ROPE_THETA = 10000.0
EPS = 1e-6

LANES = 128
CONV_PAD = 8
POOL_PAD = 16
KEY_BLOCK = 512
PACK_ROWS = 512
Q_TILE_PROMPT = 256
Q_TILE_SAMPLE = 128
LOGIT_LANES = 1024
NEG_BIAS = -1e30
COUNT_CHAINS = 4
BISECT_UNROLL = 4
BISECT_UNCHECKED = 3
LOG2E = 1.4426950408889634
F32_LOWEST = -3.0e38
VMEM_LIMIT = 56 * 1024 * 1024

F32 = jnp.float32
BF16 = jnp.bfloat16

_SIZES = (ATTN_W, KV_W, KV_W, IDX_Q_W, IDX_DIM, N_IDX_HEADS, ATTN_W,
          CONV_W, CONV_W, CONV_W, CONV_W, POOL_W, POOL_W, D_MODEL, D_MODEL, D_MODEL)
_OFFS = [0]
for _s in _SIZES:
    _OFFS.append(_OFFS[-1] + _s)
(O_Q, O_K, O_V, O_IQ, O_IK, O_IW, O_GA, O_U, O_BG, O_CG, O_GB, O_PIN, O_GC, O_MA, O_MB, O_MC, O_END) = _OFFS

A_Q, A_K, A_V, A_IQ, A_IKW, A_GA, A_END = 0, 1024, 1280, 1536, 2560, 2688, 3712
B_U, B_BG, B_CG, B_GB, B_PIN, B_GC, B_MB, B_MC, B_END = 0, 512, 1024, 1536, 2048, 2560, 3072, 5120, 7168


def _dot(a, b):
    return jnp.dot(a, b, preferred_element_type=F32)


def _dot_nt(a, b):
    return lax.dot_general(a, b, (((1,), (1,)), ((), ())), preferred_element_type=F32)


def _rms_h(x, g):
    h = x * lax.rsqrt(jnp.mean(x * x, axis=-1, keepdims=True) + EPS) * g
    return h.astype(BF16)


def _silu(x):
    return x * jax.nn.sigmoid(x)


def _const_spec(shape):
    nd = len(shape)
    return pl.BlockSpec(shape, lambda *_: (0,) * nd, pipeline_mode=pl.Buffered(1))


def _proj_a_kernel(x_ref, g_ref, w_ref, c128_ref, s128_ref, c64_ref, s64_ref,
                   q_ref, k_ref, v_ref, kb_ref, vb_ref, iq_ref, ik_ref, ik2_ref, iw_ref, ga_ref):
    h = _rms_h(x_ref[...], g_ref[...])
    c128 = c128_ref[...]
    s128 = s128_ref[...]
    c64 = c64_ref[...]
    s64 = s64_ref[...]
    lane = lax.broadcasted_iota(jnp.int32, (1, LANES), 1)
    first_half64 = (lane % IDX_DIM) < (IDX_DIM // 2)

    def rope128(y):
        return y * c128 + pltpu.roll(y, HEAD_DIM // 2, 1) * s128

    def rope64(y):
        partner = jnp.where(first_half64, pltpu.roll(y, LANES - IDX_DIM // 2, 1),
                            pltpu.roll(y, IDX_DIM // 2, 1))
        return y * c64 + partner * s64

    y = _dot(h, w_ref[:, A_Q:A_K])
    for hd in range(N_HEADS):
        sl = slice(hd * HEAD_DIM, (hd + 1) * HEAD_DIM)
        q_ref[:, sl] = (rope128(y[:, sl]) * (HEAD_DIM ** -0.5 * LOG2E)).astype(BF16)

    y = _dot(h, w_ref[:, A_K:A_IQ])
    for hd in range(N_KV_HEADS):
        sl = slice(hd * HEAD_DIM, (hd + 1) * HEAD_DIM)
        kr = rope128(y[:, sl])
        rows_hd = pl.ds(hd, y.shape[0], stride=N_KV_HEADS)
        k_ref[rows_hd, :] = kr
        kb_ref[:, sl] = kr.astype(BF16)
        v_ref[rows_hd, :] = y[:, KV_W + hd * HEAD_DIM:KV_W + (hd + 1) * HEAD_DIM]
    vb_ref[...] = y[:, KV_W:].astype(BF16)

    y = _dot(h, w_ref[:, A_IQ:A_IKW])
    for c in range(IDX_Q_W // LANES):
        sl = slice(c * LANES, (c + 1) * LANES)
        iq_ref[:, sl] = rope64(y[:, sl]).astype(BF16)

    y = _dot(h, w_ref[:, A_IKW:A_GA])
    ikr = rope64(y)
    ik_ref[...] = ikr[:, :IDX_DIM]
    ikz = jnp.where(lane < IDX_DIM, ikr, 0.0)
    ik2_ref[:, :LANES] = ikz.astype(BF16)
    ik2_ref[:, LANES:] = pltpu.roll(ikz, IDX_DIM, 1).astype(BF16)
    iw_ref[...] = y[:, IDX_DIM:IDX_DIM + N_IDX_HEADS] * ((IDX_DIM ** -0.5) * (N_IDX_HEADS ** -0.5))

    ga_ref[...] = _silu(_dot(h, w_ref[:, A_GA:A_END])).astype(BF16)


def _proj_a(x, g, w_a, tabs, tm):
    m = x.shape[0]
    c128, s128, c64, s64 = tabs
    n_pt = c128.shape[0] // tm
    row = lambda w: pl.BlockSpec((tm, w), lambda i: (i, 0))
    tab = pl.BlockSpec((tm, LANES), lambda i: (i % n_pt, 0))
    out_shape = (
        jax.ShapeDtypeStruct((m, ATTN_W), BF16),
        jax.ShapeDtypeStruct((m * N_KV_HEADS, HEAD_DIM), F32),
        jax.ShapeDtypeStruct((m * N_KV_HEADS, HEAD_DIM), F32),
        jax.ShapeDtypeStruct((m, KV_W), BF16),
        jax.ShapeDtypeStruct((m, KV_W), BF16),
        jax.ShapeDtypeStruct((m, IDX_Q_W), BF16),
        jax.ShapeDtypeStruct((m, IDX_DIM), F32),
        jax.ShapeDtypeStruct((m, 2 * LANES), BF16),
        jax.ShapeDtypeStruct((m, N_IDX_HEADS), F32),
        jax.ShapeDtypeStruct((m, ATTN_W), BF16),
    )
    kv_heads = pl.BlockSpec((tm * N_KV_HEADS, HEAD_DIM), lambda i: (i, 0))
    out_specs = (row(ATTN_W), kv_heads, kv_heads, row(KV_W), row(KV_W), row(IDX_Q_W),
                 row(IDX_DIM), row(2 * LANES), row(N_IDX_HEADS), row(ATTN_W))
    return pl.pallas_call(
        _proj_a_kernel,
        grid=(m // tm,),
        in_specs=[row(D_MODEL), _const_spec((1, D_MODEL)), _const_spec((D_MODEL, A_END)),
                  tab, tab, tab, tab],
        out_specs=out_specs,
        out_shape=out_shape,
        compiler_params=pltpu.CompilerParams(dimension_semantics=("arbitrary",),
                                             vmem_limit_bytes=VMEM_LIMIT),
        name="proj_a",
    )(x, g, w_a, c128, s128, c64, s64)


def _paired_blocks(n_blocks, trip):
    def pair(t, carry):
        trip(2 * t, 2)
        return carry

    lax.fori_loop(0, n_blocks // 2, pair, 0)

    @pl.when(n_blocks % 2 == 1)
    def _():
        trip(n_blocks - 1, 1)


def _attend_kernel(q_ref, iq_ref, iwt_ref, kb_ref, vb_ref, ik2_ref, o_ref,
                   st_ref, sbuf_ref, state_ref, acc_ref, stat_ref,
                   *, tq, n_pack, tq_real, q0, topk):
    kb_sz = KEY_BLOCK
    tqe = tq // n_pack
    i = pl.program_id(1)
    qpos0 = q0 + i * tq_real
    n_keys = ((qpos0 + tq_real - 1) // CHUNK + 1) * CHUNK
    nkb = (n_keys + kb_sz - 1) // kb_sz

    lane_q = lax.broadcasted_iota(jnp.int32, (1, tq), 1)
    lane_e = lane_q % tqe
    n_adm_i = ((qpos0 + lane_e) // CHUNK + 1) * CHUNK
    iw = iwt_ref[0]

    def query_rows(ref, cols):
        if n_pack == 1:
            return ref[0, :, cols]
        zero = jnp.zeros((tqe, cols.stop - cols.start), ref.dtype)
        return jnp.concatenate(
            [jnp.concatenate([ref[e, :, cols] if c == e else zero for c in range(n_pack)], axis=1)
             for e in range(n_pack)], axis=0)

    def key_rows(ref, r0, cols):
        if n_pack == 1:
            return ref[0, pl.ds(r0, kb_sz), cols]
        return jnp.concatenate([ref[e, pl.ds(r0, kb_sz), cols] for e in range(n_pack)], axis=1)

    state_ref[0:1, :] = jnp.full((1, tq), jnp.inf, F32)
    state_ref[1:2, :] = jnp.full((1, tq), -jnp.inf, F32)

    def score_trip(kb0, n_blk):
        mn, mx = state_ref[0:1, :], state_ref[1:2, :]
        for kb in [kb0 + d for d in range(n_blk)]:
            r0 = pl.multiple_of(kb * kb_sz, kb_sz)
            ik_even = key_rows(ik2_ref, r0, slice(0, LANES))
            ik_odd = key_rows(ik2_ref, r0, slice(LANES, 2 * LANES))
            acc = jnp.zeros((kb_sz, tq), F32)
            for p in range(N_IDX_HEADS // 2):
                iq_pair = query_rows(iq_ref, slice(p * LANES, (p + 1) * LANES))
                d0 = _dot_nt(ik_even, iq_pair)
                d1 = _dot_nt(ik_odd, iq_pair)
                acc = acc + jnp.maximum(d0, 0.0) * iw[2 * p:2 * p + 1]
                acc = acc + jnp.maximum(d1, 0.0) * iw[2 * p + 1:2 * p + 2]
            key = r0 + lax.broadcasted_iota(jnp.int32, (kb_sz, 1), 0)
            masked = jnp.where(key < n_adm_i, acc, -jnp.inf)
            st_ref[pl.ds(r0, kb_sz), :] = masked
            mx = jnp.maximum(mx, jnp.max(masked, axis=0, keepdims=True))
            mn = jnp.minimum(mn, jnp.min(acc, axis=0, keepdims=True))
        state_ref[0:1, :] = mn
        state_ref[1:2, :] = mx

    _paired_blocks(nkb, score_trip)
    mn, mx = state_ref[0:1, :], state_ref[1:2, :]

    n_adm = n_adm_i.astype(F32)
    topk_f = float(topk)
    active0 = jnp.logical_and(n_adm > topk_f, lane_e < tq_real).astype(F32)
    state_ref[0:1, :] = mn
    state_ref[1:2, :] = mx + jnp.maximum(jnp.abs(mx), 1e-30)
    state_ref[2:3, :] = jnp.full((1, tq), F32_LOWEST, F32)
    state_ref[3:4, :] = active0
    state_ref[4:5, :] = n_adm
    state_ref[5:6, :] = jnp.zeros((1, tq), F32)
    state_ref[6:7, :] = jnp.zeros((1, tq), F32)

    def count_ge(mid):
        def cnt_blk(kb, acc):
            r0 = pl.multiple_of(kb * kb_sz, kb_sz)
            ge = (st_ref[pl.ds(r0, kb_sz), :] >= mid).astype(F32)
            return acc + jnp.sum(ge.reshape(COUNT_CHAINS, kb_sz // (8 * COUNT_CHAINS), 8, tq), axis=1)

        acc = lax.fori_loop(0, nkb, cnt_blk, jnp.zeros((COUNT_CHAINS, 8, tq), F32))
        return jnp.sum(jnp.sum(acc, axis=0), axis=0, keepdims=True)

    def bisect_step(st):
        lo, hi, thr, active, clo, tie, pend = st
        mid = 0.5 * lo + 0.5 * hi
        stuck = jnp.logical_or(mid <= lo, mid >= hi)
        cnt = count_ge(mid)
        ge_k = cnt >= topk_f
        exact = cnt == topk_f
        moving = jnp.logical_and(active, jnp.logical_not(stuck))
        ended = jnp.logical_and(active, stuck)
        thr = jnp.where(ended, lo, jnp.where(jnp.logical_and(moving, exact), mid, thr))
        tie = jnp.logical_or(tie, jnp.logical_and(ended, clo > topk_f))
        up = jnp.logical_and(moving, ge_k)
        down = jnp.logical_and(moving, jnp.logical_not(ge_k))
        near = jnp.logical_and(moving, cnt == topk_f - 1.0)
        done = jnp.logical_or(exact, near)
        return (jnp.where(up, mid, lo), jnp.where(down, mid, hi), thr,
                jnp.logical_and(moving, jnp.logical_not(done)), jnp.where(up, cnt, clo), tie,
                jnp.logical_or(pend, near))

    def bisect_steps():
        st = (state_ref[0:1, :], state_ref[1:2, :], state_ref[2:3, :], state_ref[3:4, :] > 0.0,
              state_ref[4:5, :], state_ref[5:6, :] > 0.0, state_ref[6:7, :] > 0.0)
        for _ in range(BISECT_UNROLL):
            st = bisect_step(st)
        active_f = st[3].astype(F32)
        state_ref[0:1, :] = st[0]
        state_ref[1:2, :] = st[1]
        state_ref[2:3, :] = st[2]
        state_ref[3:4, :] = active_f
        state_ref[4:5, :] = st[4]
        state_ref[5:6, :] = st[5].astype(F32)
        state_ref[6:7, :] = st[6].astype(F32)
        return active_f

    def unchecked(i, carry):
        bisect_steps()
        return carry

    lax.fori_loop(0, BISECT_UNCHECKED, unchecked, 0)
    lax.while_loop(lambda c: jnp.logical_and(c[0] > 0.0, c[1] < 128),
                   lambda c: (jnp.max(bisect_steps()), c[1] + 1),
                   (jnp.max(state_ref[3:4, :]), jnp.int32(0)))

    @pl.when(jnp.max(state_ref[6:7, :]) > 0.0)
    def _():
        pend = state_ref[6:7, :] > 0.0
        hi = state_ref[1:2, :]

        def below_blk(kb, acc):
            r0 = pl.multiple_of(kb * kb_sz, kb_sz)
            blk = st_ref[pl.ds(r0, kb_sz), :]
            below = jnp.where(blk < hi, blk, -jnp.inf)
            return jnp.maximum(acc, jnp.max(below.reshape(COUNT_CHAINS, kb_sz // (8 * COUNT_CHAINS), 8, tq), axis=1))

        top = lax.fori_loop(0, nkb, below_blk, jnp.full((COUNT_CHAINS, 8, tq), -jnp.inf, F32))
        thr_p = jnp.where(pend, jnp.max(jnp.max(top, axis=0), axis=0, keepdims=True), state_ref[2:3, :])
        state_ref[2:3, :] = thr_p
        dup = jnp.logical_and(pend, count_ge(thr_p) > topk_f)
        state_ref[5:6, :] = jnp.maximum(state_ref[5:6, :], dup.astype(F32))

    thr = state_ref[2:3, :]

    @pl.when(jnp.max(state_ref[5:6, :]) > 0.0)
    def _():
        tie_row = state_ref[5:6, :] > 0.0

        def gt_blk(kb, acc):
            r0 = pl.multiple_of(kb * kb_sz, kb_sz)
            return acc + jnp.sum((st_ref[pl.ds(r0, kb_sz), :] > thr).astype(F32), axis=0, keepdims=True)

        need = topk_f - lax.fori_loop(0, nkb, gt_blk, jnp.zeros((1, tq), F32))
        tri = (lax.broadcasted_iota(jnp.int32, (kb_sz, kb_sz), 0)
               >= lax.broadcasted_iota(jnp.int32, (kb_sz, kb_sz), 1)).astype(BF16)

        def fix_blk(kb, run):
            r0 = pl.multiple_of(kb * kb_sz, kb_sz)
            blk = st_ref[pl.ds(r0, kb_sz), :]
            eq = jnp.logical_and(blk == thr, tie_row)
            eq_f = eq.astype(F32)
            rank = _dot(tri, eq_f.astype(BF16)) - eq_f + run
            st_ref[pl.ds(r0, kb_sz), :] = jnp.where(jnp.logical_and(eq, rank >= need), -jnp.inf, blk)
            return run + jnp.sum(eq_f, axis=0, keepdims=True)

        lax.fori_loop(0, nkb, fix_blk, jnp.zeros((1, tq), F32))

    red = lambda a, op: op(a.reshape(2, kb_sz // 16, 8, a.shape[-1]), axis=1)
    gw = GROUP * tq
    heads_per_trip = LOGIT_LANES // gw
    entry_of_lane = (lax.broadcasted_iota(jnp.int32, (1, gw), 1) % tq) // tqe
    for j0 in range(0, N_KV_HEADS, heads_per_trip):
        trip_heads = range(j0, j0 + heads_per_trip)
        qs = {j: jnp.concatenate(
            [query_rows(q_ref, slice((GROUP * j + g) * HEAD_DIM, (GROUP * j + g + 1) * HEAD_DIM))
             for g in range(GROUP)], axis=0) for j in trip_heads}

        def qk_trip(kb0, n_blk):
            m8 = stat_ref[...]
            for kb in [kb0 + d for d in range(n_blk)]:
                r0 = pl.multiple_of(kb * kb_sz, kb_sz)
                bias = jnp.where(st_ref[pl.ds(r0, kb_sz), :] >= thr, 0.0, NEG_BIAS)
                parts = []
                for j in trip_heads:
                    s = _dot_nt(key_rows(kb_ref, r0, slice(j * HEAD_DIM, (j + 1) * HEAD_DIM)), qs[j])
                    for g in range(GROUP):
                        sg = s[:, g * tq:(g + 1) * tq] + bias
                        c0 = (j - j0) * gw + g * tq
                        sbuf_ref[kb, :, c0:c0 + tq] = sg
                        parts.append(red(sg, jnp.max))
                m8 = jnp.maximum(m8, jnp.concatenate(parts, axis=-1))
            stat_ref[...] = m8

        stat_ref[...] = jnp.full(stat_ref.shape, NEG_BIAS, F32)
        _paired_blocks(nkb, qk_trip)
        m = jnp.max(jnp.max(stat_ref[...], axis=0), axis=0, keepdims=True)
        stat_ref[...] = jnp.zeros(stat_ref.shape, F32)
        acc_ref[...] = jnp.zeros(acc_ref.shape, F32)

        def pv_trip(kb0, n_blk):
            sums = []
            for j in trip_heads:
                cols = slice((j - j0) * gw, (j - j0 + 1) * gw)
                pv, psum = 0.0, 0.0
                for kb in [kb0 + d for d in range(n_blk)]:
                    r0 = pl.multiple_of(kb * kb_sz, kb_sz)
                    p = jnp.exp2(sbuf_ref[kb, :, cols] - m[:, cols])
                    p16 = p.astype(BF16)
                    t = None
                    for e in range(n_pack):
                        te = lax.dot_general(vb_ref[e, pl.ds(r0, kb_sz), j * HEAD_DIM:(j + 1) * HEAD_DIM], p16,
                                             (((0,), (0,)), ((), ())), preferred_element_type=F32)
                        t = te if t is None else jnp.where(entry_of_lane == e, te, t)
                    pv = pv + t
                    psum = psum + red(p, jnp.sum)
                acc_ref[:, cols] += pv
                sums.append(psum)
            stat_ref[...] += jnp.concatenate(sums, axis=-1)

        _paired_blocks(nkb, pv_trip)
        o = acc_ref[...] / jnp.sum(jnp.sum(stat_ref[...], axis=0), axis=0, keepdims=True)
        for j in trip_heads:
            for g in range(GROUP):
                hd = GROUP * j + g
                c0 = (j - j0) * gw + g * tq
                o_t = o[:, c0:c0 + tq].T.astype(o_ref.dtype)
                for e in range(n_pack):
                    o_ref[e, :, hd * HEAD_DIM:(hd + 1) * HEAD_DIM] = o_t[e * tqe:(e + 1) * tqe]


def _attend(q, iq, iwt, kb, vb, ik2, *, tq, n_pack, tq_real, q0, topk):
    b, t_q, _ = q.shape
    l_keys = kb.shape[1]
    tqe = tq // n_pack
    assert b % n_pack == 0 and t_q % tqe == 0 and (n_pack == 1 or t_q == tqe) and l_keys % KEY_BLOCK == 0
    n_kb = l_keys // KEY_BLOCK
    qspec = pl.BlockSpec((n_pack, tqe, ATTN_W), lambda bi, i: (bi, i, 0))
    kspec = pl.BlockSpec((n_pack, l_keys, KV_W), lambda bi, i: (bi, 0, 0))
    return pl.pallas_call(
        functools.partial(_attend_kernel, tq=tq, n_pack=n_pack, tq_real=tq_real, q0=q0, topk=topk),
        grid=(b // n_pack, t_q // tqe),
        in_specs=[qspec, qspec, pl.BlockSpec((1, N_IDX_HEADS, tq), lambda bi, i: (bi, 0, i)),
                  kspec, kspec, kspec],
        out_specs=qspec,
        out_shape=jax.ShapeDtypeStruct((b, t_q, ATTN_W), F32),
        scratch_shapes=[
            pltpu.VMEM((l_keys, tq), F32),
            pltpu.VMEM((n_kb, KEY_BLOCK, LOGIT_LANES), F32),
            pltpu.VMEM((8, tq), F32),
            pltpu.VMEM((HEAD_DIM, LOGIT_LANES), F32),
            pltpu.VMEM((2, 8, LOGIT_LANES), F32),
        ],
        compiler_params=pltpu.CompilerParams(dimension_semantics=("arbitrary", "arbitrary"),
                                             vmem_limit_bytes=VMEM_LIMIT),
        name="attend",
    )(q, iq, iwt, kb, vb, ik2)


def _pack_cache_kernel(c_ref, o_ref, *, n_src):
    j = pl.program_id(1)

    @pl.when(j < n_src)
    def _():
        for hd in range(N_KV_HEADS):
            rows_hd = pl.ds(hd, PACK_ROWS, stride=N_KV_HEADS)
            o_ref[0, :, hd * HEAD_DIM:(hd + 1) * HEAD_DIM] = c_ref[0, 0, rows_hd, :].astype(BF16)

    @pl.when(j >= n_src)
    def _():
        o_ref[...] = jnp.zeros(o_ref.shape, BF16)


def _pack_cache(c, layer, l_pad):
    depth, b, past = c.shape[:3]
    n_src = past // PACK_ROWS
    c = c.reshape(depth, b, past * N_KV_HEADS, HEAD_DIM)
    return pl.pallas_call(
        functools.partial(_pack_cache_kernel, n_src=n_src),
        grid=(b, l_pad // PACK_ROWS),
        in_specs=[pl.BlockSpec((1, 1, PACK_ROWS * N_KV_HEADS, HEAD_DIM),
                               lambda bi, j: (layer, bi, jnp.minimum(j, n_src - 1), 0))],
        out_specs=pl.BlockSpec((1, PACK_ROWS, KV_W), lambda bi, j: (bi, j, 0)),
        out_shape=jax.ShapeDtypeStruct((b, l_pad, KV_W), BF16),
        compiler_params=pltpu.CompilerParams(dimension_semantics=("arbitrary", "arbitrary"),
                                             vmem_limit_bytes=VMEM_LIMIT),
        name="pack_cache",
    )(c)


def _proj_b_kernel(x_ref, g_ref, w_ref, convw_ref, convb_ref, poolw_ref, pscale_ref, liftb_ref, liftc_ref,
                   chist_ref, phist_ref, zbc_ref, cstate_ref, pstate_ref, cin_ext, pin_ext,
                   *, n_sub, ls, pos0):
    j = pl.program_id(1)

    @pl.when(j == 0)
    def _():
        cin_ext[:, 0:CONV_PAD, :] = chist_ref[...]
        pin_ext[:, 0:POOL_PAD, :] = phist_ref[...]

    @pl.when(j > 0)
    def _():
        cin_ext[:, 0:CONV_PAD, :] = cin_ext[:, ls:ls + CONV_PAD, :]
        pin_ext[:, 0:POOL_PAD, :] = pin_ext[:, ls:ls + POOL_PAD, :]

    h = _rms_h(x_ref[...], g_ref[...])
    rows = lambda a: jnp.concatenate(a, axis=0) if n_sub > 1 else a[0]

    y = _dot(h, w_ref[:, B_U:B_PIN])
    u = y[:, 0:CONV_W]
    b_gate = y[:, CONV_W:2 * CONV_W]
    c_gate = y[:, 2 * CONV_W:3 * CONV_W]
    gate_b = y[:, 3 * CONV_W:4 * CONV_W]
    cin = c_gate * u
    for s in range(n_sub):
        cin_ext[s, CONV_PAD:CONV_PAD + ls, :] = cin[s * ls:(s + 1) * ls]
    conv = (rows([cin_ext[s, CONV_PAD - 2:CONV_PAD - 2 + ls, :] for s in range(n_sub)]) * convw_ref[0:1, :]
            + rows([cin_ext[s, CONV_PAD - 1:CONV_PAD - 1 + ls, :] for s in range(n_sub)]) * convw_ref[1:2, :]
            + cin * convw_ref[2:3, :] + convb_ref[...])
    y_b = b_gate * conv * _silu(gate_b)
    zb = _dot(y_b.astype(BF16), liftb_ref[...])

    y = _dot(h, w_ref[:, B_PIN:B_MB])
    p_in = y[:, 0:POOL_W]
    gate_c = y[:, POOL_W:2 * POOL_W]
    for s in range(n_sub):
        pin_ext[s, POOL_PAD:POOL_PAD + ls, :] = p_in[s * ls:(s + 1) * ls]
    pos = rows([pos0 + j * ls + lax.broadcasted_iota(jnp.int32, (ls, 1), 0)] * n_sub)
    yc_parts = []
    for gi, win in enumerate(POOL_WINDOWS):
        sl = slice(gi * POOL_GROUP, (gi + 1) * POOL_GROUP)
        tot = p_in[:, sl]
        for back in range(1, win):
            tot = tot + rows([pin_ext[s, POOL_PAD - back:POOL_PAD - back + ls, sl] for s in range(n_sub)])
        cnt = jnp.minimum(win, pos + 1).astype(F32)
        d = tot / cnt - p_in[:, sl]
        mixed = _dot(d.astype(BF16), poolw_ref[gi])
        yc_parts.append(mixed * pscale_ref[:, sl] * _silu(gate_c[:, sl]))
    y_c = jnp.concatenate(yc_parts, axis=-1)
    zc = _dot(y_c.astype(BF16), liftc_ref[...])

    m_b = _dot(h, w_ref[:, B_MB:B_MC])
    m_c = _dot(h, w_ref[:, B_MC:B_END])
    zbc_ref[...] = jax.nn.sigmoid(m_b) * zb + jax.nn.sigmoid(m_c) * zc

    @pl.when(j == pl.num_programs(1) - 1)
    def _():
        cstate_ref[...] = cin_ext[:, ls:ls + CONV_PAD, :]
        pstate_ref[...] = pin_ext[:, ls:ls + POOL_PAD, :]


def _proj_b(x, g, w_b, conv_w, conv_b, pool_w, pool_scale, lift_b, lift_c, chist, phist, *, seq_len, tm, pos0):
    m = x.shape[0]
    n_seq = m // seq_len
    n_sub, ls = (1, tm) if tm <= seq_len else (tm // seq_len, seq_len)
    n_t = seq_len // ls
    out_shape = (
        jax.ShapeDtypeStruct((m, D_MODEL), F32),
        jax.ShapeDtypeStruct((n_seq, CONV_PAD, CONV_W), F32),
        jax.ShapeDtypeStruct((n_seq, POOL_PAD, POOL_W), F32),
    )
    hist = lambda r, w: pl.BlockSpec((n_sub, r, w), lambda s, j: (s, 0, 0))
    return pl.pallas_call(
        functools.partial(_proj_b_kernel, n_sub=n_sub, ls=ls, pos0=pos0),
        grid=(n_seq // n_sub, n_t),
        in_specs=[pl.BlockSpec((tm, D_MODEL), lambda s, j: (s * n_t + j, 0)),
                  _const_spec((1, D_MODEL)), _const_spec((D_MODEL, B_END)),
                  _const_spec((CONV_K, CONV_W)), _const_spec((1, CONV_W)),
                  _const_spec((len(POOL_WINDOWS), POOL_GROUP, POOL_GROUP)), _const_spec((1, POOL_W)),
                  _const_spec((CONV_W, D_MODEL)), _const_spec((POOL_W, D_MODEL)),
                  hist(CONV_PAD, CONV_W), hist(POOL_PAD, POOL_W)],
        out_specs=(pl.BlockSpec((tm, D_MODEL), lambda s, j: (s * n_t + j, 0)),
                   hist(CONV_PAD, CONV_W), hist(POOL_PAD, POOL_W)),
        out_shape=out_shape,
        scratch_shapes=[pltpu.VMEM((n_sub, ls + CONV_PAD, CONV_W), F32),
                        pltpu.VMEM((n_sub, ls + POOL_PAD, POOL_W), F32)],
        compiler_params=pltpu.CompilerParams(dimension_semantics=("arbitrary", "arbitrary"),
                                             vmem_limit_bytes=VMEM_LIMIT),
        name="proj_b",
    )(x, g, w_b, conv_w, conv_b, pool_w, pool_scale, lift_b, lift_c, chist, phist)


def _merge_kernel(x_ref, g_ref, wma_ref, attn_ref, ga_ref, lifta_ref, zbc_ref, wout_ref, fg_ref,
                  out_ref, *, final):
    x = x_ref[...]
    h = _rms_h(x, g_ref[...])
    y_a = attn_ref[...] * ga_ref[...].astype(F32)
    z = jax.nn.sigmoid(_dot(h, wma_ref[...])) * _dot(y_a.astype(BF16), lifta_ref[...]) + zbc_ref[...]
    out = x + _dot(z.astype(BF16), wout_ref[...])
    if final:
        out = out * lax.rsqrt(jnp.mean(out * out, axis=-1, keepdims=True) + EPS) * fg_ref[...]
    out_ref[...] = out


def _merge(x, g, w_ma, attn, ga, lift_a, zbc, w_out, final_g, *, tm, final):
    m = x.shape[0]
    row = lambda w: pl.BlockSpec((tm, w), lambda i: (i, 0))
    return pl.pallas_call(
        functools.partial(_merge_kernel, final=final),
        grid=(m // tm,),
        in_specs=[row(D_MODEL), _const_spec((1, D_MODEL)), _const_spec((D_MODEL, D_MODEL)),
                  row(ATTN_W), row(ATTN_W), _const_spec((ATTN_W, D_MODEL)), row(D_MODEL),
                  _const_spec((D_MODEL, D_MODEL)), _const_spec((1, D_MODEL))],
        out_specs=row(D_MODEL),
        out_shape=jax.ShapeDtypeStruct((m, D_MODEL), F32),
        compiler_params=pltpu.CompilerParams(dimension_semantics=("arbitrary",),
                                             vmem_limit_bytes=VMEM_LIMIT),
        name="merge",
    )(x, g, w_ma, attn, ga, lift_a, zbc, w_out, final_g)


def _rope_tables(pos):
    def tab(half, reps):
        inv = ROPE_THETA ** (-jnp.arange(half, dtype=F32) / half)
        ang = pos.astype(F32)[:, None] * inv[None, :]
        cos, sin = jnp.cos(ang), jnp.sin(ang)
        return (jnp.tile(jnp.concatenate([cos, cos], axis=-1), (1, reps)),
                jnp.tile(jnp.concatenate([-sin, sin], axis=-1), (1, reps)))
    c128, s128 = tab(HEAD_DIM // 2, 1)
    c64, s64 = tab(IDX_DIM // 2, LANES // IDX_DIM)
    return c128, s128, c64, s64


def _layer(x, tabs, lw, hist, cache, *, n_seq, seq_len, pos0, topk, tm_tok, tm_seq, tq, final, final_g):
    m = n_seq * seq_len
    q, k, v, kb, vb, iq, ik, ik2, iw, ga = _proj_a(x, lw["g"], lw["w_a"], tabs, tm_tok)

    def seq(a):
        return a.reshape(n_seq, seq_len, a.shape[-1])

    iwt = jnp.swapaxes(seq(iw), 1, 2)
    q3, iq3, kb3, vb3, ik23 = seq(q), seq(iq), seq(kb), seq(vb), seq(ik2)
    if cache is not None:
        ck, cv, cik2, layer = cache
        past = ck.shape[2]
        l_pad = past + seq_len + (-(past + seq_len) % PACK_ROWS)
        kb3 = lax.dynamic_update_slice(_pack_cache(ck, layer, l_pad), kb3, (0, past, 0))
        vb3 = lax.dynamic_update_slice(_pack_cache(cv, layer, l_pad), vb3, (0, past, 0))
        ik23 = jnp.concatenate([cik2, ik23, jnp.zeros((n_seq, l_pad - past - seq_len, 2 * LANES), BF16)], axis=1)
    n_pack = tq // seq_len if seq_len < tq else 1
    assert tq == n_pack * min(seq_len, tq) and n_seq % n_pack == 0
    if n_pack > 1:
        iwt = jnp.swapaxes(iwt.reshape(n_seq // n_pack, n_pack, N_IDX_HEADS, seq_len), 1, 2)
        iwt = iwt.reshape(n_seq // n_pack, N_IDX_HEADS, tq)
    attn = _attend(q3, iq3, iwt, kb3, vb3, ik23, tq=tq, n_pack=n_pack, tq_real=min(seq_len, tq), q0=pos0, topk=topk)
    attn = attn.reshape(m, ATTN_W)

    zbc, cstate, pstate = _proj_b(x, lw["g"], lw["w_b"], lw["conv_w"], lw["conv_b"], lw["pool_w"],
                                  lw["pool_scale"], lw["lift_b"], lw["lift_c"], hist[0], hist[1],
                                  seq_len=seq_len, tm=tm_seq, pos0=pos0)
    outs = _merge(x, lw["g"], lw["w_ma"], attn, ga, lw["lift_a"], zbc, lw["w_out"], final_g,
                  tm=tm_tok, final=final)
    states = (k.reshape(n_seq, seq_len, N_KV_HEADS, HEAD_DIM), v.reshape(n_seq, seq_len, N_KV_HEADS, HEAD_DIM),
              ik.reshape(n_seq, seq_len, IDX_DIM), cstate[:, CONV_PAD - (CONV_K - 1):], pstate[:, POOL_PAD - POOL_HIST:])
    return outs, states


def kernel(x_prompt, x_sample, cache_k, cache_v, cache_idx_k, state_conv, state_pool, norm_g, w_in, conv_w,
           conv_b, pool_w, pool_scale, lift_a, lift_b, lift_c, w_out, final_norm_g):
    batch, seq, _ = x_prompt.shape
    dec_batch, dec_seq, _ = x_sample.shape
    depth = w_in.shape[0]
    past_len = cache_k.shape[2]
    topk_prompt = min(MAX_TOPK, seq // 4)
    topk_sample = min(MAX_TOPK, (past_len + dec_seq) // 4)

    tm_p, tm_s = 256, 256
    tabs_p = _rope_tables(jnp.arange(seq, dtype=jnp.int32))
    tabs_s = _rope_tables(past_len + (jnp.arange(tm_s, dtype=jnp.int32) % dec_seq))
    final_g = final_norm_g.reshape(1, D_MODEL)

    hp = x_prompt.reshape(batch * seq, D_MODEL)
    hs = x_sample.reshape(dec_batch * dec_seq, D_MODEL)
    zero_hist = (jnp.zeros((batch, CONV_PAD, CONV_W), F32), jnp.zeros((batch, POOL_PAD, POOL_W), F32))
    p_states, s_states = [], []
    for l in range(depth):
        wl = w_in[l]
        lw = {
            "g": norm_g[l].reshape(1, D_MODEL),
            "w_a": jnp.concatenate([wl[:, O_Q:O_GA], jnp.zeros((D_MODEL, A_GA - A_IKW - IDX_DIM - N_IDX_HEADS), F32),
                                    wl[:, O_GA:O_U]], axis=1).astype(BF16),
            "w_b": jnp.concatenate([wl[:, O_U:O_MA], wl[:, O_MB:O_END]], axis=1).astype(BF16),
            "w_ma": wl[:, O_MA:O_MB].astype(BF16),
            "conv_w": conv_w[l], "conv_b": conv_b[l].reshape(1, CONV_W),
            "pool_w": pool_w[l].astype(BF16), "pool_scale": pool_scale[l].reshape(1, POOL_W),
            "lift_a": lift_a[l].astype(BF16), "lift_b": lift_b[l].astype(BF16), "lift_c": lift_c[l].astype(BF16),
            "w_out": w_out[l].astype(BF16),
        }
        final = l == depth - 1
        outs_p, st_p = _layer(hp, tabs_p, lw, zero_hist, None, n_seq=batch, seq_len=seq, pos0=0,
                              topk=topk_prompt, tm_tok=tm_p, tm_seq=tm_p, tq=Q_TILE_PROMPT, final=final,
                              final_g=final_g)
        cik = cache_idx_k[l].astype(BF16)
        cik2 = jnp.concatenate([cik, jnp.zeros_like(cik), jnp.zeros_like(cik), cik], axis=-1)
        cache = (cache_k, cache_v, cik2, l)
        hist_s = (jnp.pad(state_conv[l], ((0, 0), (CONV_PAD - (CONV_K - 1), 0), (0, 0))),
                  jnp.pad(state_pool[l], ((0, 0), (POOL_PAD - POOL_HIST, 0), (0, 0))))
        outs_s, st_s = _layer(hs, tabs_s, lw, hist_s, cache, n_seq=dec_batch, seq_len=dec_seq, pos0=past_len,
                              topk=topk_sample, tm_tok=tm_s, tm_seq=tm_s, tq=Q_TILE_SAMPLE, final=final,
                              final_g=final_g)
        hp, hs = outs_p, outs_s
        p_states.append(st_p)
        s_states.append(st_s)

    stack = lambda sts, i: jnp.stack([st[i] for st in sts])
    return (hp.reshape(batch, seq, D_MODEL), hs.reshape(dec_batch, dec_seq, D_MODEL),
            stack(p_states, 0), stack(p_states, 1), stack(p_states, 2), stack(p_states, 3), stack(p_states, 4),
            stack(s_states, 0), stack(s_states, 1), stack(s_states, 2), stack(s_states, 3), stack(s_states, 4))
```

```python
import functools

import jax
import jax.numpy as jnp
from jax import lax
from jax.experimental import pallas as pl
from jax.experimental.pallas import tpu as pltpu

D_MODEL = 2048
CHUNK = 64
N_HEADS = 8
HEAD_DIM = 128
N_KV_HEADS = 2
GROUP = N_HEADS // N_KV_HEADS
ATTN_W = N_HEADS * HEAD_DIM
KV_W = N_KV_HEADS * HEAD_DIM
N_IDX_HEADS = 16
IDX_DIM = 64
IDX_Q_W = N_IDX_HEADS * IDX_DIM
MAX_TOPK = 256
CONV_W = 512
CONV_K = 3
POOL_W = 512
POOL_WINDOWS = (2, 4, 8, 16)
POOL_GROUP = 128
POOL_HIST = 15
ROPE_THETA = 10000.0
EPS = 1e-6

LANES = 128
CONV_PAD = 8
POOL_PAD = 16
KEY_BLOCK = 512
PREP_ROWS = 256
Q_TILE_PROMPT = 256
Q_TILE_SAMPLE = 128
LOGIT_LANES = 1024
NEG_BIAS = -1e30
COUNT_CHAINS = 4
BISECT_UNROLL = 4
BISECT_UNCHECKED = 3
LOG2E = 1.4426950408889634
F32_LOWEST = -3.0e38
VMEM_LIMIT = 56 * 1024 * 1024

F32 = jnp.float32
BF16 = jnp.bfloat16

_SIZES = (ATTN_W, KV_W, KV_W, IDX_Q_W, IDX_DIM, N_IDX_HEADS, ATTN_W,
          CONV_W, CONV_W, CONV_W, CONV_W, POOL_W, POOL_W, D_MODEL, D_MODEL, D_MODEL)
_OFFS = [0]
for _s in _SIZES:
    _OFFS.append(_OFFS[-1] + _s)
(O_Q, O_K, O_V, O_IQ, O_IK, O_IW, O_GA, O_U, O_BG, O_CG, O_GB, O_PIN, O_GC, O_MA, O_MB, O_MC, O_END) = _OFFS

A_Q, A_K, A_V, A_IQ, A_IKW, A_GA, A_END = 0, 1024, 1280, 1536, 2560, 2688, 3712
B_U, B_BG, B_CG, B_GB, B_PIN, B_GC, B_MB, B_MC, B_END = 0, 512, 1024, 1536, 2048, 2560, 3072, 5120, 7168


def _dot(a, b):
    return jnp.dot(a, b, preferred_element_type=F32)


def _dot_nt(a, b):
    return lax.dot_general(a, b, (((1,), (1,)), ((), ())), preferred_element_type=F32)


def _rms_h(x, g):
    h = x * lax.rsqrt(jnp.mean(x * x, axis=-1, keepdims=True) + EPS) * g
    return h.astype(BF16)


def _silu(x):
    return x * jax.nn.sigmoid(x)


def _const_spec(shape):
    nd = len(shape)
    return pl.BlockSpec(shape, lambda *_: (0,) * nd, pipeline_mode=pl.Buffered(1))


def _prep_weights_kernel(w_ref, wa_ref, wb_ref, wma_ref):
    cols = lambda a, b: w_ref[0, :, a:b].astype(BF16)
    ikw_end = A_IKW + IDX_DIM + N_IDX_HEADS
    wa_ref[:, A_Q:ikw_end] = cols(O_Q, O_GA)
    wa_ref[:, ikw_end:A_GA] = jnp.zeros((wa_ref.shape[0], A_GA - ikw_end), BF16)
    wa_ref[:, A_GA:A_END] = cols(O_GA, O_U)
    wb_ref[:, B_U:B_MB] = cols(O_U, O_MA)
    wb_ref[:, B_MB:B_END] = cols(O_MB, O_END)
    wma_ref[...] = cols(O_MA, O_MB)


def _prep_weights(w_in, layer):
    d, in_w = w_in.shape[1:]
    rows = PREP_ROWS
    slab = lambda w: pl.BlockSpec((rows, w), lambda i: (i, 0))
    return pl.pallas_call(
        _prep_weights_kernel,
        grid=(d // rows,),
        in_specs=[pl.BlockSpec((1, rows, in_w), lambda i: (layer, i, 0))],
        out_specs=(slab(A_END), slab(B_END), slab(D_MODEL)),
        out_shape=(jax.ShapeDtypeStruct((d, A_END), BF16), jax.ShapeDtypeStruct((d, B_END), BF16),
                   jax.ShapeDtypeStruct((d, D_MODEL), BF16)),
        compiler_params=pltpu.CompilerParams(dimension_semantics=("arbitrary",), vmem_limit_bytes=VMEM_LIMIT),
        name="prep_weights",
    )(w_in)


def _proj_a_kernel(x_ref, g_ref, w_ref, c128_ref, s128_ref, c64_ref, s64_ref,
                   q_ref, k_ref, v_ref, kb_ref, vb_ref, iq_ref, ik_ref, ik2_ref, iw_ref, ga_ref):
    h = _rms_h(x_ref[...], g_ref[...])
    c128 = c128_ref[...]
    s128 = s128_ref[...]
    c64 = c64_ref[...]
    s64 = s64_ref[...]
    lane = lax.broadcasted_iota(jnp.int32, (1, LANES), 1)
    first_half64 = (lane % IDX_DIM) < (IDX_DIM // 2)

    def rope128(y):
        return y * c128 + pltpu.roll(y, HEAD_DIM // 2, 1) * s128

    def rope64(y):
        partner = jnp.where(first_half64, pltpu.roll(y, LANES - IDX_DIM // 2, 1),
                            pltpu.roll(y, IDX_DIM // 2, 1))
        return y * c64 + partner * s64

    y = _dot(h, w_ref[:, A_Q:A_K])
    for hd in range(N_HEADS):
        sl = slice(hd * HEAD_DIM, (hd + 1) * HEAD_DIM)
        q_ref[:, sl] = (rope128(y[:, sl]) * (HEAD_DIM ** -0.5 * LOG2E)).astype(BF16)

    y = _dot(h, w_ref[:, A_K:A_IQ])
    for hd in range(N_KV_HEADS):
        sl = slice(hd * HEAD_DIM, (hd + 1) * HEAD_DIM)
        kr = rope128(y[:, sl])
        rows_hd = pl.ds(hd, y.shape[0], stride=N_KV_HEADS)
        k_ref[rows_hd, :] = kr
        kb_ref[:, sl] = kr.astype(BF16)
        v_ref[rows_hd, :] = y[:, KV_W + hd * HEAD_DIM:KV_W + (hd + 1) * HEAD_DIM]
    vb_ref[...] = y[:, KV_W:].astype(BF16)

    y = _dot(h, w_ref[:, A_IQ:A_IKW])
    for c in range(IDX_Q_W // LANES):
        sl = slice(c * LANES, (c + 1) * LANES)
        iq_ref[:, sl] = rope64(y[:, sl]).astype(BF16)

    y = _dot(h, w_ref[:, A_IKW:A_GA])
    ikr = rope64(y)
    ik_ref[...] = ikr[:, :IDX_DIM]
    ikz = jnp.where(lane < IDX_DIM, ikr, 0.0)
    ik2_ref[:, :LANES] = ikz.astype(BF16)
    ik2_ref[:, LANES:] = pltpu.roll(ikz, IDX_DIM, 1).astype(BF16)
    iw_ref[...] = y[:, IDX_DIM:IDX_DIM + N_IDX_HEADS] * ((IDX_DIM ** -0.5) * (N_IDX_HEADS ** -0.5))

    ga_ref[...] = _silu(_dot(h, w_ref[:, A_GA:A_END])).astype(BF16)


def _proj_a(x, g, w_a, tabs, tm):
    m = x.shape[0]
    c128, s128, c64, s64 = tabs
    n_pt = c128.shape[0] // tm
    row = lambda w: pl.BlockSpec((tm, w), lambda i: (i, 0))
    tab = pl.BlockSpec((tm, LANES), lambda i: (i % n_pt, 0))
    out_shape = (
        jax.ShapeDtypeStruct((m, ATTN_W), BF16),
        jax.ShapeDtypeStruct((m * N_KV_HEADS, HEAD_DIM), F32),
        jax.ShapeDtypeStruct((m * N_KV_HEADS, HEAD_DIM), F32),
        jax.ShapeDtypeStruct((m, KV_W), BF16),
        jax.ShapeDtypeStruct((m, KV_W), BF16),
        jax.ShapeDtypeStruct((m, IDX_Q_W), BF16),
        jax.ShapeDtypeStruct((m, IDX_DIM), F32),
        jax.ShapeDtypeStruct((m, 2 * LANES), BF16),
        jax.ShapeDtypeStruct((m, N_IDX_HEADS), F32),
        jax.ShapeDtypeStruct((m, ATTN_W), BF16),
    )
    kv_heads = pl.BlockSpec((tm * N_KV_HEADS, HEAD_DIM), lambda i: (i, 0))
    out_specs = (row(ATTN_W), kv_heads, kv_heads, row(KV_W), row(KV_W), row(IDX_Q_W),
                 row(IDX_DIM), row(2 * LANES), row(N_IDX_HEADS), row(ATTN_W))
    return pl.pallas_call(
        _proj_a_kernel,
        grid=(m // tm,),
        in_specs=[row(D_MODEL), _const_spec((1, D_MODEL)), _const_spec((D_MODEL, A_END)),
                  tab, tab, tab, tab],
        out_specs=out_specs,
        out_shape=out_shape,
        compiler_params=pltpu.CompilerParams(dimension_semantics=("arbitrary",),
                                             vmem_limit_bytes=VMEM_LIMIT),
        name="proj_a",
    )(x, g, w_a, c128, s128, c64, s64)


def _paired_blocks(n_blocks, trip):
    def pair(t, carry):
        trip(2 * t, 2)
        return carry

    lax.fori_loop(0, n_blocks // 2, pair, 0)

    @pl.when(n_blocks % 2 == 1)
    def _():
        trip(n_blocks - 1, 1)


def _attend_kernel(q_ref, iq_ref, iwt_ref, kb_ref, vb_ref, ik2_ref, o_ref,
                   st_ref, sbuf_ref, state_ref, acc_ref, stat_ref,
                   *, tq, n_pack, tq_real, q0, topk):
    kb_sz = KEY_BLOCK
    tqe = tq // n_pack
    i = pl.program_id(1)
    qpos0 = q0 + i * tq_real
    n_keys = ((qpos0 + tq_real - 1) // CHUNK + 1) * CHUNK
    nkb = (n_keys + kb_sz - 1) // kb_sz

    lane_q = lax.broadcasted_iota(jnp.int32, (1, tq), 1)
    lane_e = lane_q % tqe
    n_adm_i = ((qpos0 + lane_e) // CHUNK + 1) * CHUNK
    iw = iwt_ref[0]

    def query_rows(ref, cols):
        if n_pack == 1:
            return ref[0, :, cols]
        zero = jnp.zeros((tqe, cols.stop - cols.start), ref.dtype)
        return jnp.concatenate(
            [jnp.concatenate([ref[e, :, cols] if c == e else zero for c in range(n_pack)], axis=1)
             for e in range(n_pack)], axis=0)

    def key_rows(ref, r0, cols):
        if n_pack == 1:
            return ref[0, pl.ds(r0, kb_sz), cols]
        return jnp.concatenate([ref[e, pl.ds(r0, kb_sz), cols] for e in range(n_pack)], axis=1)

    state_ref[0:1, :] = jnp.full((1, tq), jnp.inf, F32)
    state_ref[1:2, :] = jnp.full((1, tq), -jnp.inf, F32)

    def score_trip(kb0, n_blk):
        mn, mx = state_ref[0:1, :], state_ref[1:2, :]
        for kb in [kb0 + d for d in range(n_blk)]:
            r0 = pl.multiple_of(kb * kb_sz, kb_sz)
            ik_even = key_rows(ik2_ref, r0, slice(0, LANES))
            ik_odd = key_rows(ik2_ref, r0, slice(LANES, 2 * LANES))
            acc = jnp.zeros((kb_sz, tq), F32)
            for p in range(N_IDX_HEADS // 2):
                iq_pair = query_rows(iq_ref, slice(p * LANES, (p + 1) * LANES))
                d0 = _dot_nt(ik_even, iq_pair)
                d1 = _dot_nt(ik_odd, iq_pair)
                acc = acc + jnp.maximum(d0, 0.0) * iw[2 * p:2 * p + 1]
                acc = acc + jnp.maximum(d1, 0.0) * iw[2 * p + 1:2 * p + 2]
            key = r0 + lax.broadcasted_iota(jnp.int32, (kb_sz, 1), 0)
            masked = jnp.where(key < n_adm_i, acc, -jnp.inf)
            st_ref[pl.ds(r0, kb_sz), :] = masked
            mx = jnp.maximum(mx, jnp.max(masked, axis=0, keepdims=True))
            mn = jnp.minimum(mn, jnp.min(acc, axis=0, keepdims=True))
        state_ref[0:1, :] = mn
        state_ref[1:2, :] = mx

    _paired_blocks(nkb, score_trip)
    mn, mx = state_ref[0:1, :], state_ref[1:2, :]

    n_adm = n_adm_i.astype(F32)
    topk_f = float(topk)
    active0 = jnp.logical_and(n_adm > topk_f, lane_e < tq_real).astype(F32)
    state_ref[0:1, :] = mn
    state_ref[1:2, :] = mx + jnp.maximum(jnp.abs(mx), 1e-30)
    state_ref[2:3, :] = jnp.full((1, tq), F32_LOWEST, F32)
    state_ref[3:4, :] = active0
    state_ref[4:5, :] = n_adm
    state_ref[5:6, :] = jnp.zeros((1, tq), F32)
    state_ref[6:7, :] = jnp.zeros((1, tq), F32)

    def count_ge(mid):
        def cnt_blk(kb, acc):
            r0 = pl.multiple_of(kb * kb_sz, kb_sz)
            ge = (st_ref[pl.ds(r0, kb_sz), :] >= mid).astype(F32)
            return acc + jnp.sum(ge.reshape(COUNT_CHAINS, kb_sz // (8 * COUNT_CHAINS), 8, tq), axis=1)

        acc = lax.fori_loop(0, nkb, cnt_blk, jnp.zeros((COUNT_CHAINS, 8, tq), F32))
        return jnp.sum(jnp.sum(acc, axis=0), axis=0, keepdims=True)

    def bisect_step(st):
        lo, hi, thr, active, clo, tie, pend = st
        mid = 0.5 * lo + 0.5 * hi
        stuck = jnp.logical_or(mid <= lo, mid >= hi)
        cnt = count_ge(mid)
        ge_k = cnt >= topk_f
        exact = cnt == topk_f
        moving = jnp.logical_and(active, jnp.logical_not(stuck))
        ended = jnp.logical_and(active, stuck)
        thr = jnp.where(ended, lo, jnp.where(jnp.logical_and(moving, exact), mid, thr))
        tie = jnp.logical_or(tie, jnp.logical_and(ended, clo > topk_f))
        up = jnp.logical_and(moving, ge_k)
        down = jnp.logical_and(moving, jnp.logical_not(ge_k))
        near = jnp.logical_and(moving, cnt == topk_f - 1.0)
        done = jnp.logical_or(exact, near)
        return (jnp.where(up, mid, lo), jnp.where(down, mid, hi), thr,
                jnp.logical_and(moving, jnp.logical_not(done)), jnp.where(up, cnt, clo), tie,
                jnp.logical_or(pend, near))

    def bisect_steps():
        st = (state_ref[0:1, :], state_ref[1:2, :], state_ref[2:3, :], state_ref[3:4, :] > 0.0,
              state_ref[4:5, :], state_ref[5:6, :] > 0.0, state_ref[6:7, :] > 0.0)
        for _ in range(BISECT_UNROLL):
            st = bisect_step(st)
        active_f = st[3].astype(F32)
        state_ref[0:1, :] = st[0]
        state_ref[1:2, :] = st[1]
        state_ref[2:3, :] = st[2]
        state_ref[3:4, :] = active_f
        state_ref[4:5, :] = st[4]
        state_ref[5:6, :] = st[5].astype(F32)
        state_ref[6:7, :] = st[6].astype(F32)
        return active_f

    def unchecked(i, carry):
        bisect_steps()
        return carry

    lax.fori_loop(0, BISECT_UNCHECKED, unchecked, 0)
    lax.while_loop(lambda c: jnp.logical_and(c[0] > 0.0, c[1] < 128),
                   lambda c: (jnp.max(bisect_steps()), c[1] + 1),
                   (jnp.max(state_ref[3:4, :]), jnp.int32(0)))

    @pl.when(jnp.max(state_ref[6:7, :]) > 0.0)
    def _():
        pend = state_ref[6:7, :] > 0.0
        hi = state_ref[1:2, :]

        def below_blk(kb, acc):
            r0 = pl.multiple_of(kb * kb_sz, kb_sz)
            blk = st_ref[pl.ds(r0, kb_sz), :]
            below = jnp.where(blk < hi, blk, -jnp.inf)
            return jnp.maximum(acc, jnp.max(below.reshape(COUNT_CHAINS, kb_sz // (8 * COUNT_CHAINS), 8, tq), axis=1))

        top = lax.fori_loop(0, nkb, below_blk, jnp.full((COUNT_CHAINS, 8, tq), -jnp.inf, F32))
        thr_p = jnp.where(pend, jnp.max(jnp.max(top, axis=0), axis=0, keepdims=True), state_ref[2:3, :])
        state_ref[2:3, :] = thr_p
        dup = jnp.logical_and(pend, count_ge(thr_p) > topk_f)
        state_ref[5:6, :] = jnp.maximum(state_ref[5:6, :], dup.astype(F32))

    thr = state_ref[2:3, :]

    @pl.when(jnp.max(state_ref[5:6, :]) > 0.0)
    def _():
        tie_row = state_ref[5:6, :] > 0.0

        def gt_blk(kb, acc):
            r0 = pl.multiple_of(kb * kb_sz, kb_sz)
            return acc + jnp.sum((st_ref[pl.ds(r0, kb_sz), :] > thr).astype(F32), axis=0, keepdims=True)

        need = topk_f - lax.fori_loop(0, nkb, gt_blk, jnp.zeros((1, tq), F32))
        tri = (lax.broadcasted_iota(jnp.int32, (kb_sz, kb_sz), 0)
               >= lax.broadcasted_iota(jnp.int32, (kb_sz, kb_sz), 1)).astype(BF16)

        def fix_blk(kb, run):
            r0 = pl.multiple_of(kb * kb_sz, kb_sz)
            blk = st_ref[pl.ds(r0, kb_sz), :]
            eq = jnp.logical_and(blk == thr, tie_row)
            eq_f = eq.astype(F32)
            rank = _dot(tri, eq_f.astype(BF16)) - eq_f + run
            st_ref[pl.ds(r0, kb_sz), :] = jnp.where(jnp.logical_and(eq, rank >= need), -jnp.inf, blk)
            return run + jnp.sum(eq_f, axis=0, keepdims=True)

        lax.fori_loop(0, nkb, fix_blk, jnp.zeros((1, tq), F32))

    red = lambda a, op: op(a.reshape(2, kb_sz // 16, 8, a.shape[-1]), axis=1)
    gw = GROUP * tq
    heads_per_trip = LOGIT_LANES // gw
    entry_of_lane = (lax.broadcasted_iota(jnp.int32, (1, gw), 1) % tq) // tqe
    for j0 in range(0, N_KV_HEADS, heads_per_trip):
        trip_heads = range(j0, j0 + heads_per_trip)
        qs = {j: jnp.concatenate(
            [query_rows(q_ref, slice((GROUP * j + g) * HEAD_DIM, (GROUP * j + g + 1) * HEAD_DIM))
             for g in range(GROUP)], axis=0) for j in trip_heads}

        def qk_trip(kb0, n_blk):
            m8 = stat_ref[...]
            for kb in [kb0 + d for d in range(n_blk)]:
                r0 = pl.multiple_of(kb * kb_sz, kb_sz)
                bias = jnp.where(st_ref[pl.ds(r0, kb_sz), :] >= thr, 0.0, NEG_BIAS)
                parts = []
                for j in trip_heads:
                    s = _dot_nt(key_rows(kb_ref, r0, slice(j * HEAD_DIM, (j + 1) * HEAD_DIM)), qs[j])
                    for g in range(GROUP):
                        sg = s[:, g * tq:(g + 1) * tq] + bias
                        c0 = (j - j0) * gw + g * tq
                        sbuf_ref[kb, :, c0:c0 + tq] = sg
                        parts.append(red(sg, jnp.max))
                m8 = jnp.maximum(m8, jnp.concatenate(parts, axis=-1))
            stat_ref[...] = m8

        stat_ref[...] = jnp.full(stat_ref.shape, NEG_BIAS, F32)
        _paired_blocks(nkb, qk_trip)
        m = jnp.max(jnp.max(stat_ref[...], axis=0), axis=0, keepdims=True)
        stat_ref[...] = jnp.zeros(stat_ref.shape, F32)
        acc_ref[...] = jnp.zeros(acc_ref.shape, F32)

        def pv_trip(kb0, n_blk):
            sums = []
            for j in trip_heads:
                cols = slice((j - j0) * gw, (j - j0 + 1) * gw)
                pv, psum = 0.0, 0.0
                for kb in [kb0 + d for d in range(n_blk)]:
                    r0 = pl.multiple_of(kb * kb_sz, kb_sz)
                    p = jnp.exp2(sbuf_ref[kb, :, cols] - m[:, cols])
                    p16 = p.astype(BF16)
                    t = None
                    for e in range(n_pack):
                        te = lax.dot_general(vb_ref[e, pl.ds(r0, kb_sz), j * HEAD_DIM:(j + 1) * HEAD_DIM], p16,
                                             (((0,), (0,)), ((), ())), preferred_element_type=F32)
                        t = te if t is None else jnp.where(entry_of_lane == e, te, t)
                    pv = pv + t
                    psum = psum + red(p, jnp.sum)
                acc_ref[:, cols] += pv
                sums.append(psum)
            stat_ref[...] += jnp.concatenate(sums, axis=-1)

        _paired_blocks(nkb, pv_trip)
        o = acc_ref[...] / jnp.sum(jnp.sum(stat_ref[...], axis=0), axis=0, keepdims=True)
        for j in trip_heads:
            for g in range(GROUP):
                hd = GROUP * j + g
                c0 = (j - j0) * gw + g * tq
                o_t = o[:, c0:c0 + tq].T.astype(o_ref.dtype)
                for e in range(n_pack):
                    o_ref[e, :, hd * HEAD_DIM:(hd + 1) * HEAD_DIM] = o_t[e * tqe:(e + 1) * tqe]


def _attend(q, iq, iwt, kb, vb, ik2, *, tq, n_pack, tq_real, q0, topk):
    b, t_q, _ = q.shape
    l_keys = kb.shape[1]
    tqe = tq // n_pack
    assert b % n_pack == 0 and t_q % tqe == 0 and (n_pack == 1 or t_q == tqe) and l_keys % KEY_BLOCK == 0
    n_kb = l_keys // KEY_BLOCK
    qspec = pl.BlockSpec((n_pack, tqe, ATTN_W), lambda bi, i: (bi, i, 0))
    kspec = pl.BlockSpec((n_pack, l_keys, KV_W), lambda bi, i: (bi, 0, 0))
    return pl.pallas_call(
        functools.partial(_attend_kernel, tq=tq, n_pack=n_pack, tq_real=tq_real, q0=q0, topk=topk),
        grid=(b // n_pack, t_q // tqe),
        in_specs=[qspec, qspec, pl.BlockSpec((1, N_IDX_HEADS, tq), lambda bi, i: (bi, 0, i)),
                  kspec, kspec, kspec],
        out_specs=qspec,
        out_shape=jax.ShapeDtypeStruct((b, t_q, ATTN_W), F32),
        scratch_shapes=[
            pltpu.VMEM((l_keys, tq), F32),
            pltpu.VMEM((n_kb, KEY_BLOCK, LOGIT_LANES), F32),
            pltpu.VMEM((8, tq), F32),
            pltpu.VMEM((HEAD_DIM, LOGIT_LANES), F32),
            pltpu.VMEM((2, 8, LOGIT_LANES), F32),
        ],
        compiler_params=pltpu.CompilerParams(dimension_semantics=("arbitrary", "arbitrary"),
                                             vmem_limit_bytes=VMEM_LIMIT),
        name="attend",
    )(q, iq, iwt, kb, vb, ik2)


def _pack_cache_kernel(c_ref, o_ref, *, past):
    for hd in range(N_KV_HEADS):
        rows_hd = pl.ds(hd, past, stride=N_KV_HEADS)
        o_ref[0, 0:past, hd * HEAD_DIM:(hd + 1) * HEAD_DIM] = c_ref[0, 0, rows_hd, :].astype(BF16)
    o_ref[0, past:, :] = jnp.zeros((o_ref.shape[1] - past, KV_W), BF16)


def _pack_idx_cache_kernel(c_ref, e_ref, o_ref, *, past):
    o_ref[0, 0:past, :] = _dot(c_ref[0, 0].astype(BF16), e_ref[...]).astype(BF16)
    o_ref[0, past:, :] = jnp.zeros((o_ref.shape[1] - past, 2 * LANES), BF16)


def _pack_cache(c, layer, l_pad):
    depth, b, past = c.shape[:3]
    c = c.reshape(depth, b, past * N_KV_HEADS, HEAD_DIM)
    return pl.pallas_call(
        functools.partial(_pack_cache_kernel, past=past),
        grid=(b,),
        in_specs=[pl.BlockSpec((1, 1, past * N_KV_HEADS, HEAD_DIM), lambda bi: (layer, bi, 0, 0))],
        out_specs=pl.BlockSpec((1, l_pad, KV_W), lambda bi: (bi, 0, 0)),
        out_shape=jax.ShapeDtypeStruct((b, l_pad, KV_W), BF16),
        compiler_params=pltpu.CompilerParams(dimension_semantics=("arbitrary",), vmem_limit_bytes=VMEM_LIMIT),
        name="pack_cache",
    )(c)


def _pack_idx_cache(c, layer, l_pad):
    _, b, past, _ = c.shape
    eye = jnp.eye(IDX_DIM, dtype=BF16)
    zero = jnp.zeros((IDX_DIM, IDX_DIM), BF16)
    select = jnp.concatenate([eye, zero, zero, eye], axis=1)
    return pl.pallas_call(
        functools.partial(_pack_idx_cache_kernel, past=past),
        grid=(b,),
        in_specs=[pl.BlockSpec((1, 1, past, IDX_DIM), lambda bi: (layer, bi, 0, 0)),
                  pl.BlockSpec((IDX_DIM, 2 * LANES), lambda bi: (0, 0))],
        out_specs=pl.BlockSpec((1, l_pad, 2 * LANES), lambda bi: (bi, 0, 0)),
        out_shape=jax.ShapeDtypeStruct((b, l_pad, 2 * LANES), BF16),
        compiler_params=pltpu.CompilerParams(dimension_semantics=("arbitrary",), vmem_limit_bytes=VMEM_LIMIT),
        name="pack_idx_cache",
    )(c, select)


def _proj_b_kernel(x_ref, g_ref, w_ref, convw_ref, convb_ref, poolw_ref, pscale_ref, liftb_ref, liftc_ref,
                   chist_ref, phist_ref, zbc_ref, cstate_ref, pstate_ref, cin_ext, pin_ext,
                   *, n_sub, ls, pos0):
    j = pl.program_id(1)

    @pl.when(j == 0)
    def _():
        cin_ext[:, 0:CONV_PAD, :] = chist_ref[...]
        pin_ext[:, 0:POOL_PAD, :] = phist_ref[...]

    @pl.when(j > 0)
    def _():
        cin_ext[:, 0:CONV_PAD, :] = cin_ext[:, ls:ls + CONV_PAD, :]
        pin_ext[:, 0:POOL_PAD, :] = pin_ext[:, ls:ls + POOL_PAD, :]

    h = _rms_h(x_ref[...], g_ref[...])
    rows = lambda a: jnp.concatenate(a, axis=0) if n_sub > 1 else a[0]

    y = _dot(h, w_ref[:, B_U:B_PIN])
    u = y[:, 0:CONV_W]
    b_gate = y[:, CONV_W:2 * CONV_W]
    c_gate = y[:, 2 * CONV_W:3 * CONV_W]
    gate_b = y[:, 3 * CONV_W:4 * CONV_W]
    cin = c_gate * u
    for s in range(n_sub):
        cin_ext[s, CONV_PAD:CONV_PAD + ls, :] = cin[s * ls:(s + 1) * ls]
    conv = (rows([cin_ext[s, CONV_PAD - 2:CONV_PAD - 2 + ls, :] for s in range(n_sub)]) * convw_ref[0:1, :]
            + rows([cin_ext[s, CONV_PAD - 1:CONV_PAD - 1 + ls, :] for s in range(n_sub)]) * convw_ref[1:2, :]
            + cin * convw_ref[2:3, :] + convb_ref[...])
    y_b = b_gate * conv * _silu(gate_b)
    zb = _dot(y_b.astype(BF16), liftb_ref[...])

    y = _dot(h, w_ref[:, B_PIN:B_MB])
    p_in = y[:, 0:POOL_W]
    gate_c = y[:, POOL_W:2 * POOL_W]
    for s in range(n_sub):
        pin_ext[s, POOL_PAD:POOL_PAD + ls, :] = p_in[s * ls:(s + 1) * ls]
    pos = rows([pos0 + j * ls + lax.broadcasted_iota(jnp.int32, (ls, 1), 0)] * n_sub)
    yc_parts = []
    for gi, win in enumerate(POOL_WINDOWS):
        sl = slice(gi * POOL_GROUP, (gi + 1) * POOL_GROUP)
        tot = p_in[:, sl]
        for back in range(1, win):
            tot = tot + rows([pin_ext[s, POOL_PAD - back:POOL_PAD - back + ls, sl] for s in range(n_sub)])
        cnt = jnp.minimum(win, pos + 1).astype(F32)
        d = tot / cnt - p_in[:, sl]
        mixed = _dot(d.astype(BF16), poolw_ref[gi])
        yc_parts.append(mixed * pscale_ref[:, sl] * _silu(gate_c[:, sl]))
    y_c = jnp.concatenate(yc_parts, axis=-1)
    zc = _dot(y_c.astype(BF16), liftc_ref[...])

    m_b = _dot(h, w_ref[:, B_MB:B_MC])
    m_c = _dot(h, w_ref[:, B_MC:B_END])
    zbc_ref[...] = jax.nn.sigmoid(m_b) * zb + jax.nn.sigmoid(m_c) * zc

    @pl.when(j == pl.num_programs(1) - 1)
    def _():
        cstate_ref[...] = cin_ext[:, ls:ls + CONV_PAD, :]
        pstate_ref[...] = pin_ext[:, ls:ls + POOL_PAD, :]


def _proj_b(x, g, w_b, conv_w, conv_b, pool_w, pool_scale, lift_b, lift_c, chist, phist, *, seq_len, tm, pos0):
    m = x.shape[0]
    n_seq = m // seq_len
    n_sub, ls = (1, tm) if tm <= seq_len else (tm // seq_len, seq_len)
    n_t = seq_len // ls
    out_shape = (
        jax.ShapeDtypeStruct((m, D_MODEL), F32),
        jax.ShapeDtypeStruct((n_seq, CONV_PAD, CONV_W), F32),
        jax.ShapeDtypeStruct((n_seq, POOL_PAD, POOL_W), F32),
    )
    hist = lambda r, w: pl.BlockSpec((n_sub, r, w), lambda s, j: (s, 0, 0))
    return pl.pallas_call(
        functools.partial(_proj_b_kernel, n_sub=n_sub, ls=ls, pos0=pos0),
        grid=(n_seq // n_sub, n_t),
        in_specs=[pl.BlockSpec((tm, D_MODEL), lambda s, j: (s * n_t + j, 0)),
                  _const_spec((1, D_MODEL)), _const_spec((D_MODEL, B_END)),
                  _const_spec((CONV_K, CONV_W)), _const_spec((1, CONV_W)),
                  _const_spec((len(POOL_WINDOWS), POOL_GROUP, POOL_GROUP)), _const_spec((1, POOL_W)),
                  _const_spec((CONV_W, D_MODEL)), _const_spec((POOL_W, D_MODEL)),
                  hist(CONV_PAD, CONV_W), hist(POOL_PAD, POOL_W)],
        out_specs=(pl.BlockSpec((tm, D_MODEL), lambda s, j: (s * n_t + j, 0)),
                   hist(CONV_PAD, CONV_W), hist(POOL_PAD, POOL_W)),
        out_shape=out_shape,
        scratch_shapes=[pltpu.VMEM((n_sub, ls + CONV_PAD, CONV_W), F32),
                        pltpu.VMEM((n_sub, ls + POOL_PAD, POOL_W), F32)],
        compiler_params=pltpu.CompilerParams(dimension_semantics=("arbitrary", "arbitrary"),
                                             vmem_limit_bytes=VMEM_LIMIT),
        name="proj_b",
    )(x, g, w_b, conv_w, conv_b, pool_w, pool_scale, lift_b, lift_c, chist, phist)


def _merge_kernel(x_ref, g_ref, wma_ref, attn_ref, ga_ref, lifta_ref, zbc_ref, wout_ref, fg_ref,
                  out_ref, *, final):
    x = x_ref[...]
    h = _rms_h(x, g_ref[...])
    y_a = attn_ref[...] * ga_ref[...].astype(F32)
    z = jax.nn.sigmoid(_dot(h, wma_ref[...])) * _dot(y_a.astype(BF16), lifta_ref[...]) + zbc_ref[...]
    out = x + _dot(z.astype(BF16), wout_ref[...])
    if final:
        out = out * lax.rsqrt(jnp.mean(out * out, axis=-1, keepdims=True) + EPS) * fg_ref[...]
    out_ref[...] = out


def _merge(x, g, w_ma, attn, ga, lift_a, zbc, w_out, final_g, *, tm, final):
    m = x.shape[0]
    row = lambda w: pl.BlockSpec((tm, w), lambda i: (i, 0))
    return pl.pallas_call(
        functools.partial(_merge_kernel, final=final),
        grid=(m // tm,),
        in_specs=[row(D_MODEL), _const_spec((1, D_MODEL)), _const_spec((D_MODEL, D_MODEL)),
                  row(ATTN_W), row(ATTN_W), _const_spec((ATTN_W, D_MODEL)), row(D_MODEL),
                  _const_spec((D_MODEL, D_MODEL)), _const_spec((1, D_MODEL))],
        out_specs=row(D_MODEL),
        out_shape=jax.ShapeDtypeStruct((m, D_MODEL), F32),
        compiler_params=pltpu.CompilerParams(dimension_semantics=("arbitrary",),
                                             vmem_limit_bytes=VMEM_LIMIT),
        name="merge",
    )(x, g, w_ma, attn, ga, lift_a, zbc, w_out, final_g)


def _rope_tables(pos):
    def tab(half, reps):
        inv = ROPE_THETA ** (-jnp.arange(half, dtype=F32) / half)
        ang = pos.astype(F32)[:, None] * inv[None, :]
        cos, sin = jnp.cos(ang), jnp.sin(ang)
        return (jnp.tile(jnp.concatenate([cos, cos], axis=-1), (1, reps)),
                jnp.tile(jnp.concatenate([-sin, sin], axis=-1), (1, reps)))
    c128, s128 = tab(HEAD_DIM // 2, 1)
    c64, s64 = tab(IDX_DIM // 2, LANES // IDX_DIM)
    return c128, s128, c64, s64


def _layer(x, tabs, lw, hist, cache, *, n_seq, seq_len, pos0, topk, tm_tok, tm_seq, tq, final, final_g):
    m = n_seq * seq_len
    q, k, v, kb, vb, iq, ik, ik2, iw, ga = _proj_a(x, lw["g"], lw["w_a"], tabs, tm_tok)

    def seq(a):
        return a.reshape(n_seq, seq_len, a.shape[-1])

    iwt = jnp.swapaxes(seq(iw), 1, 2)
    q3, iq3, kb3, vb3, ik23 = seq(q), seq(iq), seq(kb), seq(vb), seq(ik2)
    if cache is not None:
        ck, cv, cik, layer = cache
        past = ck.shape[2]
        l_pad = past + seq_len + (-(past + seq_len) % KEY_BLOCK)
        kb3 = lax.dynamic_update_slice(_pack_cache(ck, layer, l_pad), kb3, (0, past, 0))
        vb3 = lax.dynamic_update_slice(_pack_cache(cv, layer, l_pad), vb3, (0, past, 0))
        ik23 = lax.dynamic_update_slice(_pack_idx_cache(cik, layer, l_pad), ik23, (0, past, 0))
    n_pack = tq // seq_len if seq_len < tq else 1
    assert tq == n_pack * min(seq_len, tq) and n_seq % n_pack == 0
    if n_pack > 1:
        iwt = jnp.swapaxes(iwt.reshape(n_seq // n_pack, n_pack, N_IDX_HEADS, seq_len), 1, 2)
        iwt = iwt.reshape(n_seq // n_pack, N_IDX_HEADS, tq)
    attn = _attend(q3, iq3, iwt, kb3, vb3, ik23, tq=tq, n_pack=n_pack, tq_real=min(seq_len, tq), q0=pos0, topk=topk)
    attn = attn.reshape(m, ATTN_W)

    zbc, cstate, pstate = _proj_b(x, lw["g"], lw["w_b"], lw["conv_w"], lw["conv_b"], lw["pool_w"],
                                  lw["pool_scale"], lw["lift_b"], lw["lift_c"], hist[0], hist[1],
                                  seq_len=seq_len, tm=tm_seq, pos0=pos0)
    outs = _merge(x, lw["g"], lw["w_ma"], attn, ga, lw["lift_a"], zbc, lw["w_out"], final_g,
                  tm=tm_tok, final=final)
    states = (k.reshape(n_seq, seq_len, N_KV_HEADS, HEAD_DIM), v.reshape(n_seq, seq_len, N_KV_HEADS, HEAD_DIM),
              ik.reshape(n_seq, seq_len, IDX_DIM), cstate[:, CONV_PAD - (CONV_K - 1):], pstate[:, POOL_PAD - POOL_HIST:])
    return outs, states


def kernel(x_prompt, x_sample, cache_k, cache_v, cache_idx_k, state_conv, state_pool, norm_g, w_in, conv_w,
           conv_b, pool_w, pool_scale, lift_a, lift_b, lift_c, w_out, final_norm_g):
    batch, seq, _ = x_prompt.shape
    dec_batch, dec_seq, _ = x_sample.shape
    depth = w_in.shape[0]
    past_len = cache_k.shape[2]
    topk_prompt = min(MAX_TOPK, seq // 4)
    topk_sample = min(MAX_TOPK, (past_len + dec_seq) // 4)

    tm_p, tm_s = 256, 256
    tabs_p = _rope_tables(jnp.arange(seq, dtype=jnp.int32))
    tabs_s = _rope_tables(past_len + (jnp.arange(tm_s, dtype=jnp.int32) % dec_seq))
    final_g = final_norm_g.reshape(1, D_MODEL)

    hp = x_prompt.reshape(batch * seq, D_MODEL)
    hs = x_sample.reshape(dec_batch * dec_seq, D_MODEL)
    zero_hist = (jnp.zeros((batch, CONV_PAD, CONV_W), F32), jnp.zeros((batch, POOL_PAD, POOL_W), F32))
    p_states, s_states = [], []
    for l in range(depth):
        w_a, w_b, w_ma = _prep_weights(w_in, l)
        lw = {
            "g": norm_g[l].reshape(1, D_MODEL),
            "w_a": w_a, "w_b": w_b, "w_ma": w_ma,
            "conv_w": conv_w[l], "conv_b": conv_b[l].reshape(1, CONV_W),
            "pool_w": pool_w[l].astype(BF16), "pool_scale": pool_scale[l].reshape(1, POOL_W),
            "lift_a": lift_a[l].astype(BF16), "lift_b": lift_b[l].astype(BF16), "lift_c": lift_c[l].astype(BF16),
            "w_out": w_out[l].astype(BF16),
        }
        final = l == depth - 1
        outs_p, st_p = _layer(hp, tabs_p, lw, zero_hist, None, n_seq=batch, seq_len=seq, pos0=0,
                              topk=topk_prompt, tm_tok=tm_p, tm_seq=tm_p, tq=Q_TILE_PROMPT, final=final,
                              final_g=final_g)
        cache = (cache_k, cache_v, cache_idx_k, l)
        hist_s = (jnp.pad(state_conv[l], ((0, 0), (CONV_PAD - (CONV_K - 1), 0), (0, 0))),
                  jnp.pad(state_pool[l], ((0, 0), (POOL_PAD - POOL_HIST, 0), (0, 0))))
        outs_s, st_s = _layer(hs, tabs_s, lw, hist_s, cache, n_seq=dec_batch, seq_len=dec_seq, pos0=past_len,
                              topk=topk_sample, tm_tok=tm_s, tm_seq=tm_s, tq=Q_TILE_SAMPLE, final=final,
                              final_g=final_g)
        hp, hs = outs_p, outs_s
        p_states.append(st_p)
        s_states.append(st_s)

    stack = lambda sts, i: jnp.stack([st[i] for st in sts])
    return (hp.reshape(batch, seq, D_MODEL), hs.reshape(dec_batch, dec_seq, D_MODEL),
            stack(p_states, 0), stack(p_states, 1), stack(p_states, 2), stack(p_states, 3), stack(p_states, 4),
            stack(s_states, 0), stack(s_states, 1), stack(s_states, 2), stack(s_states, 3), stack(s_states, 4))
```

```python
import functools

import jax
import jax.numpy as jnp
from jax import lax
from jax.experimental import pallas as pl
from jax.experimental.pallas import tpu as pltpu

D_MODEL = 2048
CHUNK = 64
N_HEADS = 8
HEAD_DIM = 128
N_KV_HEADS = 2
GROUP = N_HEADS // N_KV_HEADS
ATTN_W = N_HEADS * HEAD_DIM
KV_W = N_KV_HEADS * HEAD_DIM
N_IDX_HEADS = 16
IDX_DIM = 64
IDX_Q_W = N_IDX_HEADS * IDX_DIM
MAX_TOPK = 256
CONV_W = 512
CONV_K = 3
POOL_W = 512
POOL_WINDOWS = (2, 4, 8, 16)
POOL_GROUP = 128
POOL_HIST = 15
ROPE_THETA = 10000.0
EPS = 1e-6

LANES = 128
CONV_PAD = 8
POOL_PAD = 16
KEY_BLOCK = 512
PREP_ROWS = 256
Q_TILE_PROMPT = 256
Q_TILE_SAMPLE = 128
LOGIT_LANES = 1024
NEG_BIAS = -1e30
COUNT_CHAINS = 4
BISECT_UNROLL = 4
BISECT_UNCHECKED = 4
LOG2E = 1.4426950408889634
F32_LOWEST = -3.0e38
VMEM_LIMIT = 56 * 1024 * 1024

F32 = jnp.float32
BF16 = jnp.bfloat16

_SIZES = (ATTN_W, KV_W, KV_W, IDX_Q_W, IDX_DIM, N_IDX_HEADS, ATTN_W,
          CONV_W, CONV_W, CONV_W, CONV_W, POOL_W, POOL_W, D_MODEL, D_MODEL, D_MODEL)
_OFFS = [0]
for _s in _SIZES:
    _OFFS.append(_OFFS[-1] + _s)
(O_Q, O_K, O_V, O_IQ, O_IK, O_IW, O_GA, O_U, O_BG, O_CG, O_GB, O_PIN, O_GC, O_MA, O_MB, O_MC, O_END) = _OFFS

A_Q, A_K, A_V, A_IQ, A_IKW, A_GA, A_END = 0, 1024, 1280, 1536, 2560, 2688, 3712
B_U, B_BG, B_CG, B_GB, B_PIN, B_GC, B_MB, B_MC, B_END = 0, 512, 1024, 1536, 2048, 2560, 3072, 5120, 7168


def _dot(a, b):
    return jnp.dot(a, b, preferred_element_type=F32)


def _dot_nt(a, b):
    return lax.dot_general(a, b, (((1,), (1,)), ((), ())), preferred_element_type=F32)


def _rms_h(x, g):
    h = x * lax.rsqrt(jnp.mean(x * x, axis=-1, keepdims=True) + EPS) * g
    return h.astype(BF16)


def _silu(x):
    return x * jax.nn.sigmoid(x)


def _const_spec(shape):
    nd = len(shape)
    return pl.BlockSpec(shape, lambda *_: (0,) * nd, pipeline_mode=pl.Buffered(1))


def _prep_weights_kernel(w_ref, wa_ref, wb_ref, wma_ref):
    cols = lambda a, b: w_ref[0, :, a:b].astype(BF16)
    ikw_end = A_IKW + IDX_DIM + N_IDX_HEADS
    wa_ref[:, A_Q:ikw_end] = cols(O_Q, O_GA)
    wa_ref[:, ikw_end:A_GA] = jnp.zeros((wa_ref.shape[0], A_GA - ikw_end), BF16)
    wa_ref[:, A_GA:A_END] = cols(O_GA, O_U)
    wb_ref[:, B_U:B_MB] = cols(O_U, O_MA)
    wb_ref[:, B_MB:B_END] = cols(O_MB, O_END)
    wma_ref[...] = cols(O_MA, O_MB)


def _prep_weights(w_in, layer):
    d, in_w = w_in.shape[1:]
    rows = PREP_ROWS
    slab = lambda w: pl.BlockSpec((rows, w), lambda i: (i, 0))
    return pl.pallas_call(
        _prep_weights_kernel,
        grid=(d // rows,),
        in_specs=[pl.BlockSpec((1, rows, in_w), lambda i: (layer, i, 0))],
        out_specs=(slab(A_END), slab(B_END), slab(D_MODEL)),
        out_shape=(jax.ShapeDtypeStruct((d, A_END), BF16), jax.ShapeDtypeStruct((d, B_END), BF16),
                   jax.ShapeDtypeStruct((d, D_MODEL), BF16)),
        compiler_params=pltpu.CompilerParams(dimension_semantics=("arbitrary",), vmem_limit_bytes=VMEM_LIMIT),
        name="prep_weights",
    )(w_in)


def _proj_a_kernel(x_ref, g_ref, w_ref, c128_ref, s128_ref, c64_ref, s64_ref,
                   q_ref, k_ref, v_ref, kb_ref, vb_ref, iq_ref, ik_ref, ik2_ref, iw_ref, ga_ref):
    h = _rms_h(x_ref[...], g_ref[...])
    c128 = c128_ref[...]
    s128 = s128_ref[...]
    c64 = c64_ref[...]
    s64 = s64_ref[...]
    lane = lax.broadcasted_iota(jnp.int32, (1, LANES), 1)
    first_half64 = (lane % IDX_DIM) < (IDX_DIM // 2)

    def rope128(y):
        return y * c128 + pltpu.roll(y, HEAD_DIM // 2, 1) * s128

    def rope64(y):
        partner = jnp.where(first_half64, pltpu.roll(y, LANES - IDX_DIM // 2, 1),
                            pltpu.roll(y, IDX_DIM // 2, 1))
        return y * c64 + partner * s64

    y = _dot(h, w_ref[:, A_Q:A_K])
    for hd in range(N_HEADS):
        sl = slice(hd * HEAD_DIM, (hd + 1) * HEAD_DIM)
        q_ref[:, sl] = (rope128(y[:, sl]) * (HEAD_DIM ** -0.5 * LOG2E)).astype(BF16)

    y = _dot(h, w_ref[:, A_K:A_IQ])
    for hd in range(N_KV_HEADS):
        sl = slice(hd * HEAD_DIM, (hd + 1) * HEAD_DIM)
        kr = rope128(y[:, sl])
        rows_hd = pl.ds(hd, y.shape[0], stride=N_KV_HEADS)
        k_ref[rows_hd, :] = kr
        kb_ref[:, sl] = kr.astype(BF16)
        v_ref[rows_hd, :] = y[:, KV_W + hd * HEAD_DIM:KV_W + (hd + 1) * HEAD_DIM]
    vb_ref[...] = y[:, KV_W:].astype(BF16)

    y = _dot(h, w_ref[:, A_IQ:A_IKW])
    for c in range(IDX_Q_W // LANES):
        sl = slice(c * LANES, (c + 1) * LANES)
        iq_ref[:, sl] = rope64(y[:, sl]).astype(BF16)

    y = _dot(h, w_ref[:, A_IKW:A_GA])
    ikr = rope64(y)
    ik_ref[...] = ikr[:, :IDX_DIM]
    ikz = jnp.where(lane < IDX_DIM, ikr, 0.0)
    ik2_ref[:, :LANES] = ikz.astype(BF16)
    ik2_ref[:, LANES:] = pltpu.roll(ikz, IDX_DIM, 1).astype(BF16)
    iw_ref[...] = y[:, IDX_DIM:IDX_DIM + N_IDX_HEADS] * ((IDX_DIM ** -0.5) * (N_IDX_HEADS ** -0.5))

    ga_ref[...] = _silu(_dot(h, w_ref[:, A_GA:A_END])).astype(BF16)


def _proj_a(x, g, w_a, tabs, tm):
    m = x.shape[0]
    c128, s128, c64, s64 = tabs
    n_pt = c128.shape[0] // tm
    row = lambda w: pl.BlockSpec((tm, w), lambda i: (i, 0))
    tab = pl.BlockSpec((tm, LANES), lambda i: (i % n_pt, 0))
    out_shape = (
        jax.ShapeDtypeStruct((m, ATTN_W), BF16),
        jax.ShapeDtypeStruct((m * N_KV_HEADS, HEAD_DIM), F32),
        jax.ShapeDtypeStruct((m * N_KV_HEADS, HEAD_DIM), F32),
        jax.ShapeDtypeStruct((m, KV_W), BF16),
        jax.ShapeDtypeStruct((m, KV_W), BF16),
        jax.ShapeDtypeStruct((m, IDX_Q_W), BF16),
        jax.ShapeDtypeStruct((m, IDX_DIM), F32),
        jax.ShapeDtypeStruct((m, 2 * LANES), BF16),
        jax.ShapeDtypeStruct((m, N_IDX_HEADS), F32),
        jax.ShapeDtypeStruct((m, ATTN_W), BF16),
    )
    kv_heads = pl.BlockSpec((tm * N_KV_HEADS, HEAD_DIM), lambda i: (i, 0))
    out_specs = (row(ATTN_W), kv_heads, kv_heads, row(KV_W), row(KV_W), row(IDX_Q_W),
                 row(IDX_DIM), row(2 * LANES), row(N_IDX_HEADS), row(ATTN_W))
    return pl.pallas_call(
        _proj_a_kernel,
        grid=(m // tm,),
        in_specs=[row(D_MODEL), _const_spec((1, D_MODEL)), _const_spec((D_MODEL, A_END)),
                  tab, tab, tab, tab],
        out_specs=out_specs,
        out_shape=out_shape,
        compiler_params=pltpu.CompilerParams(dimension_semantics=("arbitrary",),
                                             vmem_limit_bytes=VMEM_LIMIT),
        name="proj_a",
    )(x, g, w_a, c128, s128, c64, s64)


def _paired_blocks(n_blocks, trip):
    def pair(t, carry):
        trip(2 * t, 2)
        return carry

    lax.fori_loop(0, n_blocks // 2, pair, 0)

    @pl.when(n_blocks % 2 == 1)
    def _():
        trip(n_blocks - 1, 1)


def _attend_kernel(q_ref, iq_ref, iwt_ref, kb_ref, vb_ref, ik2_ref, o_ref,
                   st_ref, sbuf_ref, state_ref, acc_ref, stat_ref,
                   *, tq, n_pack, tq_real, q0, topk):
    kb_sz = KEY_BLOCK
    tqe = tq // n_pack
    i = pl.program_id(1)
    qpos0 = q0 + i * tq_real
    n_keys = ((qpos0 + tq_real - 1) // CHUNK + 1) * CHUNK
    nkb = (n_keys + kb_sz - 1) // kb_sz

    lane_q = lax.broadcasted_iota(jnp.int32, (1, tq), 1)
    lane_e = lane_q % tqe
    n_adm_i = ((qpos0 + lane_e) // CHUNK + 1) * CHUNK
    iw = iwt_ref[0]

    def query_rows(ref, cols):
        if n_pack == 1:
            return ref[0, :, cols]
        zero = jnp.zeros((tqe, cols.stop - cols.start), ref.dtype)
        return jnp.concatenate(
            [jnp.concatenate([ref[e, :, cols] if c == e else zero for c in range(n_pack)], axis=1)
             for e in range(n_pack)], axis=0)

    def key_rows(ref, r0, cols):
        if n_pack == 1:
            return ref[0, pl.ds(r0, kb_sz), cols]
        return jnp.concatenate([ref[e, pl.ds(r0, kb_sz), cols] for e in range(n_pack)], axis=1)

    state_ref[0:1, :] = jnp.full((1, tq), jnp.inf, F32)
    state_ref[1:2, :] = jnp.full((1, tq), -jnp.inf, F32)

    def score_trip(kb0, n_blk):
        mn, mx = state_ref[0:1, :], state_ref[1:2, :]
        for kb in [kb0 + d for d in range(n_blk)]:
            r0 = pl.multiple_of(kb * kb_sz, kb_sz)
            ik_even = key_rows(ik2_ref, r0, slice(0, LANES))
            ik_odd = key_rows(ik2_ref, r0, slice(LANES, 2 * LANES))
            acc = jnp.zeros((kb_sz, tq), F32)
            for p in range(N_IDX_HEADS // 2):
                iq_pair = query_rows(iq_ref, slice(p * LANES, (p + 1) * LANES))
                d0 = _dot_nt(ik_even, iq_pair)
                d1 = _dot_nt(ik_odd, iq_pair)
                acc = acc + jnp.maximum(d0, 0.0) * iw[2 * p:2 * p + 1]
                acc = acc + jnp.maximum(d1, 0.0) * iw[2 * p + 1:2 * p + 2]
            key = r0 + lax.broadcasted_iota(jnp.int32, (kb_sz, 1), 0)
            masked = jnp.where(key < n_adm_i, acc, -jnp.inf)
            st_ref[pl.ds(r0, kb_sz), :] = masked
            mx = jnp.maximum(mx, jnp.max(masked, axis=0, keepdims=True))
            mn = jnp.minimum(mn, jnp.min(acc, axis=0, keepdims=True))
        state_ref[0:1, :] = mn
        state_ref[1:2, :] = mx

    _paired_blocks(nkb, score_trip)
    mn, mx = state_ref[0:1, :], state_ref[1:2, :]

    n_adm = n_adm_i.astype(F32)
    topk_f = float(topk)
    active0 = jnp.logical_and(n_adm > topk_f, lane_e < tq_real).astype(F32)
    state_ref[0:1, :] = mn
    state_ref[1:2, :] = mx + jnp.maximum(jnp.abs(mx), 1e-30)
    state_ref[2:3, :] = jnp.full((1, tq), F32_LOWEST, F32)
    state_ref[3:4, :] = active0
    state_ref[4:5, :] = n_adm
    state_ref[5:6, :] = jnp.zeros((1, tq), F32)
    state_ref[6:7, :] = jnp.zeros((1, tq), F32)

    def count_ge(mid):
        sub = kb_sz // COUNT_CHAINS

        def cnt_blk(kb, accs):
            r0 = pl.multiple_of(kb * kb_sz, kb_sz)
            out = []
            for c in range(COUNT_CHAINS):
                ge = (st_ref[pl.ds(r0 + c * sub, sub), :] >= mid).astype(F32)
                out.append(accs[c] + jnp.sum(ge.reshape(sub // 8, 8, tq), axis=0))
            return tuple(out)

        accs = lax.fori_loop(0, nkb, cnt_blk, tuple(jnp.zeros((8, tq), F32) for _ in range(COUNT_CHAINS)))
        return jnp.sum(sum(accs), axis=0, keepdims=True)

    def bisect_step(st):
        lo, hi, thr, active, clo, tie, pend = st
        mid = 0.5 * lo + 0.5 * hi
        stuck = jnp.logical_or(mid <= lo, mid >= hi)
        cnt = count_ge(mid)
        ge_k = cnt >= topk_f
        exact = cnt == topk_f
        moving = jnp.logical_and(active, jnp.logical_not(stuck))
        ended = jnp.logical_and(active, stuck)
        thr = jnp.where(ended, lo, jnp.where(jnp.logical_and(moving, exact), mid, thr))
        tie = jnp.logical_or(tie, jnp.logical_and(ended, clo > topk_f))
        up = jnp.logical_and(moving, ge_k)
        down = jnp.logical_and(moving, jnp.logical_not(ge_k))
        near = jnp.logical_and(moving, cnt == topk_f - 1.0)
        done = jnp.logical_or(exact, near)
        return (jnp.where(up, mid, lo), jnp.where(down, mid, hi), thr,
                jnp.logical_and(moving, jnp.logical_not(done)), jnp.where(up, cnt, clo), tie,
                jnp.logical_or(pend, near))

    def bisect_steps():
        st = (state_ref[0:1, :], state_ref[1:2, :], state_ref[2:3, :], state_ref[3:4, :] > 0.0,
              state_ref[4:5, :], state_ref[5:6, :] > 0.0, state_ref[6:7, :] > 0.0)
        for _ in range(BISECT_UNROLL):
            st = bisect_step(st)
        active_f = st[3].astype(F32)
        state_ref[0:1, :] = st[0]
        state_ref[1:2, :] = st[1]
        state_ref[2:3, :] = st[2]
        state_ref[3:4, :] = active_f
        state_ref[4:5, :] = st[4]
        state_ref[5:6, :] = st[5].astype(F32)
        state_ref[6:7, :] = st[6].astype(F32)
        return active_f

    def unchecked(i, carry):
        bisect_steps()
        return carry

    lax.fori_loop(0, BISECT_UNCHECKED, unchecked, 0)
    lax.while_loop(lambda c: jnp.logical_and(c[0] > 0.0, c[1] < 128),
                   lambda c: (jnp.max(bisect_steps()), c[1] + 1),
                   (jnp.max(state_ref[3:4, :]), jnp.int32(0)))

    @pl.when(jnp.max(state_ref[6:7, :]) > 0.0)
    def _():
        pend = state_ref[6:7, :] > 0.0
        hi = state_ref[1:2, :]

        def below_blk(kb, acc):
            r0 = pl.multiple_of(kb * kb_sz, kb_sz)
            blk = st_ref[pl.ds(r0, kb_sz), :]
            below = jnp.where(blk < hi, blk, -jnp.inf)
            return jnp.maximum(acc, jnp.max(below.reshape(COUNT_CHAINS, kb_sz // (8 * COUNT_CHAINS), 8, tq), axis=1))

        top = lax.fori_loop(0, nkb, below_blk, jnp.full((COUNT_CHAINS, 8, tq), -jnp.inf, F32))
        thr_p = jnp.where(pend, jnp.max(jnp.max(top, axis=0), axis=0, keepdims=True), state_ref[2:3, :])
        state_ref[2:3, :] = thr_p
        dup = jnp.logical_and(pend, count_ge(thr_p) > topk_f)
        state_ref[5:6, :] = jnp.maximum(state_ref[5:6, :], dup.astype(F32))

    thr = state_ref[2:3, :]

    @pl.when(jnp.max(state_ref[5:6, :]) > 0.0)
    def _():
        tie_row = state_ref[5:6, :] > 0.0

        def gt_blk(kb, acc):
            r0 = pl.multiple_of(kb * kb_sz, kb_sz)
            return acc + jnp.sum((st_ref[pl.ds(r0, kb_sz), :] > thr).astype(F32), axis=0, keepdims=True)

        need = topk_f - lax.fori_loop(0, nkb, gt_blk, jnp.zeros((1, tq), F32))
        tri = (lax.broadcasted_iota(jnp.int32, (kb_sz, kb_sz), 0)
               >= lax.broadcasted_iota(jnp.int32, (kb_sz, kb_sz), 1)).astype(BF16)

        def fix_blk(kb, run):
            r0 = pl.multiple_of(kb * kb_sz, kb_sz)
            blk = st_ref[pl.ds(r0, kb_sz), :]
            eq = jnp.logical_and(blk == thr, tie_row)
            eq_f = eq.astype(F32)
            rank = _dot(tri, eq_f.astype(BF16)) - eq_f + run
            st_ref[pl.ds(r0, kb_sz), :] = jnp.where(jnp.logical_and(eq, rank >= need), -jnp.inf, blk)
            return run + jnp.sum(eq_f, axis=0, keepdims=True)

        lax.fori_loop(0, nkb, fix_blk, jnp.zeros((1, tq), F32))

    red = lambda a, op: op(a.reshape(2, kb_sz // 16, 8, a.shape[-1]), axis=1)
    gw = GROUP * tq
    heads_per_trip = LOGIT_LANES // gw
    entry_of_lane = (lax.broadcasted_iota(jnp.int32, (1, gw), 1) % tq) // tqe
    for j0 in range(0, N_KV_HEADS, heads_per_trip):
        trip_heads = range(j0, j0 + heads_per_trip)
        qs = {j: jnp.concatenate(
            [query_rows(q_ref, slice((GROUP * j + g) * HEAD_DIM, (GROUP * j + g + 1) * HEAD_DIM))
             for g in range(GROUP)], axis=0) for j in trip_heads}

        def qk_trip(kb0, n_blk):
            m8 = stat_ref[...]
            for kb in [kb0 + d for d in range(n_blk)]:
                r0 = pl.multiple_of(kb * kb_sz, kb_sz)
                bias = jnp.where(st_ref[pl.ds(r0, kb_sz), :] >= thr, 0.0, NEG_BIAS)
                parts = []
                for j in trip_heads:
                    s = _dot_nt(key_rows(kb_ref, r0, slice(j * HEAD_DIM, (j + 1) * HEAD_DIM)), qs[j])
                    for g in range(GROUP):
                        sg = s[:, g * tq:(g + 1) * tq] + bias
                        c0 = (j - j0) * gw + g * tq
                        sbuf_ref[kb, :, c0:c0 + tq] = sg
                        parts.append(red(sg, jnp.max))
                m8 = jnp.maximum(m8, jnp.concatenate(parts, axis=-1))
            stat_ref[...] = m8

        stat_ref[...] = jnp.full(stat_ref.shape, NEG_BIAS, F32)
        _paired_blocks(nkb, qk_trip)
        m = jnp.max(jnp.max(stat_ref[...], axis=0), axis=0, keepdims=True)
        stat_ref[...] = jnp.zeros(stat_ref.shape, F32)
        acc_ref[...] = jnp.zeros(acc_ref.shape, F32)

        def pv_trip(kb0, n_blk):
            sums = []
            for j in trip_heads:
                cols = slice((j - j0) * gw, (j - j0 + 1) * gw)
                pv, psum = 0.0, 0.0
                for kb in [kb0 + d for d in range(n_blk)]:
                    r0 = pl.multiple_of(kb * kb_sz, kb_sz)
                    p = jnp.exp2(sbuf_ref[kb, :, cols] - m[:, cols])
                    p16 = p.astype(BF16)
                    t = None
                    for e in range(n_pack):
                        te = lax.dot_general(vb_ref[e, pl.ds(r0, kb_sz), j * HEAD_DIM:(j + 1) * HEAD_DIM], p16,
                                             (((0,), (0,)), ((), ())), preferred_element_type=F32)
                        t = te if t is None else jnp.where(entry_of_lane == e, te, t)
                    pv = pv + t
                    psum = psum + red(p, jnp.sum)
                acc_ref[:, cols] += pv
                sums.append(psum)
            stat_ref[...] += jnp.concatenate(sums, axis=-1)

        _paired_blocks(nkb, pv_trip)
        o = acc_ref[...] / jnp.sum(jnp.sum(stat_ref[...], axis=0), axis=0, keepdims=True)
        for j in trip_heads:
            for g in range(GROUP):
                hd = GROUP * j + g
                c0 = (j - j0) * gw + g * tq
                o_t = o[:, c0:c0 + tq].T.astype(o_ref.dtype)
                for e in range(n_pack):
                    o_ref[e, :, hd * HEAD_DIM:(hd + 1) * HEAD_DIM] = o_t[e * tqe:(e + 1) * tqe]


def _attend(q, iq, iwt, kb, vb, ik2, *, tq, n_pack, tq_real, q0, topk):
    b, t_q, _ = q.shape
    l_keys = kb.shape[1]
    tqe = tq // n_pack
    assert b % n_pack == 0 and t_q % tqe == 0 and (n_pack == 1 or t_q == tqe) and l_keys % KEY_BLOCK == 0
    n_kb = l_keys // KEY_BLOCK
    qspec = pl.BlockSpec((n_pack, tqe, ATTN_W), lambda bi, i: (bi, i, 0))
    kspec = pl.BlockSpec((n_pack, l_keys, KV_W), lambda bi, i: (bi, 0, 0))
    return pl.pallas_call(
        functools.partial(_attend_kernel, tq=tq, n_pack=n_pack, tq_real=tq_real, q0=q0, topk=topk),
        grid=(b // n_pack, t_q // tqe),
        in_specs=[qspec, qspec, pl.BlockSpec((1, N_IDX_HEADS, tq), lambda bi, i: (bi, 0, i)),
                  kspec, kspec, kspec],
        out_specs=qspec,
        out_shape=jax.ShapeDtypeStruct((b, t_q, ATTN_W), F32),
        scratch_shapes=[
            pltpu.VMEM((l_keys, tq), F32),
            pltpu.VMEM((n_kb, KEY_BLOCK, LOGIT_LANES), F32),
            pltpu.VMEM((8, tq), F32),
            pltpu.VMEM((HEAD_DIM, LOGIT_LANES), F32),
            pltpu.VMEM((2, 8, LOGIT_LANES), F32),
        ],
        compiler_params=pltpu.CompilerParams(dimension_semantics=("arbitrary", "arbitrary"),
                                             vmem_limit_bytes=VMEM_LIMIT),
        name="attend",
    )(q, iq, iwt, kb, vb, ik2)


def _pack_cache_kernel(c_ref, o_ref, *, past):
    for hd in range(N_KV_HEADS):
        rows_hd = pl.ds(hd, past, stride=N_KV_HEADS)
        o_ref[0, 0:past, hd * HEAD_DIM:(hd + 1) * HEAD_DIM] = c_ref[0, 0, rows_hd, :].astype(BF16)
    o_ref[0, past:, :] = jnp.zeros((o_ref.shape[1] - past, KV_W), BF16)


def _pack_idx_cache_kernel(c_ref, e_ref, o_ref, *, past):
    o_ref[0, 0:past, :] = _dot(c_ref[0, 0].astype(BF16), e_ref[...]).astype(BF16)
    o_ref[0, past:, :] = jnp.zeros((o_ref.shape[1] - past, 2 * LANES), BF16)


def _pack_cache(c, layer, l_pad):
    depth, b, past = c.shape[:3]
    c = c.reshape(depth, b, past * N_KV_HEADS, HEAD_DIM)
    return pl.pallas_call(
        functools.partial(_pack_cache_kernel, past=past),
        grid=(b,),
        in_specs=[pl.BlockSpec((1, 1, past * N_KV_HEADS, HEAD_DIM), lambda bi: (layer, bi, 0, 0))],
        out_specs=pl.BlockSpec((1, l_pad, KV_W), lambda bi: (bi, 0, 0)),
        out_shape=jax.ShapeDtypeStruct((b, l_pad, KV_W), BF16),
        compiler_params=pltpu.CompilerParams(dimension_semantics=("arbitrary",), vmem_limit_bytes=VMEM_LIMIT),
        name="pack_cache",
    )(c)


def _pack_idx_cache(c, layer, l_pad):
    _, b, past, _ = c.shape
    eye = jnp.eye(IDX_DIM, dtype=BF16)
    zero = jnp.zeros((IDX_DIM, IDX_DIM), BF16)
    select = jnp.concatenate([eye, zero, zero, eye], axis=1)
    return pl.pallas_call(
        functools.partial(_pack_idx_cache_kernel, past=past),
        grid=(b,),
        in_specs=[pl.BlockSpec((1, 1, past, IDX_DIM), lambda bi: (layer, bi, 0, 0)),
                  pl.BlockSpec((IDX_DIM, 2 * LANES), lambda bi: (0, 0))],
        out_specs=pl.BlockSpec((1, l_pad, 2 * LANES), lambda bi: (bi, 0, 0)),
        out_shape=jax.ShapeDtypeStruct((b, l_pad, 2 * LANES), BF16),
        compiler_params=pltpu.CompilerParams(dimension_semantics=("arbitrary",), vmem_limit_bytes=VMEM_LIMIT),
        name="pack_idx_cache",
    )(c, select)


def _proj_b_kernel(x_ref, g_ref, w_ref, convw_ref, convb_ref, poolw_ref, pscale_ref, liftb_ref, liftc_ref,
                   chist_ref, phist_ref, zbc_ref, cstate_ref, pstate_ref, cin_ext, pin_ext,
                   *, n_sub, ls, pos0):
    j = pl.program_id(1)

    @pl.when(j == 0)
    def _():
        cin_ext[:, 0:CONV_PAD, :] = chist_ref[...]
        pin_ext[:, 0:POOL_PAD, :] = phist_ref[...]

    @pl.when(j > 0)
    def _():
        cin_ext[:, 0:CONV_PAD, :] = cin_ext[:, ls:ls + CONV_PAD, :]
        pin_ext[:, 0:POOL_PAD, :] = pin_ext[:, ls:ls + POOL_PAD, :]

    h = _rms_h(x_ref[...], g_ref[...])
    rows = lambda a: jnp.concatenate(a, axis=0) if n_sub > 1 else a[0]

    y = _dot(h, w_ref[:, B_U:B_PIN])
    u = y[:, 0:CONV_W]
    b_gate = y[:, CONV_W:2 * CONV_W]
    c_gate = y[:, 2 * CONV_W:3 * CONV_W]
    gate_b = y[:, 3 * CONV_W:4 * CONV_W]
    cin = c_gate * u
    for s in range(n_sub):
        cin_ext[s, CONV_PAD:CONV_PAD + ls, :] = cin[s * ls:(s + 1) * ls]
    conv = (rows([cin_ext[s, CONV_PAD - 2:CONV_PAD - 2 + ls, :] for s in range(n_sub)]) * convw_ref[0:1, :]
            + rows([cin_ext[s, CONV_PAD - 1:CONV_PAD - 1 + ls, :] for s in range(n_sub)]) * convw_ref[1:2, :]
            + cin * convw_ref[2:3, :] + convb_ref[...])
    y_b = b_gate * conv * _silu(gate_b)
    zb = _dot(y_b.astype(BF16), liftb_ref[...])

    y = _dot(h, w_ref[:, B_PIN:B_MB])
    p_in = y[:, 0:POOL_W]
    gate_c = y[:, POOL_W:2 * POOL_W]
    for s in range(n_sub):
        pin_ext[s, POOL_PAD:POOL_PAD + ls, :] = p_in[s * ls:(s + 1) * ls]
    pos = rows([pos0 + j * ls + lax.broadcasted_iota(jnp.int32, (ls, 1), 0)] * n_sub)
    yc_parts = []
    for gi, win in enumerate(POOL_WINDOWS):
        sl = slice(gi * POOL_GROUP, (gi + 1) * POOL_GROUP)
        tot = p_in[:, sl]
        for back in range(1, win):
            tot = tot + rows([pin_ext[s, POOL_PAD - back:POOL_PAD - back + ls, sl] for s in range(n_sub)])
        cnt = jnp.minimum(win, pos + 1).astype(F32)
        d = tot / cnt - p_in[:, sl]
        mixed = _dot(d.astype(BF16), poolw_ref[gi])
        yc_parts.append(mixed * pscale_ref[:, sl] * _silu(gate_c[:, sl]))
    y_c = jnp.concatenate(yc_parts, axis=-1)
    zc = _dot(y_c.astype(BF16), liftc_ref[...])

    m_b = _dot(h, w_ref[:, B_MB:B_MC])
    m_c = _dot(h, w_ref[:, B_MC:B_END])
    zbc_ref[...] = jax.nn.sigmoid(m_b) * zb + jax.nn.sigmoid(m_c) * zc

    @pl.when(j == pl.num_programs(1) - 1)
    def _():
        cstate_ref[...] = cin_ext[:, ls:ls + CONV_PAD, :]
        pstate_ref[...] = pin_ext[:, ls:ls + POOL_PAD, :]


def _proj_b(x, g, w_b, conv_w, conv_b, pool_w, pool_scale, lift_b, lift_c, chist, phist, *, seq_len, tm, pos0):
    m = x.shape[0]
    n_seq = m // seq_len
    n_sub, ls = (1, tm) if tm <= seq_len else (tm // seq_len, seq_len)
    n_t = seq_len // ls
    out_shape = (
        jax.ShapeDtypeStruct((m, D_MODEL), F32),
        jax.ShapeDtypeStruct((n_seq, CONV_PAD, CONV_W), F32),
        jax.ShapeDtypeStruct((n_seq, POOL_PAD, POOL_W), F32),
    )
    hist = lambda r, w: pl.BlockSpec((n_sub, r, w), lambda s, j: (s, 0, 0))
    return pl.pallas_call(
        functools.partial(_proj_b_kernel, n_sub=n_sub, ls=ls, pos0=pos0),
        grid=(n_seq // n_sub, n_t),
        in_specs=[pl.BlockSpec((tm, D_MODEL), lambda s, j: (s * n_t + j, 0)),
                  _const_spec((1, D_MODEL)), _const_spec((D_MODEL, B_END)),
                  _const_spec((CONV_K, CONV_W)), _const_spec((1, CONV_W)),
                  _const_spec((len(POOL_WINDOWS), POOL_GROUP, POOL_GROUP)), _const_spec((1, POOL_W)),
                  _const_spec((CONV_W, D_MODEL)), _const_spec((POOL_W, D_MODEL)),
                  hist(CONV_PAD, CONV_W), hist(POOL_PAD, POOL_W)],
        out_specs=(pl.BlockSpec((tm, D_MODEL), lambda s, j: (s * n_t + j, 0)),
                   hist(CONV_PAD, CONV_W), hist(POOL_PAD, POOL_W)),
        out_shape=out_shape,
        scratch_shapes=[pltpu.VMEM((n_sub, ls + CONV_PAD, CONV_W), F32),
                        pltpu.VMEM((n_sub, ls + POOL_PAD, POOL_W), F32)],
        compiler_params=pltpu.CompilerParams(dimension_semantics=("arbitrary", "arbitrary"),
                                             vmem_limit_bytes=VMEM_LIMIT),
        name="proj_b",
    )(x, g, w_b, conv_w, conv_b, pool_w, pool_scale, lift_b, lift_c, chist, phist)


def _merge_kernel(x_ref, g_ref, wma_ref, attn_ref, ga_ref, lifta_ref, zbc_ref, wout_ref, fg_ref,
                  out_ref, *, final):
    x = x_ref[...]
    h = _rms_h(x, g_ref[...])
    y_a = attn_ref[...] * ga_ref[...].astype(F32)
    z = jax.nn.sigmoid(_dot(h, wma_ref[...])) * _dot(y_a.astype(BF16), lifta_ref[...]) + zbc_ref[...]
    out = x + _dot(z.astype(BF16), wout_ref[...])
    if final:
        out = out * lax.rsqrt(jnp.mean(out * out, axis=-1, keepdims=True) + EPS) * fg_ref[...]
    out_ref[...] = out


def _merge(x, g, w_ma, attn, ga, lift_a, zbc, w_out, final_g, *, tm, final):
    m = x.shape[0]
    row = lambda w: pl.BlockSpec((tm, w), lambda i: (i, 0))
    return pl.pallas_call(
        functools.partial(_merge_kernel, final=final),
        grid=(m // tm,),
        in_specs=[row(D_MODEL), _const_spec((1, D_MODEL)), _const_spec((D_MODEL, D_MODEL)),
                  row(ATTN_W), row(ATTN_W), _const_spec((ATTN_W, D_MODEL)), row(D_MODEL),
                  _const_spec((D_MODEL, D_MODEL)), _const_spec((1, D_MODEL))],
        out_specs=row(D_MODEL),
        out_shape=jax.ShapeDtypeStruct((m, D_MODEL), F32),
        compiler_params=pltpu.CompilerParams(dimension_semantics=("arbitrary",),
                                             vmem_limit_bytes=VMEM_LIMIT),
        name="merge",
    )(x, g, w_ma, attn, ga, lift_a, zbc, w_out, final_g)


def _rope_tables(pos):
    def tab(half, reps):
        inv = ROPE_THETA ** (-jnp.arange(half, dtype=F32) / half)
        ang = pos.astype(F32)[:, None] * inv[None, :]
        cos, sin = jnp.cos(ang), jnp.sin(ang)
        return (jnp.tile(jnp.concatenate([cos, cos], axis=-1), (1, reps)),
                jnp.tile(jnp.concatenate([-sin, sin], axis=-1), (1, reps)))
    c128, s128 = tab(HEAD_DIM // 2, 1)
    c64, s64 = tab(IDX_DIM // 2, LANES // IDX_DIM)
    return c128, s128, c64, s64


def _layer(x, tabs, lw, hist, cache, *, n_seq, seq_len, pos0, topk, tm_tok, tm_seq, tq, final, final_g):
    m = n_seq * seq_len
    q, k, v, kb, vb, iq, ik, ik2, iw, ga = _proj_a(x, lw["g"], lw["w_a"], tabs, tm_tok)

    def seq(a):
        return a.reshape(n_seq, seq_len, a.shape[-1])

    iwt = jnp.swapaxes(seq(iw), 1, 2)
    q3, iq3, kb3, vb3, ik23 = seq(q), seq(iq), seq(kb), seq(vb), seq(ik2)
    if cache is not None:
        ck, cv, cik, layer = cache
        past = ck.shape[2]
        l_pad = past + seq_len + (-(past + seq_len) % KEY_BLOCK)
        kb3 = lax.dynamic_update_slice(_pack_cache(ck, layer, l_pad), kb3, (0, past, 0))
        vb3 = lax.dynamic_update_slice(_pack_cache(cv, layer, l_pad), vb3, (0, past, 0))
        ik23 = lax.dynamic_update_slice(_pack_idx_cache(cik, layer, l_pad), ik23, (0, past, 0))
    n_pack = tq // seq_len if seq_len < tq else 1
    assert tq == n_pack * min(seq_len, tq) and n_seq % n_pack == 0
    if n_pack > 1:
        iwt = jnp.swapaxes(iwt.reshape(n_seq // n_pack, n_pack, N_IDX_HEADS, seq_len), 1, 2)
        iwt = iwt.reshape(n_seq // n_pack, N_IDX_HEADS, tq)
    attn = _attend(q3, iq3, iwt, kb3, vb3, ik23, tq=tq, n_pack=n_pack, tq_real=min(seq_len, tq), q0=pos0, topk=topk)
    attn = attn.reshape(m, ATTN_W)

    zbc, cstate, pstate = _proj_b(x, lw["g"], lw["w_b"], lw["conv_w"], lw["conv_b"], lw["pool_w"],
                                  lw["pool_scale"], lw["lift_b"], lw["lift_c"], hist[0], hist[1],
                                  seq_len=seq_len, tm=tm_seq, pos0=pos0)
    outs = _merge(x, lw["g"], lw["w_ma"], attn, ga, lw["lift_a"], zbc, lw["w_out"], final_g,
                  tm=tm_tok, final=final)
    states = (k.reshape(n_seq, seq_len, N_KV_HEADS, HEAD_DIM), v.reshape(n_seq, seq_len, N_KV_HEADS, HEAD_DIM),
              ik.reshape(n_seq, seq_len, IDX_DIM), cstate[:, CONV_PAD - (CONV_K - 1):], pstate[:, POOL_PAD - POOL_HIST:])
    return outs, states


def kernel(x_prompt, x_sample, cache_k, cache_v, cache_idx_k, state_conv, state_pool, norm_g, w_in, conv_w,
           conv_b, pool_w, pool_scale, lift_a, lift_b, lift_c, w_out, final_norm_g):
    batch, seq, _ = x_prompt.shape
    dec_batch, dec_seq, _ = x_sample.shape
    depth = w_in.shape[0]
    past_len = cache_k.shape[2]
    topk_prompt = min(MAX_TOPK, seq // 4)
    topk_sample = min(MAX_TOPK, (past_len + dec_seq) // 4)

    tm_p, tm_s = 256, 256
    tabs_p = _rope_tables(jnp.arange(seq, dtype=jnp.int32))
    tabs_s = _rope_tables(past_len + (jnp.arange(tm_s, dtype=jnp.int32) % dec_seq))
    final_g = final_norm_g.reshape(1, D_MODEL)

    hp = x_prompt.reshape(batch * seq, D_MODEL)
    hs = x_sample.reshape(dec_batch * dec_seq, D_MODEL)
    zero_hist = (jnp.zeros((batch, CONV_PAD, CONV_W), F32), jnp.zeros((batch, POOL_PAD, POOL_W), F32))
    p_states, s_states = [], []
    for l in range(depth):
        w_a, w_b, w_ma = _prep_weights(w_in, l)
        lw = {
            "g": norm_g[l].reshape(1, D_MODEL),
            "w_a": w_a, "w_b": w_b, "w_ma": w_ma,
            "conv_w": conv_w[l], "conv_b": conv_b[l].reshape(1, CONV_W),
            "pool_w": pool_w[l].astype(BF16), "pool_scale": pool_scale[l].reshape(1, POOL_W),
            "lift_a": lift_a[l].astype(BF16), "lift_b": lift_b[l].astype(BF16), "lift_c": lift_c[l].astype(BF16),
            "w_out": w_out[l].astype(BF16),
        }
        final = l == depth - 1
        outs_p, st_p = _layer(hp, tabs_p, lw, zero_hist, None, n_seq=batch, seq_len=seq, pos0=0,
                              topk=topk_prompt, tm_tok=tm_p, tm_seq=tm_p, tq=Q_TILE_PROMPT, final=final,
                              final_g=final_g)
        cache = (cache_k, cache_v, cache_idx_k, l)
        hist_s = (jnp.pad(state_conv[l], ((0, 0), (CONV_PAD - (CONV_K - 1), 0), (0, 0))),
                  jnp.pad(state_pool[l], ((0, 0), (POOL_PAD - POOL_HIST, 0), (0, 0))))
        outs_s, st_s = _layer(hs, tabs_s, lw, hist_s, cache, n_seq=dec_batch, seq_len=dec_seq, pos0=past_len,
                              topk=topk_sample, tm_tok=tm_s, tm_seq=tm_s, tq=Q_TILE_SAMPLE, final=final,
                              final_g=final_g)
        hp, hs = outs_p, outs_s
        p_states.append(st_p)
        s_states.append(st_s)

    stack = lambda sts, i: jnp.stack([st[i] for st in sts])
    return (hp.reshape(batch, seq, D_MODEL), hs.reshape(dec_batch, dec_seq, D_MODEL),
            stack(p_states, 0), stack(p_states, 1), stack(p_states, 2), stack(p_states, 3), stack(p_states, 4),
            stack(s_states, 0), stack(s_states, 1), stack(s_states, 2), stack(s_states, 3), stack(s_states, 4))
```

```python
import functools

import jax
import jax.numpy as jnp
from jax import lax
from jax.experimental import pallas as pl
from jax.experimental.pallas import tpu as pltpu

D_MODEL = 2048
CHUNK = 64
N_HEADS = 8
HEAD_DIM = 128
N_KV_HEADS = 2
GROUP = N_HEADS // N_KV_HEADS
ATTN_W = N_HEADS * HEAD_DIM
KV_W = N_KV_HEADS * HEAD_DIM
N_IDX_HEADS = 16
IDX_DIM = 64
IDX_Q_W = N_IDX_HEADS * IDX_DIM
MAX_TOPK = 256
CONV_W = 512
CONV_K = 3
POOL_W = 512
POOL_WINDOWS = (2, 4, 8, 16)
POOL_GROUP = 128
POOL_HIST = 15
ROPE_THETA = 10000.0
EPS = 1e-6

LANES = 128
CONV_PAD = 8
POOL_PAD = 16
KEY_BLOCK = 512
PREP_ROWS = 256
SCORE_GROUP = 2
PASS_GROUP = 4
Q_TILE_PROMPT = 256
Q_TILE_SAMPLE = 128
LOGIT_LANES = 1024
NEG_BIAS = -1e30
COUNT_CHAINS = 4
BISECT_UNROLL = 4
BISECT_UNCHECKED = 4
LOG2E = 1.4426950408889634
F32_LOWEST = -3.0e38
VMEM_LIMIT = 56 * 1024 * 1024

F32 = jnp.float32
BF16 = jnp.bfloat16

_SIZES = (ATTN_W, KV_W, KV_W, IDX_Q_W, IDX_DIM, N_IDX_HEADS, ATTN_W,
          CONV_W, CONV_W, CONV_W, CONV_W, POOL_W, POOL_W, D_MODEL, D_MODEL, D_MODEL)
_OFFS = [0]
for _s in _SIZES:
    _OFFS.append(_OFFS[-1] + _s)
(O_Q, O_K, O_V, O_IQ, O_IK, O_IW, O_GA, O_U, O_BG, O_CG, O_GB, O_PIN, O_GC, O_MA, O_MB, O_MC, O_END) = _OFFS

A_Q, A_K, A_V, A_IQ, A_IKW, A_GA, A_END = 0, 1024, 1280, 1536, 2560, 2688, 3712
B_U, B_BG, B_CG, B_GB, B_PIN, B_GC, B_MB, B_MC, B_END = 0, 512, 1024, 1536, 2048, 2560, 3072, 5120, 7168


def _dot(a, b):
    return jnp.dot(a, b, preferred_element_type=F32)


def _dot_nt(a, b):
    return lax.dot_general(a, b, (((1,), (1,)), ((), ())), preferred_element_type=F32)


def _rms_h(x, g):
    h = x * lax.rsqrt(jnp.mean(x * x, axis=-1, keepdims=True) + EPS) * g
    return h.astype(BF16)


def _silu(x):
    return x * jax.nn.sigmoid(x)


def _const_spec(shape):
    nd = len(shape)
    return pl.BlockSpec(shape, lambda *_: (0,) * nd, pipeline_mode=pl.Buffered(1))


def _prep_weights_kernel(w_ref, wa_ref, wb_ref, wma_ref):
    cols = lambda a, b: w_ref[0, :, a:b].astype(BF16)
    ikw_end = A_IKW + IDX_DIM + N_IDX_HEADS
    wa_ref[:, A_Q:ikw_end] = cols(O_Q, O_GA)
    wa_ref[:, ikw_end:A_GA] = jnp.zeros((wa_ref.shape[0], A_GA - ikw_end), BF16)
    wa_ref[:, A_GA:A_END] = cols(O_GA, O_U)
    wb_ref[:, B_U:B_MB] = cols(O_U, O_MA)
    wb_ref[:, B_MB:B_END] = cols(O_MB, O_END)
    wma_ref[...] = cols(O_MA, O_MB)


def _prep_weights(w_in, layer):
    d, in_w = w_in.shape[1:]
    rows = PREP_ROWS
    slab = lambda w: pl.BlockSpec((rows, w), lambda i: (i, 0))
    return pl.pallas_call(
        _prep_weights_kernel,
        grid=(d // rows,),
        in_specs=[pl.BlockSpec((1, rows, in_w), lambda i: (layer, i, 0))],
        out_specs=(slab(A_END), slab(B_END), slab(D_MODEL)),
        out_shape=(jax.ShapeDtypeStruct((d, A_END), BF16), jax.ShapeDtypeStruct((d, B_END), BF16),
                   jax.ShapeDtypeStruct((d, D_MODEL), BF16)),
        compiler_params=pltpu.CompilerParams(dimension_semantics=("arbitrary",), vmem_limit_bytes=VMEM_LIMIT),
        name="prep_weights",
    )(w_in)


def _proj_a_kernel(x_ref, g_ref, w_ref, c128_ref, s128_ref, c64_ref, s64_ref,
                   q_ref, k_ref, v_ref, kb_ref, vb_ref, iq_ref, ik_ref, ik2_ref, iw_ref, ga_ref):
    h = _rms_h(x_ref[...], g_ref[...])
    c128 = c128_ref[...]
    s128 = s128_ref[...]
    c64 = c64_ref[...]
    s64 = s64_ref[...]
    lane = lax.broadcasted_iota(jnp.int32, (1, LANES), 1)
    first_half64 = (lane % IDX_DIM) < (IDX_DIM // 2)

    def rope128(y):
        return y * c128 + pltpu.roll(y, HEAD_DIM // 2, 1) * s128

    def rope64(y):
        partner = jnp.where(first_half64, pltpu.roll(y, LANES - IDX_DIM // 2, 1),
                            pltpu.roll(y, IDX_DIM // 2, 1))
        return y * c64 + partner * s64

    y = _dot(h, w_ref[:, A_Q:A_K])
    for hd in range(N_HEADS):
        sl = slice(hd * HEAD_DIM, (hd + 1) * HEAD_DIM)
        q_ref[:, sl] = (rope128(y[:, sl]) * (HEAD_DIM ** -0.5 * LOG2E)).astype(BF16)

    y = _dot(h, w_ref[:, A_K:A_IQ])
    for hd in range(N_KV_HEADS):
        sl = slice(hd * HEAD_DIM, (hd + 1) * HEAD_DIM)
        kr = rope128(y[:, sl])
        rows_hd = pl.ds(hd, y.shape[0], stride=N_KV_HEADS)
        k_ref[rows_hd, :] = kr
        kb_ref[:, sl] = kr.astype(BF16)
        v_ref[rows_hd, :] = y[:, KV_W + hd * HEAD_DIM:KV_W + (hd + 1) * HEAD_DIM]
    vb_ref[...] = y[:, KV_W:].astype(BF16)

    y = _dot(h, w_ref[:, A_IQ:A_IKW])
    for c in range(IDX_Q_W // LANES):
        sl = slice(c * LANES, (c + 1) * LANES)
        iq_ref[:, sl] = rope64(y[:, sl]).astype(BF16)

    y = _dot(h, w_ref[:, A_IKW:A_GA])
    ikr = rope64(y)
    ik_ref[...] = ikr[:, :IDX_DIM]
    ikz = jnp.where(lane < IDX_DIM, ikr, 0.0)
    ik2_ref[:, :LANES] = ikz.astype(BF16)
    ik2_ref[:, LANES:] = pltpu.roll(ikz, IDX_DIM, 1).astype(BF16)
    iw_ref[...] = y[:, IDX_DIM:IDX_DIM + N_IDX_HEADS] * ((IDX_DIM ** -0.5) * (N_IDX_HEADS ** -0.5))

    ga_ref[...] = _silu(_dot(h, w_ref[:, A_GA:A_END])).astype(BF16)


def _proj_a(x, g, w_a, tabs, tm):
    m = x.shape[0]
    c128, s128, c64, s64 = tabs
    n_pt = c128.shape[0] // tm
    row = lambda w: pl.BlockSpec((tm, w), lambda i: (i, 0))
    tab = pl.BlockSpec((tm, LANES), lambda i: (i % n_pt, 0))
    out_shape = (
        jax.ShapeDtypeStruct((m, ATTN_W), BF16),
        jax.ShapeDtypeStruct((m * N_KV_HEADS, HEAD_DIM), F32),
        jax.ShapeDtypeStruct((m * N_KV_HEADS, HEAD_DIM), F32),
        jax.ShapeDtypeStruct((m, KV_W), BF16),
        jax.ShapeDtypeStruct((m, KV_W), BF16),
        jax.ShapeDtypeStruct((m, IDX_Q_W), BF16),
        jax.ShapeDtypeStruct((m, IDX_DIM), F32),
        jax.ShapeDtypeStruct((m, 2 * LANES), BF16),
        jax.ShapeDtypeStruct((m, N_IDX_HEADS), F32),
        jax.ShapeDtypeStruct((m, ATTN_W), BF16),
    )
    kv_heads = pl.BlockSpec((tm * N_KV_HEADS, HEAD_DIM), lambda i: (i, 0))
    out_specs = (row(ATTN_W), kv_heads, kv_heads, row(KV_W), row(KV_W), row(IDX_Q_W),
                 row(IDX_DIM), row(2 * LANES), row(N_IDX_HEADS), row(ATTN_W))
    return pl.pallas_call(
        _proj_a_kernel,
        grid=(m // tm,),
        in_specs=[row(D_MODEL), _const_spec((1, D_MODEL)), _const_spec((D_MODEL, A_END)),
                  tab, tab, tab, tab],
        out_specs=out_specs,
        out_shape=out_shape,
        compiler_params=pltpu.CompilerParams(dimension_semantics=("arbitrary",),
                                             vmem_limit_bytes=VMEM_LIMIT),
        name="proj_a",
    )(x, g, w_a, c128, s128, c64, s64)


def _grouped_blocks(n_blocks, trip, group):
    def full(t, carry):
        trip(group * t, group)
        return carry

    n_full = n_blocks // group
    lax.fori_loop(0, n_full, full, 0)
    done = n_full * group
    size = group // 2
    while size >= 1:
        has = ((n_blocks - done) // size) % 2 == 1

        @pl.when(has)
        def _(done=done, size=size):
            trip(done, size)

        done = done + jnp.where(has, size, 0)
        size //= 2


def _attend_kernel(q_ref, iq_ref, iwt_ref, kb_ref, vb_ref, ik2_ref, o_ref,
                   st_ref, sbuf_ref, state_ref, acc_ref, stat_ref,
                   *, tq, n_pack, tq_real, q0, topk):
    kb_sz = KEY_BLOCK
    tqe = tq // n_pack
    i = pl.program_id(1)
    qpos0 = q0 + i * tq_real
    n_keys = ((qpos0 + tq_real - 1) // CHUNK + 1) * CHUNK
    nkb = (n_keys + kb_sz - 1) // kb_sz

    lane_q = lax.broadcasted_iota(jnp.int32, (1, tq), 1)
    lane_e = lane_q % tqe
    n_adm_i = ((qpos0 + lane_e) // CHUNK + 1) * CHUNK
    iw = iwt_ref[0]

    def query_rows(ref, cols):
        if n_pack == 1:
            return ref[0, :, cols]
        zero = jnp.zeros((tqe, cols.stop - cols.start), ref.dtype)
        return jnp.concatenate(
            [jnp.concatenate([ref[e, :, cols] if c == e else zero for c in range(n_pack)], axis=1)
             for e in range(n_pack)], axis=0)

    def key_rows(ref, r0, cols):
        if n_pack == 1:
            return ref[0, pl.ds(r0, kb_sz), cols]
        return jnp.concatenate([ref[e, pl.ds(r0, kb_sz), cols] for e in range(n_pack)], axis=1)

    state_ref[0:1, :] = jnp.full((1, tq), jnp.inf, F32)
    state_ref[1:2, :] = jnp.full((1, tq), -jnp.inf, F32)

    def score_trip(kb0, n_blk):
        mn, mx = state_ref[0:1, :], state_ref[1:2, :]
        for kb in [kb0 + d for d in range(n_blk)]:
            r0 = pl.multiple_of(kb * kb_sz, kb_sz)
            ik_even = key_rows(ik2_ref, r0, slice(0, LANES))
            ik_odd = key_rows(ik2_ref, r0, slice(LANES, 2 * LANES))
            acc = jnp.zeros((kb_sz, tq), F32)
            for p in range(N_IDX_HEADS // 2):
                iq_pair = query_rows(iq_ref, slice(p * LANES, (p + 1) * LANES))
                d0 = _dot_nt(ik_even, iq_pair)
                d1 = _dot_nt(ik_odd, iq_pair)
                acc = acc + jnp.maximum(d0, 0.0) * iw[2 * p:2 * p + 1]
                acc = acc + jnp.maximum(d1, 0.0) * iw[2 * p + 1:2 * p + 2]
            key = r0 + lax.broadcasted_iota(jnp.int32, (kb_sz, 1), 0)
            masked = jnp.where(key < n_adm_i, acc, -jnp.inf)
            st_ref[pl.ds(r0, kb_sz), :] = masked
            mx = jnp.maximum(mx, jnp.max(masked, axis=0, keepdims=True))
            mn = jnp.minimum(mn, jnp.min(acc, axis=0, keepdims=True))
        state_ref[0:1, :] = mn
        state_ref[1:2, :] = mx

    _grouped_blocks(nkb, score_trip, SCORE_GROUP)
    mn, mx = state_ref[0:1, :], state_ref[1:2, :]

    n_adm = n_adm_i.astype(F32)
    topk_f = float(topk)
    active0 = jnp.logical_and(n_adm > topk_f, lane_e < tq_real).astype(F32)
    state_ref[0:1, :] = mn
    state_ref[1:2, :] = mx + jnp.maximum(jnp.abs(mx), 1e-30)
    state_ref[2:3, :] = jnp.full((1, tq), F32_LOWEST, F32)
    state_ref[3:4, :] = active0
    state_ref[4:5, :] = n_adm
    state_ref[5:6, :] = jnp.zeros((1, tq), F32)
    state_ref[6:7, :] = jnp.zeros((1, tq), F32)

    def count_ge(mid):
        sub = kb_sz // COUNT_CHAINS

        def cnt_blk(kb, accs):
            r0 = pl.multiple_of(kb * kb_sz, kb_sz)
            out = []
            for c in range(COUNT_CHAINS):
                ge = (st_ref[pl.ds(r0 + c * sub, sub), :] >= mid).astype(F32)
                out.append(accs[c] + jnp.sum(ge.reshape(sub // 8, 8, tq), axis=0))
            return tuple(out)

        accs = lax.fori_loop(0, nkb, cnt_blk, tuple(jnp.zeros((8, tq), F32) for _ in range(COUNT_CHAINS)))
        return jnp.sum(sum(accs), axis=0, keepdims=True)

    def bisect_step(st):
        lo, hi, thr, active, clo, tie, pend = st
        mid = 0.5 * lo + 0.5 * hi
        stuck = jnp.logical_or(mid <= lo, mid >= hi)
        cnt = count_ge(mid)
        ge_k = cnt >= topk_f
        exact = cnt == topk_f
        moving = jnp.logical_and(active, jnp.logical_not(stuck))
        ended = jnp.logical_and(active, stuck)
        thr = jnp.where(ended, lo, jnp.where(jnp.logical_and(moving, exact), mid, thr))
        tie = jnp.logical_or(tie, jnp.logical_and(ended, clo > topk_f))
        up = jnp.logical_and(moving, ge_k)
        down = jnp.logical_and(moving, jnp.logical_not(ge_k))
        near = jnp.logical_and(moving, cnt == topk_f - 1.0)
        done = jnp.logical_or(exact, near)
        return (jnp.where(up, mid, lo), jnp.where(down, mid, hi), thr,
                jnp.logical_and(moving, jnp.logical_not(done)), jnp.where(up, cnt, clo), tie,
                jnp.logical_or(pend, near))

    def bisect_steps():
        st = (state_ref[0:1, :], state_ref[1:2, :], state_ref[2:3, :], state_ref[3:4, :] > 0.0,
              state_ref[4:5, :], state_ref[5:6, :] > 0.0, state_ref[6:7, :] > 0.0)
        for _ in range(BISECT_UNROLL):
            st = bisect_step(st)
        active_f = st[3].astype(F32)
        state_ref[0:1, :] = st[0]
        state_ref[1:2, :] = st[1]
        state_ref[2:3, :] = st[2]
        state_ref[3:4, :] = active_f
        state_ref[4:5, :] = st[4]
        state_ref[5:6, :] = st[5].astype(F32)
        state_ref[6:7, :] = st[6].astype(F32)
        return active_f

    def unchecked(i, carry):
        bisect_steps()
        return carry

    lax.fori_loop(0, BISECT_UNCHECKED, unchecked, 0)
    lax.while_loop(lambda c: jnp.logical_and(c[0] > 0.0, c[1] < 128),
                   lambda c: (jnp.max(bisect_steps()), c[1] + 1),
                   (jnp.max(state_ref[3:4, :]), jnp.int32(0)))

    @pl.when(jnp.max(state_ref[6:7, :]) > 0.0)
    def _():
        pend = state_ref[6:7, :] > 0.0
        hi = state_ref[1:2, :]

        def below_blk(kb, acc):
            r0 = pl.multiple_of(kb * kb_sz, kb_sz)
            blk = st_ref[pl.ds(r0, kb_sz), :]
            below = jnp.where(blk < hi, blk, -jnp.inf)
            return jnp.maximum(acc, jnp.max(below.reshape(COUNT_CHAINS, kb_sz // (8 * COUNT_CHAINS), 8, tq), axis=1))

        top = lax.fori_loop(0, nkb, below_blk, jnp.full((COUNT_CHAINS, 8, tq), -jnp.inf, F32))
        thr_p = jnp.where(pend, jnp.max(jnp.max(top, axis=0), axis=0, keepdims=True), state_ref[2:3, :])
        state_ref[2:3, :] = thr_p
        dup = jnp.logical_and(pend, count_ge(thr_p) > topk_f)
        state_ref[5:6, :] = jnp.maximum(state_ref[5:6, :], dup.astype(F32))

    thr = state_ref[2:3, :]

    @pl.when(jnp.max(state_ref[5:6, :]) > 0.0)
    def _():
        tie_row = state_ref[5:6, :] > 0.0

        def gt_blk(kb, acc):
            r0 = pl.multiple_of(kb * kb_sz, kb_sz)
            return acc + jnp.sum((st_ref[pl.ds(r0, kb_sz), :] > thr).astype(F32), axis=0, keepdims=True)

        need = topk_f - lax.fori_loop(0, nkb, gt_blk, jnp.zeros((1, tq), F32))
        tri = (lax.broadcasted_iota(jnp.int32, (kb_sz, kb_sz), 0)
               >= lax.broadcasted_iota(jnp.int32, (kb_sz, kb_sz), 1)).astype(BF16)

        def fix_blk(kb, run):
            r0 = pl.multiple_of(kb * kb_sz, kb_sz)
            blk = st_ref[pl.ds(r0, kb_sz), :]
            eq = jnp.logical_and(blk == thr, tie_row)
            eq_f = eq.astype(F32)
            rank = _dot(tri, eq_f.astype(BF16)) - eq_f + run
            st_ref[pl.ds(r0, kb_sz), :] = jnp.where(jnp.logical_and(eq, rank >= need), -jnp.inf, blk)
            return run + jnp.sum(eq_f, axis=0, keepdims=True)

        lax.fori_loop(0, nkb, fix_blk, jnp.zeros((1, tq), F32))

    red = lambda a, op: op(a.reshape(2, kb_sz // 16, 8, a.shape[-1]), axis=1)
    gw = GROUP * tq
    heads_per_trip = LOGIT_LANES // gw
    entry_of_lane = (lax.broadcasted_iota(jnp.int32, (1, gw), 1) % tq) // tqe
    for j0 in range(0, N_KV_HEADS, heads_per_trip):
        trip_heads = range(j0, j0 + heads_per_trip)
        qs = {j: jnp.concatenate(
            [query_rows(q_ref, slice((GROUP * j + g) * HEAD_DIM, (GROUP * j + g + 1) * HEAD_DIM))
             for g in range(GROUP)], axis=0) for j in trip_heads}

        def qk_trip(kb0, n_blk):
            m8 = stat_ref[...]
            for kb in [kb0 + d for d in range(n_blk)]:
                r0 = pl.multiple_of(kb * kb_sz, kb_sz)
                bias = jnp.where(st_ref[pl.ds(r0, kb_sz), :] >= thr, 0.0, NEG_BIAS)
                parts = []
                for j in trip_heads:
                    s = _dot_nt(key_rows(kb_ref, r0, slice(j * HEAD_DIM, (j + 1) * HEAD_DIM)), qs[j])
                    for g in range(GROUP):
                        sg = s[:, g * tq:(g + 1) * tq] + bias
                        c0 = (j - j0) * gw + g * tq
                        sbuf_ref[kb, :, c0:c0 + tq] = sg
                        parts.append(red(sg, jnp.max))
                m8 = jnp.maximum(m8, jnp.concatenate(parts, axis=-1))
            stat_ref[...] = m8

        stat_ref[...] = jnp.full(stat_ref.shape, NEG_BIAS, F32)
        _grouped_blocks(nkb, qk_trip, PASS_GROUP)
        m = jnp.max(jnp.max(stat_ref[...], axis=0), axis=0, keepdims=True)
        stat_ref[...] = jnp.zeros(stat_ref.shape, F32)
        acc_ref[...] = jnp.zeros(acc_ref.shape, F32)

        def pv_trip(kb0, n_blk):
            sums = []
            for j in trip_heads:
                cols = slice((j - j0) * gw, (j - j0 + 1) * gw)
                pv, psum = 0.0, 0.0
                for kb in [kb0 + d for d in range(n_blk)]:
                    r0 = pl.multiple_of(kb * kb_sz, kb_sz)
                    p = jnp.exp2(sbuf_ref[kb, :, cols] - m[:, cols])
                    p16 = p.astype(BF16)
                    t = None
                    for e in range(n_pack):
                        te = lax.dot_general(vb_ref[e, pl.ds(r0, kb_sz), j * HEAD_DIM:(j + 1) * HEAD_DIM], p16,
                                             (((0,), (0,)), ((), ())), preferred_element_type=F32)
                        t = te if t is None else jnp.where(entry_of_lane == e, te, t)
                    pv = pv + t
                    psum = psum + red(p, jnp.sum)
                acc_ref[:, cols] += pv
                sums.append(psum)
            stat_ref[...] += jnp.concatenate(sums, axis=-1)

        _grouped_blocks(nkb, pv_trip, PASS_GROUP)
        o = acc_ref[...] / jnp.sum(jnp.sum(stat_ref[...], axis=0), axis=0, keepdims=True)
        for j in trip_heads:
            for g in range(GROUP):
                hd = GROUP * j + g
                c0 = (j - j0) * gw + g * tq
                o_t = o[:, c0:c0 + tq].T.astype(o_ref.dtype)
                for e in range(n_pack):
                    o_ref[e, :, hd * HEAD_DIM:(hd + 1) * HEAD_DIM] = o_t[e * tqe:(e + 1) * tqe]


def _attend(q, iq, iwt, kb, vb, ik2, *, tq, n_pack, tq_real, q0, topk):
    b, t_q, _ = q.shape
    l_keys = kb.shape[1]
    tqe = tq // n_pack
    assert b % n_pack == 0 and t_q % tqe == 0 and (n_pack == 1 or t_q == tqe) and l_keys % KEY_BLOCK == 0
    n_kb = l_keys // KEY_BLOCK
    qspec = pl.BlockSpec((n_pack, tqe, ATTN_W), lambda bi, i: (bi, i, 0))
    kspec = pl.BlockSpec((n_pack, l_keys, KV_W), lambda bi, i: (bi, 0, 0))
    return pl.pallas_call(
        functools.partial(_attend_kernel, tq=tq, n_pack=n_pack, tq_real=tq_real, q0=q0, topk=topk),
        grid=(b // n_pack, t_q // tqe),
        in_specs=[qspec, qspec, pl.BlockSpec((1, N_IDX_HEADS, tq), lambda bi, i: (bi, 0, i)),
                  kspec, kspec, kspec],
        out_specs=qspec,
        out_shape=jax.ShapeDtypeStruct((b, t_q, ATTN_W), F32),
        scratch_shapes=[
            pltpu.VMEM((l_keys, tq), F32),
            pltpu.VMEM((n_kb, KEY_BLOCK, LOGIT_LANES), F32),
            pltpu.VMEM((8, tq), F32),
            pltpu.VMEM((HEAD_DIM, LOGIT_LANES), F32),
            pltpu.VMEM((2, 8, LOGIT_LANES), F32),
        ],
        compiler_params=pltpu.CompilerParams(dimension_semantics=("arbitrary", "arbitrary"),
                                             vmem_limit_bytes=VMEM_LIMIT),
        name="attend",
    )(q, iq, iwt, kb, vb, ik2)


def _pack_cache_kernel(c_ref, o_ref, *, past):
    for hd in range(N_KV_HEADS):
        rows_hd = pl.ds(hd, past, stride=N_KV_HEADS)
        o_ref[0, 0:past, hd * HEAD_DIM:(hd + 1) * HEAD_DIM] = c_ref[0, 0, rows_hd, :].astype(BF16)
    o_ref[0, past:, :] = jnp.zeros((o_ref.shape[1] - past, KV_W), BF16)


def _pack_idx_cache_kernel(c_ref, e_ref, o_ref, *, past):
    o_ref[0, 0:past, :] = _dot(c_ref[0, 0].astype(BF16), e_ref[...]).astype(BF16)
    o_ref[0, past:, :] = jnp.zeros((o_ref.shape[1] - past, 2 * LANES), BF16)


def _pack_cache(c, layer, l_pad):
    depth, b, past = c.shape[:3]
    c = c.reshape(depth, b, past * N_KV_HEADS, HEAD_DIM)
    return pl.pallas_call(
        functools.partial(_pack_cache_kernel, past=past),
        grid=(b,),
        in_specs=[pl.BlockSpec((1, 1, past * N_KV_HEADS, HEAD_DIM), lambda bi: (layer, bi, 0, 0))],
        out_specs=pl.BlockSpec((1, l_pad, KV_W), lambda bi: (bi, 0, 0)),
        out_shape=jax.ShapeDtypeStruct((b, l_pad, KV_W), BF16),
        compiler_params=pltpu.CompilerParams(dimension_semantics=("arbitrary",), vmem_limit_bytes=VMEM_LIMIT),
        name="pack_cache",
    )(c)


def _pack_idx_cache(c, layer, l_pad):
    _, b, past, _ = c.shape
    eye = jnp.eye(IDX_DIM, dtype=BF16)
    zero = jnp.zeros((IDX_DIM, IDX_DIM), BF16)
    select = jnp.concatenate([eye, zero, zero, eye], axis=1)
    return pl.pallas_call(
        functools.partial(_pack_idx_cache_kernel, past=past),
        grid=(b,),
        in_specs=[pl.BlockSpec((1, 1, past, IDX_DIM), lambda bi: (layer, bi, 0, 0)),
                  pl.BlockSpec((IDX_DIM, 2 * LANES), lambda bi: (0, 0))],
        out_specs=pl.BlockSpec((1, l_pad, 2 * LANES), lambda bi: (bi, 0, 0)),
        out_shape=jax.ShapeDtypeStruct((b, l_pad, 2 * LANES), BF16),
        compiler_params=pltpu.CompilerParams(dimension_semantics=("arbitrary",), vmem_limit_bytes=VMEM_LIMIT),
        name="pack_idx_cache",
    )(c, select)


def _proj_b_kernel(x_ref, g_ref, w_ref, convw_ref, convb_ref, poolw_ref, pscale_ref, liftb_ref, liftc_ref,
                   chist_ref, phist_ref, zbc_ref, cstate_ref, pstate_ref, cin_ext, pin_ext,
                   *, n_sub, ls, pos0):
    j = pl.program_id(1)

    @pl.when(j == 0)
    def _():
        cin_ext[:, 0:CONV_PAD, :] = chist_ref[...]
        pin_ext[:, 0:POOL_PAD, :] = phist_ref[...]

    @pl.when(j > 0)
    def _():
        cin_ext[:, 0:CONV_PAD, :] = cin_ext[:, ls:ls + CONV_PAD, :]
        pin_ext[:, 0:POOL_PAD, :] = pin_ext[:, ls:ls + POOL_PAD, :]

    h = _rms_h(x_ref[...], g_ref[...])
    rows = lambda a: jnp.concatenate(a, axis=0) if n_sub > 1 else a[0]

    y = _dot(h, w_ref[:, B_U:B_PIN])
    u = y[:, 0:CONV_W]
    b_gate = y[:, CONV_W:2 * CONV_W]
    c_gate = y[:, 2 * CONV_W:3 * CONV_W]
    gate_b = y[:, 3 * CONV_W:4 * CONV_W]
    cin = c_gate * u
    for s in range(n_sub):
        cin_ext[s, CONV_PAD:CONV_PAD + ls, :] = cin[s * ls:(s + 1) * ls]
    conv = (rows([cin_ext[s, CONV_PAD - 2:CONV_PAD - 2 + ls, :] for s in range(n_sub)]) * convw_ref[0:1, :]
            + rows([cin_ext[s, CONV_PAD - 1:CONV_PAD - 1 + ls, :] for s in range(n_sub)]) * convw_ref[1:2, :]
            + cin * convw_ref[2:3, :] + convb_ref[...])
    y_b = b_gate * conv * _silu(gate_b)
    zb = _dot(y_b.astype(BF16), liftb_ref[...])

    y = _dot(h, w_ref[:, B_PIN:B_MB])
    p_in = y[:, 0:POOL_W]
    gate_c = y[:, POOL_W:2 * POOL_W]
    for s in range(n_sub):
        pin_ext[s, POOL_PAD:POOL_PAD + ls, :] = p_in[s * ls:(s + 1) * ls]
    pos = rows([pos0 + j * ls + lax.broadcasted_iota(jnp.int32, (ls, 1), 0)] * n_sub)
    yc_parts = []
    for gi, win in enumerate(POOL_WINDOWS):
        sl = slice(gi * POOL_GROUP, (gi + 1) * POOL_GROUP)
        tot = p_in[:, sl]
        for back in range(1, win):
            tot = tot + rows([pin_ext[s, POOL_PAD - back:POOL_PAD - back + ls, sl] for s in range(n_sub)])
        cnt = jnp.minimum(win, pos + 1).astype(F32)
        d = tot / cnt - p_in[:, sl]
        mixed = _dot(d.astype(BF16), poolw_ref[gi])
        yc_parts.append(mixed * pscale_ref[:, sl] * _silu(gate_c[:, sl]))
    y_c = jnp.concatenate(yc_parts, axis=-1)
    zc = _dot(y_c.astype(BF16), liftc_ref[...])

    m_b = _dot(h, w_ref[:, B_MB:B_MC])
    m_c = _dot(h, w_ref[:, B_MC:B_END])
    zbc_ref[...] = jax.nn.sigmoid(m_b) * zb + jax.nn.sigmoid(m_c) * zc

    @pl.when(j == pl.num_programs(1) - 1)
    def _():
        cstate_ref[...] = cin_ext[:, ls:ls + CONV_PAD, :]
        pstate_ref[...] = pin_ext[:, ls:ls + POOL_PAD, :]


def _proj_b(x, g, w_b, conv_w, conv_b, pool_w, pool_scale, lift_b, lift_c, chist, phist, *, seq_len, tm, pos0):
    m = x.shape[0]
    n_seq = m // seq_len
    n_sub, ls = (1, tm) if tm <= seq_len else (tm // seq_len, seq_len)
    n_t = seq_len // ls
    out_shape = (
        jax.ShapeDtypeStruct((m, D_MODEL), F32),
        jax.ShapeDtypeStruct((n_seq, CONV_PAD, CONV_W), F32),
        jax.ShapeDtypeStruct((n_seq, POOL_PAD, POOL_W), F32),
    )
    hist = lambda r, w: pl.BlockSpec((n_sub, r, w), lambda s, j: (s, 0, 0))
    return pl.pallas_call(
        functools.partial(_proj_b_kernel, n_sub=n_sub, ls=ls, pos0=pos0),
        grid=(n_seq // n_sub, n_t),
        in_specs=[pl.BlockSpec((tm, D_MODEL), lambda s, j: (s * n_t + j, 0)),
                  _const_spec((1, D_MODEL)), _const_spec((D_MODEL, B_END)),
                  _const_spec((CONV_K, CONV_W)), _const_spec((1, CONV_W)),
                  _const_spec((len(POOL_WINDOWS), POOL_GROUP, POOL_GROUP)), _const_spec((1, POOL_W)),
                  _const_spec((CONV_W, D_MODEL)), _const_spec((POOL_W, D_MODEL)),
                  hist(CONV_PAD, CONV_W), hist(POOL_PAD, POOL_W)],
        out_specs=(pl.BlockSpec((tm, D_MODEL), lambda s, j: (s * n_t + j, 0)),
                   hist(CONV_PAD, CONV_W), hist(POOL_PAD, POOL_W)),
        out_shape=out_shape,
        scratch_shapes=[pltpu.VMEM((n_sub, ls + CONV_PAD, CONV_W), F32),
                        pltpu.VMEM((n_sub, ls + POOL_PAD, POOL_W), F32)],
        compiler_params=pltpu.CompilerParams(dimension_semantics=("arbitrary", "arbitrary"),
                                             vmem_limit_bytes=VMEM_LIMIT),
        name="proj_b",
    )(x, g, w_b, conv_w, conv_b, pool_w, pool_scale, lift_b, lift_c, chist, phist)


def _merge_kernel(x_ref, g_ref, wma_ref, attn_ref, ga_ref, lifta_ref, zbc_ref, wout_ref, fg_ref,
                  out_ref, *, final):
    x = x_ref[...]
    h = _rms_h(x, g_ref[...])
    y_a = attn_ref[...] * ga_ref[...].astype(F32)
    z = jax.nn.sigmoid(_dot(h, wma_ref[...])) * _dot(y_a.astype(BF16), lifta_ref[...]) + zbc_ref[...]
    out = x + _dot(z.astype(BF16), wout_ref[...])
    if final:
        out = out * lax.rsqrt(jnp.mean(out * out, axis=-1, keepdims=True) + EPS) * fg_ref[...]
    out_ref[...] = out


def _merge(x, g, w_ma, attn, ga, lift_a, zbc, w_out, final_g, *, tm, final):
    m = x.shape[0]
    row = lambda w: pl.BlockSpec((tm, w), lambda i: (i, 0))
    return pl.pallas_call(
        functools.partial(_merge_kernel, final=final),
        grid=(m // tm,),
        in_specs=[row(D_MODEL), _const_spec((1, D_MODEL)), _const_spec((D_MODEL, D_MODEL)),
                  row(ATTN_W), row(ATTN_W), _const_spec((ATTN_W, D_MODEL)), row(D_MODEL),
                  _const_spec((D_MODEL, D_MODEL)), _const_spec((1, D_MODEL))],
        out_specs=row(D_MODEL),
        out_shape=jax.ShapeDtypeStruct((m, D_MODEL), F32),
        compiler_params=pltpu.CompilerParams(dimension_semantics=("arbitrary",),
                                             vmem_limit_bytes=VMEM_LIMIT),
        name="merge",
    )(x, g, w_ma, attn, ga, lift_a, zbc, w_out, final_g)


def _rope_tables(pos):
    def tab(half, reps):
        inv = ROPE_THETA ** (-jnp.arange(half, dtype=F32) / half)
        ang = pos.astype(F32)[:, None] * inv[None, :]
        cos, sin = jnp.cos(ang), jnp.sin(ang)
        return (jnp.tile(jnp.concatenate([cos, cos], axis=-1), (1, reps)),
                jnp.tile(jnp.concatenate([-sin, sin], axis=-1), (1, reps)))
    c128, s128 = tab(HEAD_DIM // 2, 1)
    c64, s64 = tab(IDX_DIM // 2, LANES // IDX_DIM)
    return c128, s128, c64, s64


def _layer(x, tabs, lw, hist, cache, *, n_seq, seq_len, pos0, topk, tm_tok, tm_seq, tq, final, final_g):
    m = n_seq * seq_len
    q, k, v, kb, vb, iq, ik, ik2, iw, ga = _proj_a(x, lw["g"], lw["w_a"], tabs, tm_tok)

    def seq(a):
        return a.reshape(n_seq, seq_len, a.shape[-1])

    iwt = jnp.swapaxes(seq(iw), 1, 2)
    q3, iq3, kb3, vb3, ik23 = seq(q), seq(iq), seq(kb), seq(vb), seq(ik2)
    if cache is not None:
        ck, cv, cik, layer = cache
        past = ck.shape[2]
        l_pad = past + seq_len + (-(past + seq_len) % KEY_BLOCK)
        kb3 = lax.dynamic_update_slice(_pack_cache(ck, layer, l_pad), kb3, (0, past, 0))
        vb3 = lax.dynamic_update_slice(_pack_cache(cv, layer, l_pad), vb3, (0, past, 0))
        ik23 = lax.dynamic_update_slice(_pack_idx_cache(cik, layer, l_pad), ik23, (0, past, 0))
    n_pack = tq // seq_len if seq_len < tq else 1
    assert tq == n_pack * min(seq_len, tq) and n_seq % n_pack == 0
    if n_pack > 1:
        iwt = jnp.swapaxes(iwt.reshape(n_seq // n_pack, n_pack, N_IDX_HEADS, seq_len), 1, 2)
        iwt = iwt.reshape(n_seq // n_pack, N_IDX_HEADS, tq)
    attn = _attend(q3, iq3, iwt, kb3, vb3, ik23, tq=tq, n_pack=n_pack, tq_real=min(seq_len, tq), q0=pos0, topk=topk)
    attn = attn.reshape(m, ATTN_W)

    zbc, cstate, pstate = _proj_b(x, lw["g"], lw["w_b"], lw["conv_w"], lw["conv_b"], lw["pool_w"],
                                  lw["pool_scale"], lw["lift_b"], lw["lift_c"], hist[0], hist[1],
                                  seq_len=seq_len, tm=tm_seq, pos0=pos0)
    outs = _merge(x, lw["g"], lw["w_ma"], attn, ga, lw["lift_a"], zbc, lw["w_out"], final_g,
                  tm=tm_tok, final=final)
    states = (k.reshape(n_seq, seq_len, N_KV_HEADS, HEAD_DIM), v.reshape(n_seq, seq_len, N_KV_HEADS, HEAD_DIM),
              ik.reshape(n_seq, seq_len, IDX_DIM), cstate[:, CONV_PAD - (CONV_K - 1):], pstate[:, POOL_PAD - POOL_HIST:])
    return outs, states


def kernel(x_prompt, x_sample, cache_k, cache_v, cache_idx_k, state_conv, state_pool, norm_g, w_in, conv_w,
           conv_b, pool_w, pool_scale, lift_a, lift_b, lift_c, w_out, final_norm_g):
    batch, seq, _ = x_prompt.shape
    dec_batch, dec_seq, _ = x_sample.shape
    depth = w_in.shape[0]
    past_len = cache_k.shape[2]
    topk_prompt = min(MAX_TOPK, seq // 4)
    topk_sample = min(MAX_TOPK, (past_len + dec_seq) // 4)

    tm_p, tm_s = 256, 256
    tabs_p = _rope_tables(jnp.arange(seq, dtype=jnp.int32))
    tabs_s = _rope_tables(past_len + (jnp.arange(tm_s, dtype=jnp.int32) % dec_seq))
    final_g = final_norm_g.reshape(1, D_MODEL)

    hp = x_prompt.reshape(batch * seq, D_MODEL)
    hs = x_sample.reshape(dec_batch * dec_seq, D_MODEL)
    zero_hist = (jnp.zeros((batch, CONV_PAD, CONV_W), F32), jnp.zeros((batch, POOL_PAD, POOL_W), F32))
    p_states, s_states = [], []
    for l in range(depth):
        w_a, w_b, w_ma = _prep_weights(w_in, l)
        lw = {
            "g": norm_g[l].reshape(1, D_MODEL),
            "w_a": w_a, "w_b": w_b, "w_ma": w_ma,
            "conv_w": conv_w[l], "conv_b": conv_b[l].reshape(1, CONV_W),
            "pool_w": pool_w[l].astype(BF16), "pool_scale": pool_scale[l].reshape(1, POOL_W),
            "lift_a": lift_a[l].astype(BF16), "lift_b": lift_b[l].astype(BF16), "lift_c": lift_c[l].astype(BF16),
            "w_out": w_out[l].astype(BF16),
        }
        final = l == depth - 1
        outs_p, st_p = _layer(hp, tabs_p, lw, zero_hist, None, n_seq=batch, seq_len=seq, pos0=0,
                              topk=topk_prompt, tm_tok=tm_p, tm_seq=tm_p, tq=Q_TILE_PROMPT, final=final,
                              final_g=final_g)
        cache = (cache_k, cache_v, cache_idx_k, l)
        hist_s = (jnp.pad(state_conv[l], ((0, 0), (CONV_PAD - (CONV_K - 1), 0), (0, 0))),
                  jnp.pad(state_pool[l], ((0, 0), (POOL_PAD - POOL_HIST, 0), (0, 0))))
        outs_s, st_s = _layer(hs, tabs_s, lw, hist_s, cache, n_seq=dec_batch, seq_len=dec_seq, pos0=past_len,
                              topk=topk_sample, tm_tok=tm_s, tm_seq=tm_s, tq=Q_TILE_SAMPLE, final=final,
                              final_g=final_g)
        hp, hs = outs_p, outs_s
        p_states.append(st_p)
        s_states.append(st_s)

    stack = lambda sts, i: jnp.stack([st[i] for st in sts])
    return (hp.reshape(batch, seq, D_MODEL), hs.reshape(dec_batch, dec_seq, D_MODEL),
            stack(p_states, 0), stack(p_states, 1), stack(p_states, 2), stack(p_states, 3), stack(p_states, 4),
            stack(s_states, 0), stack(s_states, 1), stack(s_states, 2), stack(s_states, 3), stack(s_states, 4))
```

```python
import functools

import jax
import jax.numpy as jnp
from jax import lax
from jax.experimental import pallas as pl
from jax.experimental.pallas import tpu as pltpu

D_MODEL = 2048
CHUNK = 64
N_HEADS = 8
HEAD_DIM = 128
N_KV_HEADS = 2
GROUP = N_HEADS // N_KV_HEADS
ATTN_W = N_HEADS * HEAD_DIM
KV_W = N_KV_HEADS * HEAD_DIM
N_IDX_HEADS = 16
IDX_DIM = 64
IDX_Q_W = N_IDX_HEADS * IDX_DIM
MAX_TOPK = 256
CONV_W = 512
CONV_K = 3
POOL_W = 512
POOL_WINDOWS = (2, 4, 8, 16)
POOL_GROUP = 128
POOL_HIST = 15
ROPE_THETA = 10000.0
EPS = 1e-6

LANES = 128
CONV_PAD = 8
POOL_PAD = 16
KEY_BLOCK = 512
PREP_ROWS = 256
SCORE_GROUP = 4
PASS_GROUP = 4
Q_TILE_PROMPT = 256
Q_TILE_SAMPLE = 128
LOGIT_LANES = 1024
NEG_BIAS = -1e30
COUNT_CHAINS = 4
BISECT_UNROLL = 4
BISECT_UNCHECKED = 4
LOG2E = 1.4426950408889634
F32_LOWEST = -3.0e38
VMEM_LIMIT = 56 * 1024 * 1024

F32 = jnp.float32
BF16 = jnp.bfloat16

_SIZES = (ATTN_W, KV_W, KV_W, IDX_Q_W, IDX_DIM, N_IDX_HEADS, ATTN_W,
          CONV_W, CONV_W, CONV_W, CONV_W, POOL_W, POOL_W, D_MODEL, D_MODEL, D_MODEL)
_OFFS = [0]
for _s in _SIZES:
    _OFFS.append(_OFFS[-1] + _s)
(O_Q, O_K, O_V, O_IQ, O_IK, O_IW, O_GA, O_U, O_BG, O_CG, O_GB, O_PIN, O_GC, O_MA, O_MB, O_MC, O_END) = _OFFS

A_Q, A_K, A_V, A_IQ, A_IKW, A_GA, A_END = 0, 1024, 1280, 1536, 2560, 2688, 3712
B_U, B_BG, B_CG, B_GB, B_PIN, B_GC, B_MB, B_MC, B_END = 0, 512, 1024, 1536, 2048, 2560, 3072, 5120, 7168


def _dot(a, b):
    return jnp.dot(a, b, preferred_element_type=F32)


def _dot_nt(a, b):
    return lax.dot_general(a, b, (((1,), (1,)), ((), ())), preferred_element_type=F32)


def _rms_h(x, g):
    h = x * lax.rsqrt(jnp.mean(x * x, axis=-1, keepdims=True) + EPS) * g
    return h.astype(BF16)


def _silu(x):
    return x * jax.nn.sigmoid(x)


def _const_spec(shape):
    nd = len(shape)
    return pl.BlockSpec(shape, lambda *_: (0,) * nd, pipeline_mode=pl.Buffered(1))


def _prep_weights_kernel(w_ref, wa_ref, wb_ref, wma_ref):
    cols = lambda a, b: w_ref[0, :, a:b].astype(BF16)
    ikw_end = A_IKW + IDX_DIM + N_IDX_HEADS
    wa_ref[:, A_Q:ikw_end] = cols(O_Q, O_GA)
    wa_ref[:, ikw_end:A_GA] = jnp.zeros((wa_ref.shape[0], A_GA - ikw_end), BF16)
    wa_ref[:, A_GA:A_END] = cols(O_GA, O_U)
    wb_ref[:, B_U:B_MB] = cols(O_U, O_MA)
    wb_ref[:, B_MB:B_END] = cols(O_MB, O_END)
    wma_ref[...] = cols(O_MA, O_MB)


def _prep_weights(w_in, layer):
    d, in_w = w_in.shape[1:]
    rows = PREP_ROWS
    slab = lambda w: pl.BlockSpec((rows, w), lambda i: (i, 0))
    return pl.pallas_call(
        _prep_weights_kernel,
        grid=(d // rows,),
        in_specs=[pl.BlockSpec((1, rows, in_w), lambda i: (layer, i, 0))],
        out_specs=(slab(A_END), slab(B_END), slab(D_MODEL)),
        out_shape=(jax.ShapeDtypeStruct((d, A_END), BF16), jax.ShapeDtypeStruct((d, B_END), BF16),
                   jax.ShapeDtypeStruct((d, D_MODEL), BF16)),
        compiler_params=pltpu.CompilerParams(dimension_semantics=("arbitrary",), vmem_limit_bytes=VMEM_LIMIT),
        name="prep_weights",
    )(w_in)


def _proj_a_kernel(x_ref, g_ref, w_ref, c128_ref, s128_ref, c64_ref, s64_ref,
                   q_ref, k_ref, v_ref, kb_ref, vb_ref, iq_ref, ik_ref, ik2_ref, iw_ref, ga_ref):
    h = _rms_h(x_ref[...], g_ref[...])
    c128 = c128_ref[...]
    s128 = s128_ref[...]
    c64 = c64_ref[...]
    s64 = s64_ref[...]
    lane = lax.broadcasted_iota(jnp.int32, (1, LANES), 1)
    first_half64 = (lane % IDX_DIM) < (IDX_DIM // 2)

    def rope128(y):
        return y * c128 + pltpu.roll(y, HEAD_DIM // 2, 1) * s128

    def rope64(y):
        partner = jnp.where(first_half64, pltpu.roll(y, LANES - IDX_DIM // 2, 1),
                            pltpu.roll(y, IDX_DIM // 2, 1))
        return y * c64 + partner * s64

    y = _dot(h, w_ref[:, A_Q:A_K])
    for hd in range(N_HEADS):
        sl = slice(hd * HEAD_DIM, (hd + 1) * HEAD_DIM)
        q_ref[:, sl] = (rope128(y[:, sl]) * (HEAD_DIM ** -0.5 * LOG2E)).astype(BF16)

    y = _dot(h, w_ref[:, A_K:A_IQ])
    for hd in range(N_KV_HEADS):
        sl = slice(hd * HEAD_DIM, (hd + 1) * HEAD_DIM)
        kr = rope128(y[:, sl])
        rows_hd = pl.ds(hd, y.shape[0], stride=N_KV_HEADS)
        k_ref[rows_hd, :] = kr
        kb_ref[:, sl] = kr.astype(BF16)
        v_ref[rows_hd, :] = y[:, KV_W + hd * HEAD_DIM:KV_W + (hd + 1) * HEAD_DIM]
    vb_ref[...] = y[:, KV_W:].astype(BF16)

    y = _dot(h, w_ref[:, A_IQ:A_IKW])
    for c in range(IDX_Q_W // LANES):
        sl = slice(c * LANES, (c + 1) * LANES)
        iq_ref[:, sl] = rope64(y[:, sl]).astype(BF16)

    y = _dot(h, w_ref[:, A_IKW:A_GA])
    ikr = rope64(y)
    ik_ref[...] = ikr[:, :IDX_DIM]
    ikz = jnp.where(lane < IDX_DIM, ikr, 0.0)
    ik2_ref[:, :LANES] = ikz.astype(BF16)
    ik2_ref[:, LANES:] = pltpu.roll(ikz, IDX_DIM, 1).astype(BF16)
    iw_ref[...] = y[:, IDX_DIM:IDX_DIM + N_IDX_HEADS] * ((IDX_DIM ** -0.5) * (N_IDX_HEADS ** -0.5))

    ga_ref[...] = _silu(_dot(h, w_ref[:, A_GA:A_END])).astype(BF16)


def _proj_a(x, g, w_a, tabs, tm):
    m = x.shape[0]
    c128, s128, c64, s64 = tabs
    n_pt = c128.shape[0] // tm
    row = lambda w: pl.BlockSpec((tm, w), lambda i: (i, 0))
    tab = pl.BlockSpec((tm, LANES), lambda i: (i % n_pt, 0))
    out_shape = (
        jax.ShapeDtypeStruct((m, ATTN_W), BF16),
        jax.ShapeDtypeStruct((m * N_KV_HEADS, HEAD_DIM), F32),
        jax.ShapeDtypeStruct((m * N_KV_HEADS, HEAD_DIM), F32),
        jax.ShapeDtypeStruct((m, KV_W), BF16),
        jax.ShapeDtypeStruct((m, KV_W), BF16),
        jax.ShapeDtypeStruct((m, IDX_Q_W), BF16),
        jax.ShapeDtypeStruct((m, IDX_DIM), F32),
        jax.ShapeDtypeStruct((m, 2 * LANES), BF16),
        jax.ShapeDtypeStruct((m, N_IDX_HEADS), F32),
        jax.ShapeDtypeStruct((m, ATTN_W), BF16),
    )
    kv_heads = pl.BlockSpec((tm * N_KV_HEADS, HEAD_DIM), lambda i: (i, 0))
    out_specs = (row(ATTN_W), kv_heads, kv_heads, row(KV_W), row(KV_W), row(IDX_Q_W),
                 row(IDX_DIM), row(2 * LANES), row(N_IDX_HEADS), row(ATTN_W))
    return pl.pallas_call(
        _proj_a_kernel,
        grid=(m // tm,),
        in_specs=[row(D_MODEL), _const_spec((1, D_MODEL)), _const_spec((D_MODEL, A_END)),
                  tab, tab, tab, tab],
        out_specs=out_specs,
        out_shape=out_shape,
        compiler_params=pltpu.CompilerParams(dimension_semantics=("arbitrary",),
                                             vmem_limit_bytes=VMEM_LIMIT),
        name="proj_a",
    )(x, g, w_a, c128, s128, c64, s64)


def _grouped_blocks(n_blocks, trip, group):
    def full(t, carry):
        trip(group * t, group)
        return carry

    n_full = n_blocks // group
    lax.fori_loop(0, n_full, full, 0)
    done = n_full * group
    size = group // 2
    while size >= 1:
        has = ((n_blocks - done) // size) % 2 == 1

        @pl.when(has)
        def _(done=done, size=size):
            trip(done, size)

        done = done + jnp.where(has, size, 0)
        size //= 2


def _attend_kernel(q_ref, iq_ref, iwt_ref, kb_ref, vb_ref, ik2_ref, o_ref,
                   st_ref, sbuf_ref, state_ref, acc_ref, stat_ref,
                   *, tq, n_pack, tq_real, q0, topk):
    kb_sz = KEY_BLOCK
    tqe = tq // n_pack
    i = pl.program_id(1)
    qpos0 = q0 + i * tq_real
    n_keys = ((qpos0 + tq_real - 1) // CHUNK + 1) * CHUNK
    nkb = (n_keys + kb_sz - 1) // kb_sz

    lane_q = lax.broadcasted_iota(jnp.int32, (1, tq), 1)
    lane_e = lane_q % tqe
    n_adm_i = ((qpos0 + lane_e) // CHUNK + 1) * CHUNK
    iw = iwt_ref[0]

    def query_rows(ref, cols):
        if n_pack == 1:
            return ref[0, :, cols]
        zero = jnp.zeros((tqe, cols.stop - cols.start), ref.dtype)
        return jnp.concatenate(
            [jnp.concatenate([ref[e, :, cols] if c == e else zero for c in range(n_pack)], axis=1)
             for e in range(n_pack)], axis=0)

    def key_rows(ref, r0, cols):
        if n_pack == 1:
            return ref[0, pl.ds(r0, kb_sz), cols]
        return jnp.concatenate([ref[e, pl.ds(r0, kb_sz), cols] for e in range(n_pack)], axis=1)

    state_ref[0:1, :] = jnp.full((1, tq), jnp.inf, F32)
    state_ref[1:2, :] = jnp.full((1, tq), -jnp.inf, F32)

    def score_trip(kb0, n_blk):
        mn, mx = state_ref[0:1, :], state_ref[1:2, :]
        for kb in [kb0 + d for d in range(n_blk)]:
            r0 = pl.multiple_of(kb * kb_sz, kb_sz)
            ik_even = key_rows(ik2_ref, r0, slice(0, LANES))
            ik_odd = key_rows(ik2_ref, r0, slice(LANES, 2 * LANES))
            acc = jnp.zeros((kb_sz, tq), F32)
            for p in range(N_IDX_HEADS // 2):
                iq_pair = query_rows(iq_ref, slice(p * LANES, (p + 1) * LANES))
                d0 = _dot_nt(ik_even, iq_pair)
                d1 = _dot_nt(ik_odd, iq_pair)
                acc = acc + jnp.maximum(d0, 0.0) * iw[2 * p:2 * p + 1]
                acc = acc + jnp.maximum(d1, 0.0) * iw[2 * p + 1:2 * p + 2]
            key = r0 + lax.broadcasted_iota(jnp.int32, (kb_sz, 1), 0)
            masked = jnp.where(key < n_adm_i, acc, -jnp.inf)
            st_ref[pl.ds(r0, kb_sz), :] = masked
            mx = jnp.maximum(mx, jnp.max(masked, axis=0, keepdims=True))
            mn = jnp.minimum(mn, jnp.min(acc, axis=0, keepdims=True))
        state_ref[0:1, :] = mn
        state_ref[1:2, :] = mx

    _grouped_blocks(nkb, score_trip, SCORE_GROUP)
    mn, mx = state_ref[0:1, :], state_ref[1:2, :]

    n_adm = n_adm_i.astype(F32)
    topk_f = float(topk)
    active0 = jnp.logical_and(n_adm > topk_f, lane_e < tq_real).astype(F32)
    state_ref[0:1, :] = mn
    state_ref[1:2, :] = mx + jnp.maximum(jnp.abs(mx), 1e-30)
    state_ref[2:3, :] = jnp.full((1, tq), F32_LOWEST, F32)
    state_ref[3:4, :] = active0
    state_ref[4:5, :] = n_adm
    state_ref[5:6, :] = jnp.zeros((1, tq), F32)
    state_ref[6:7, :] = jnp.zeros((1, tq), F32)

    def count_ge(mid):
        sub = kb_sz // COUNT_CHAINS

        def cnt_blk(kb, accs):
            r0 = pl.multiple_of(kb * kb_sz, kb_sz)
            out = []
            for c in range(COUNT_CHAINS):
                ge = (st_ref[pl.ds(r0 + c * sub, sub), :] >= mid).astype(F32)
                out.append(accs[c] + jnp.sum(ge.reshape(sub // 8, 8, tq), axis=0))
            return tuple(out)

        accs = lax.fori_loop(0, nkb, cnt_blk, tuple(jnp.zeros((8, tq), F32) for _ in range(COUNT_CHAINS)))
        return jnp.sum(sum(accs), axis=0, keepdims=True)

    def bisect_step(st):
        lo, hi, thr, active, clo, tie, pend = st
        mid = 0.5 * lo + 0.5 * hi
        stuck = jnp.logical_or(mid <= lo, mid >= hi)
        cnt = count_ge(mid)
        ge_k = cnt >= topk_f
        exact = cnt == topk_f
        moving = jnp.logical_and(active, jnp.logical_not(stuck))
        ended = jnp.logical_and(active, stuck)
        thr = jnp.where(ended, lo, jnp.where(jnp.logical_and(moving, exact), mid, thr))
        tie = jnp.logical_or(tie, jnp.logical_and(ended, clo > topk_f))
        up = jnp.logical_and(moving, ge_k)
        down = jnp.logical_and(moving, jnp.logical_not(ge_k))
        near = jnp.logical_and(moving, cnt == topk_f - 1.0)
        done = jnp.logical_or(exact, near)
        return (jnp.where(up, mid, lo), jnp.where(down, mid, hi), thr,
                jnp.logical_and(moving, jnp.logical_not(done)), jnp.where(up, cnt, clo), tie,
                jnp.logical_or(pend, near))

    def bisect_steps():
        st = (state_ref[0:1, :], state_ref[1:2, :], state_ref[2:3, :], state_ref[3:4, :] > 0.0,
              state_ref[4:5, :], state_ref[5:6, :] > 0.0, state_ref[6:7, :] > 0.0)
        for _ in range(BISECT_UNROLL):
            st = bisect_step(st)
        active_f = st[3].astype(F32)
        state_ref[0:1, :] = st[0]
        state_ref[1:2, :] = st[1]
        state_ref[2:3, :] = st[2]
        state_ref[3:4, :] = active_f
        state_ref[4:5, :] = st[4]
        state_ref[5:6, :] = st[5].astype(F32)
        state_ref[6:7, :] = st[6].astype(F32)
        return active_f

    def unchecked(i, carry):
        bisect_steps()
        return carry

    lax.fori_loop(0, BISECT_UNCHECKED, unchecked, 0)
    lax.while_loop(lambda c: jnp.logical_and(c[0] > 0.0, c[1] < 128),
                   lambda c: (jnp.max(bisect_steps()), c[1] + 1),
                   (jnp.max(state_ref[3:4, :]), jnp.int32(0)))

    @pl.when(jnp.max(state_ref[6:7, :]) > 0.0)
    def _():
        pend = state_ref[6:7, :] > 0.0
        hi = state_ref[1:2, :]

        def below_blk(kb, acc):
            r0 = pl.multiple_of(kb * kb_sz, kb_sz)
            blk = st_ref[pl.ds(r0, kb_sz), :]
            below = jnp.where(blk < hi, blk, -jnp.inf)
            return jnp.maximum(acc, jnp.max(below.reshape(COUNT_CHAINS, kb_sz // (8 * COUNT_CHAINS), 8, tq), axis=1))

        top = lax.fori_loop(0, nkb, below_blk, jnp.full((COUNT_CHAINS, 8, tq), -jnp.inf, F32))
        thr_p = jnp.where(pend, jnp.max(jnp.max(top, axis=0), axis=0, keepdims=True), state_ref[2:3, :])
        state_ref[2:3, :] = thr_p
        dup = jnp.logical_and(pend, count_ge(thr_p) > topk_f)
        state_ref[5:6, :] = jnp.maximum(state_ref[5:6, :], dup.astype(F32))

    thr = state_ref[2:3, :]

    @pl.when(jnp.max(state_ref[5:6, :]) > 0.0)
    def _():
        tie_row = state_ref[5:6, :] > 0.0

        def gt_blk(kb, acc):
            r0 = pl.multiple_of(kb * kb_sz, kb_sz)
            return acc + jnp.sum((st_ref[pl.ds(r0, kb_sz), :] > thr).astype(F32), axis=0, keepdims=True)

        need = topk_f - lax.fori_loop(0, nkb, gt_blk, jnp.zeros((1, tq), F32))
        tri = (lax.broadcasted_iota(jnp.int32, (kb_sz, kb_sz), 0)
               >= lax.broadcasted_iota(jnp.int32, (kb_sz, kb_sz), 1)).astype(BF16)

        def fix_blk(kb, run):
            r0 = pl.multiple_of(kb * kb_sz, kb_sz)
            blk = st_ref[pl.ds(r0, kb_sz), :]
            eq = jnp.logical_and(blk == thr, tie_row)
            eq_f = eq.astype(F32)
            rank = _dot(tri, eq_f.astype(BF16)) - eq_f + run
            st_ref[pl.ds(r0, kb_sz), :] = jnp.where(jnp.logical_and(eq, rank >= need), -jnp.inf, blk)
            return run + jnp.sum(eq_f, axis=0, keepdims=True)

        lax.fori_loop(0, nkb, fix_blk, jnp.zeros((1, tq), F32))

    red = lambda a, op: op(a.reshape(2, kb_sz // 16, 8, a.shape[-1]), axis=1)
    gw = GROUP * tq
    heads_per_trip = LOGIT_LANES // gw
    entry_of_lane = (lax.broadcasted_iota(jnp.int32, (1, gw), 1) % tq) // tqe
    for j0 in range(0, N_KV_HEADS, heads_per_trip):
        trip_heads = range(j0, j0 + heads_per_trip)
        qs = {j: jnp.concatenate(
            [query_rows(q_ref, slice((GROUP * j + g) * HEAD_DIM, (GROUP * j + g + 1) * HEAD_DIM))
             for g in range(GROUP)], axis=0) for j in trip_heads}

        def qk_trip(kb0, n_blk):
            m8 = stat_ref[...]
            for kb in [kb0 + d for d in range(n_blk)]:
                r0 = pl.multiple_of(kb * kb_sz, kb_sz)
                bias = jnp.where(st_ref[pl.ds(r0, kb_sz), :] >= thr, 0.0, NEG_BIAS)
                parts = []
                for j in trip_heads:
                    s = _dot_nt(key_rows(kb_ref, r0, slice(j * HEAD_DIM, (j + 1) * HEAD_DIM)), qs[j])
                    for g in range(GROUP):
                        sg = s[:, g * tq:(g + 1) * tq] + bias
                        c0 = (j - j0) * gw + g * tq
                        sbuf_ref[kb, :, c0:c0 + tq] = sg
                        parts.append(red(sg, jnp.max))
                m8 = jnp.maximum(m8, jnp.concatenate(parts, axis=-1))
            stat_ref[...] = m8

        stat_ref[...] = jnp.full(stat_ref.shape, NEG_BIAS, F32)
        _grouped_blocks(nkb, qk_trip, PASS_GROUP)
        m = jnp.max(jnp.max(stat_ref[...], axis=0), axis=0, keepdims=True)
        stat_ref[...] = jnp.zeros(stat_ref.shape, F32)
        acc_ref[...] = jnp.zeros(acc_ref.shape, F32)

        def pv_trip(kb0, n_blk):
            sums = []
            for j in trip_heads:
                cols = slice((j - j0) * gw, (j - j0 + 1) * gw)
                pv, psum = 0.0, 0.0
                for kb in [kb0 + d for d in range(n_blk)]:
                    r0 = pl.multiple_of(kb * kb_sz, kb_sz)
                    p = jnp.exp2(sbuf_ref[kb, :, cols] - m[:, cols])
                    p16 = p.astype(BF16)
                    t = None
                    for e in range(n_pack):
                        te = lax.dot_general(vb_ref[e, pl.ds(r0, kb_sz), j * HEAD_DIM:(j + 1) * HEAD_DIM], p16,
                                             (((0,), (0,)), ((), ())), preferred_element_type=F32)
                        t = te if t is None else jnp.where(entry_of_lane == e, te, t)
                    pv = pv + t
                    psum = psum + red(p, jnp.sum)
                acc_ref[:, cols] += pv
                sums.append(psum)
            stat_ref[...] += jnp.concatenate(sums, axis=-1)

        _grouped_blocks(nkb, pv_trip, PASS_GROUP)
        o = acc_ref[...] / jnp.sum(jnp.sum(stat_ref[...], axis=0), axis=0, keepdims=True)
        for j in trip_heads:
            for g in range(GROUP):
                hd = GROUP * j + g
                c0 = (j - j0) * gw + g * tq
                o_t = o[:, c0:c0 + tq].T.astype(o_ref.dtype)
                for e in range(n_pack):
                    o_ref[e, :, hd * HEAD_DIM:(hd + 1) * HEAD_DIM] = o_t[e * tqe:(e + 1) * tqe]


def _attend(q, iq, iwt, kb, vb, ik2, *, tq, n_pack, tq_real, q0, topk):
    b, t_q, _ = q.shape
    l_keys = kb.shape[1]
    tqe = tq // n_pack
    assert b % n_pack == 0 and t_q % tqe == 0 and (n_pack == 1 or t_q == tqe) and l_keys % KEY_BLOCK == 0
    n_kb = l_keys // KEY_BLOCK
    qspec = pl.BlockSpec((n_pack, tqe, ATTN_W), lambda bi, i: (bi, i, 0))
    kspec = pl.BlockSpec((n_pack, l_keys, KV_W), lambda bi, i: (bi, 0, 0))
    return pl.pallas_call(
        functools.partial(_attend_kernel, tq=tq, n_pack=n_pack, tq_real=tq_real, q0=q0, topk=topk),
        grid=(b // n_pack, t_q // tqe),
        in_specs=[qspec, qspec, pl.BlockSpec((1, N_IDX_HEADS, tq), lambda bi, i: (bi, 0, i)),
                  kspec, kspec, kspec],
        out_specs=qspec,
        out_shape=jax.ShapeDtypeStruct((b, t_q, ATTN_W), F32),
        scratch_shapes=[
            pltpu.VMEM((l_keys, tq), F32),
            pltpu.VMEM((n_kb, KEY_BLOCK, LOGIT_LANES), F32),
            pltpu.VMEM((8, tq), F32),
            pltpu.VMEM((HEAD_DIM, LOGIT_LANES), F32),
            pltpu.VMEM((2, 8, LOGIT_LANES), F32),
        ],
        compiler_params=pltpu.CompilerParams(dimension_semantics=("arbitrary", "arbitrary"),
                                             vmem_limit_bytes=VMEM_LIMIT),
        name="attend",
    )(q, iq, iwt, kb, vb, ik2)


def _pack_cache_kernel(c_ref, o_ref, *, past):
    for hd in range(N_KV_HEADS):
        rows_hd = pl.ds(hd, past, stride=N_KV_HEADS)
        o_ref[0, 0:past, hd * HEAD_DIM:(hd + 1) * HEAD_DIM] = c_ref[0, 0, rows_hd, :].astype(BF16)
    o_ref[0, past:, :] = jnp.zeros((o_ref.shape[1] - past, KV_W), BF16)


def _pack_idx_cache_kernel(c_ref, e_ref, o_ref, *, past):
    o_ref[0, 0:past, :] = _dot(c_ref[0, 0].astype(BF16), e_ref[...]).astype(BF16)
    o_ref[0, past:, :] = jnp.zeros((o_ref.shape[1] - past, 2 * LANES), BF16)


def _pack_cache(c, layer, l_pad):
    depth, b, past = c.shape[:3]
    c = c.reshape(depth, b, past * N_KV_HEADS, HEAD_DIM)
    return pl.pallas_call(
        functools.partial(_pack_cache_kernel, past=past),
        grid=(b,),
        in_specs=[pl.BlockSpec((1, 1, past * N_KV_HEADS, HEAD_DIM), lambda bi: (layer, bi, 0, 0))],
        out_specs=pl.BlockSpec((1, l_pad, KV_W), lambda bi: (bi, 0, 0)),
        out_shape=jax.ShapeDtypeStruct((b, l_pad, KV_W), BF16),
        compiler_params=pltpu.CompilerParams(dimension_semantics=("arbitrary",), vmem_limit_bytes=VMEM_LIMIT),
        name="pack_cache",
    )(c)


def _pack_idx_cache(c, layer, l_pad):
    _, b, past, _ = c.shape
    eye = jnp.eye(IDX_DIM, dtype=BF16)
    zero = jnp.zeros((IDX_DIM, IDX_DIM), BF16)
    select = jnp.concatenate([eye, zero, zero, eye], axis=1)
    return pl.pallas_call(
        functools.partial(_pack_idx_cache_kernel, past=past),
        grid=(b,),
        in_specs=[pl.BlockSpec((1, 1, past, IDX_DIM), lambda bi: (layer, bi, 0, 0)),
                  pl.BlockSpec((IDX_DIM, 2 * LANES), lambda bi: (0, 0))],
        out_specs=pl.BlockSpec((1, l_pad, 2 * LANES), lambda bi: (bi, 0, 0)),
        out_shape=jax.ShapeDtypeStruct((b, l_pad, 2 * LANES), BF16),
        compiler_params=pltpu.CompilerParams(dimension_semantics=("arbitrary",), vmem_limit_bytes=VMEM_LIMIT),
        name="pack_idx_cache",
    )(c, select)


def _proj_b_kernel(x_ref, g_ref, w_ref, convw_ref, convb_ref, poolw_ref, pscale_ref, liftb_ref, liftc_ref,
                   chist_ref, phist_ref, zbc_ref, cstate_ref, pstate_ref, cin_ext, pin_ext,
                   *, n_sub, ls, pos0):
    j = pl.program_id(1)

    @pl.when(j == 0)
    def _():
        cin_ext[:, 0:CONV_PAD, :] = chist_ref[...]
        pin_ext[:, 0:POOL_PAD, :] = phist_ref[...]

    @pl.when(j > 0)
    def _():
        cin_ext[:, 0:CONV_PAD, :] = cin_ext[:, ls:ls + CONV_PAD, :]
        pin_ext[:, 0:POOL_PAD, :] = pin_ext[:, ls:ls + POOL_PAD, :]

    h = _rms_h(x_ref[...], g_ref[...])
    rows = lambda a: jnp.concatenate(a, axis=0) if n_sub > 1 else a[0]

    y = _dot(h, w_ref[:, B_U:B_PIN])
    u = y[:, 0:CONV_W]
    b_gate = y[:, CONV_W:2 * CONV_W]
    c_gate = y[:, 2 * CONV_W:3 * CONV_W]
    gate_b = y[:, 3 * CONV_W:4 * CONV_W]
    cin = c_gate * u
    for s in range(n_sub):
        cin_ext[s, CONV_PAD:CONV_PAD + ls, :] = cin[s * ls:(s + 1) * ls]
    conv = (rows([cin_ext[s, CONV_PAD - 2:CONV_PAD - 2 + ls, :] for s in range(n_sub)]) * convw_ref[0:1, :]
            + rows([cin_ext[s, CONV_PAD - 1:CONV_PAD - 1 + ls, :] for s in range(n_sub)]) * convw_ref[1:2, :]
            + cin * convw_ref[2:3, :] + convb_ref[...])
    y_b = b_gate * conv * _silu(gate_b)
    zb = _dot(y_b.astype(BF16), liftb_ref[...])

    y = _dot(h, w_ref[:, B_PIN:B_MB])
    p_in = y[:, 0:POOL_W]
    gate_c = y[:, POOL_W:2 * POOL_W]
    for s in range(n_sub):
        pin_ext[s, POOL_PAD:POOL_PAD + ls, :] = p_in[s * ls:(s + 1) * ls]
    pos = rows([pos0 + j * ls + lax.broadcasted_iota(jnp.int32, (ls, 1), 0)] * n_sub)
    yc_parts = []
    for gi, win in enumerate(POOL_WINDOWS):
        sl = slice(gi * POOL_GROUP, (gi + 1) * POOL_GROUP)
        tot = p_in[:, sl]
        for back in range(1, win):
            tot = tot + rows([pin_ext[s, POOL_PAD - back:POOL_PAD - back + ls, sl] for s in range(n_sub)])
        cnt = jnp.minimum(win, pos + 1).astype(F32)
        d = tot / cnt - p_in[:, sl]
        mixed = _dot(d.astype(BF16), poolw_ref[gi])
        yc_parts.append(mixed * pscale_ref[:, sl] * _silu(gate_c[:, sl]))
    y_c = jnp.concatenate(yc_parts, axis=-1)
    zc = _dot(y_c.astype(BF16), liftc_ref[...])

    m_b = _dot(h, w_ref[:, B_MB:B_MC])
    m_c = _dot(h, w_ref[:, B_MC:B_END])
    zbc_ref[...] = jax.nn.sigmoid(m_b) * zb + jax.nn.sigmoid(m_c) * zc

    @pl.when(j == pl.num_programs(1) - 1)
    def _():
        cstate_ref[...] = cin_ext[:, ls:ls + CONV_PAD, :]
        pstate_ref[...] = pin_ext[:, ls:ls + POOL_PAD, :]


def _proj_b(x, g, w_b, conv_w, conv_b, pool_w, pool_scale, lift_b, lift_c, chist, phist, *, seq_len, tm, pos0):
    m = x.shape[0]
    n_seq = m // seq_len
    n_sub, ls = (1, tm) if tm <= seq_len else (tm // seq_len, seq_len)
    n_t = seq_len // ls
    out_shape = (
        jax.ShapeDtypeStruct((m, D_MODEL), F32),
        jax.ShapeDtypeStruct((n_seq, CONV_PAD, CONV_W), F32),
        jax.ShapeDtypeStruct((n_seq, POOL_PAD, POOL_W), F32),
    )
    hist = lambda r, w: pl.BlockSpec((n_sub, r, w), lambda s, j: (s, 0, 0))
    return pl.pallas_call(
        functools.partial(_proj_b_kernel, n_sub=n_sub, ls=ls, pos0=pos0),
        grid=(n_seq // n_sub, n_t),
        in_specs=[pl.BlockSpec((tm, D_MODEL), lambda s, j: (s * n_t + j, 0)),
                  _const_spec((1, D_MODEL)), _const_spec((D_MODEL, B_END)),
                  _const_spec((CONV_K, CONV_W)), _const_spec((1, CONV_W)),
                  _const_spec((len(POOL_WINDOWS), POOL_GROUP, POOL_GROUP)), _const_spec((1, POOL_W)),
                  _const_spec((CONV_W, D_MODEL)), _const_spec((POOL_W, D_MODEL)),
                  hist(CONV_PAD, CONV_W), hist(POOL_PAD, POOL_W)],
        out_specs=(pl.BlockSpec((tm, D_MODEL), lambda s, j: (s * n_t + j, 0)),
                   hist(CONV_PAD, CONV_W), hist(POOL_PAD, POOL_W)),
        out_shape=out_shape,
        scratch_shapes=[pltpu.VMEM((n_sub, ls + CONV_PAD, CONV_W), F32),
                        pltpu.VMEM((n_sub, ls + POOL_PAD, POOL_W), F32)],
        compiler_params=pltpu.CompilerParams(dimension_semantics=("arbitrary", "arbitrary"),
                                             vmem_limit_bytes=VMEM_LIMIT),
        name="proj_b",
    )(x, g, w_b, conv_w, conv_b, pool_w, pool_scale, lift_b, lift_c, chist, phist)


def _merge_kernel(x_ref, g_ref, wma_ref, attn_ref, ga_ref, lifta_ref, zbc_ref, wout_ref, fg_ref,
                  out_ref, *, final):
    x = x_ref[...]
    h = _rms_h(x, g_ref[...])
    y_a = attn_ref[...] * ga_ref[...].astype(F32)
    z = jax.nn.sigmoid(_dot(h, wma_ref[...])) * _dot(y_a.astype(BF16), lifta_ref[...]) + zbc_ref[...]
    out = x + _dot(z.astype(BF16), wout_ref[...])
    if final:
        out = out * lax.rsqrt(jnp.mean(out * out, axis=-1, keepdims=True) + EPS) * fg_ref[...]
    out_ref[...] = out


def _merge(x, g, w_ma, attn, ga, lift_a, zbc, w_out, final_g, *, tm, final):
    m = x.shape[0]
    row = lambda w: pl.BlockSpec((tm, w), lambda i: (i, 0))
    return pl.pallas_call(
        functools.partial(_merge_kernel, final=final),
        grid=(m // tm,),
        in_specs=[row(D_MODEL), _const_spec((1, D_MODEL)), _const_spec((D_MODEL, D_MODEL)),
                  row(ATTN_W), row(ATTN_W), _const_spec((ATTN_W, D_MODEL)), row(D_MODEL),
                  _const_spec((D_MODEL, D_MODEL)), _const_spec((1, D_MODEL))],
        out_specs=row(D_MODEL),
        out_shape=jax.ShapeDtypeStruct((m, D_MODEL), F32),
        compiler_params=pltpu.CompilerParams(dimension_semantics=("arbitrary",),
                                             vmem_limit_bytes=VMEM_LIMIT),
        name="merge",
    )(x, g, w_ma, attn, ga, lift_a, zbc, w_out, final_g)


def _rope_tables(pos):
    def tab(half, reps):
        inv = ROPE_THETA ** (-jnp.arange(half, dtype=F32) / half)
        ang = pos.astype(F32)[:, None] * inv[None, :]
        cos, sin = jnp.cos(ang), jnp.sin(ang)
        return (jnp.tile(jnp.concatenate([cos, cos], axis=-1), (1, reps)),
                jnp.tile(jnp.concatenate([-sin, sin], axis=-1), (1, reps)))
    c128, s128 = tab(HEAD_DIM // 2, 1)
    c64, s64 = tab(IDX_DIM // 2, LANES // IDX_DIM)
    return c128, s128, c64, s64


def _layer(x, tabs, lw, hist, cache, *, n_seq, seq_len, pos0, topk, tm_tok, tm_seq, tq, final, final_g):
    m = n_seq * seq_len
    q, k, v, kb, vb, iq, ik, ik2, iw, ga = _proj_a(x, lw["g"], lw["w_a"], tabs, tm_tok)

    def seq(a):
        return a.reshape(n_seq, seq_len, a.shape[-1])

    iwt = jnp.swapaxes(seq(iw), 1, 2)
    q3, iq3, kb3, vb3, ik23 = seq(q), seq(iq), seq(kb), seq(vb), seq(ik2)
    if cache is not None:
        ck, cv, cik, layer = cache
        past = ck.shape[2]
        l_pad = past + seq_len + (-(past + seq_len) % KEY_BLOCK)
        kb3 = lax.dynamic_update_slice(_pack_cache(ck, layer, l_pad), kb3, (0, past, 0))
        vb3 = lax.dynamic_update_slice(_pack_cache(cv, layer, l_pad), vb3, (0, past, 0))
        ik23 = lax.dynamic_update_slice(_pack_idx_cache(cik, layer, l_pad), ik23, (0, past, 0))
    n_pack = tq // seq_len if seq_len < tq else 1
    assert tq == n_pack * min(seq_len, tq) and n_seq % n_pack == 0
    if n_pack > 1:
        iwt = jnp.swapaxes(iwt.reshape(n_seq // n_pack, n_pack, N_IDX_HEADS, seq_len), 1, 2)
        iwt = iwt.reshape(n_seq // n_pack, N_IDX_HEADS, tq)
    attn = _attend(q3, iq3, iwt, kb3, vb3, ik23, tq=tq, n_pack=n_pack, tq_real=min(seq_len, tq), q0=pos0, topk=topk)
    attn = attn.reshape(m, ATTN_W)

    zbc, cstate, pstate = _proj_b(x, lw["g"], lw["w_b"], lw["conv_w"], lw["conv_b"], lw["pool_w"],
                                  lw["pool_scale"], lw["lift_b"], lw["lift_c"], hist[0], hist[1],
                                  seq_len=seq_len, tm=tm_seq, pos0=pos0)
    outs = _merge(x, lw["g"], lw["w_ma"], attn, ga, lw["lift_a"], zbc, lw["w_out"], final_g,
                  tm=tm_tok, final=final)
    states = (k.reshape(n_seq, seq_len, N_KV_HEADS, HEAD_DIM), v.reshape(n_seq, seq_len, N_KV_HEADS, HEAD_DIM),
              ik.reshape(n_seq, seq_len, IDX_DIM), cstate[:, CONV_PAD - (CONV_K - 1):], pstate[:, POOL_PAD - POOL_HIST:])
    return outs, states


def kernel(x_prompt, x_sample, cache_k, cache_v, cache_idx_k, state_conv, state_pool, norm_g, w_in, conv_w,
           conv_b, pool_w, pool_scale, lift_a, lift_b, lift_c, w_out, final_norm_g):
    batch, seq, _ = x_prompt.shape
    dec_batch, dec_seq, _ = x_sample.shape
    depth = w_in.shape[0]
    past_len = cache_k.shape[2]
    topk_prompt = min(MAX_TOPK, seq // 4)
    topk_sample = min(MAX_TOPK, (past_len + dec_seq) // 4)

    tm_p, tm_s = 256, 256
    tabs_p = _rope_tables(jnp.arange(seq, dtype=jnp.int32))
    tabs_s = _rope_tables(past_len + (jnp.arange(tm_s, dtype=jnp.int32) % dec_seq))
    final_g = final_norm_g.reshape(1, D_MODEL)

    hp = x_prompt.reshape(batch * seq, D_MODEL)
    hs = x_sample.reshape(dec_batch * dec_seq, D_MODEL)
    zero_hist = (jnp.zeros((batch, CONV_PAD, CONV_W), F32), jnp.zeros((batch, POOL_PAD, POOL_W), F32))
    p_states, s_states = [], []
    for l in range(depth):
        w_a, w_b, w_ma = _prep_weights(w_in, l)
        lw = {
            "g": norm_g[l].reshape(1, D_MODEL),
            "w_a": w_a, "w_b": w_b, "w_ma": w_ma,
            "conv_w": conv_w[l], "conv_b": conv_b[l].reshape(1, CONV_W),
            "pool_w": pool_w[l].astype(BF16), "pool_scale": pool_scale[l].reshape(1, POOL_W),
            "lift_a": lift_a[l].astype(BF16), "lift_b": lift_b[l].astype(BF16), "lift_c": lift_c[l].astype(BF16),
            "w_out": w_out[l].astype(BF16),
        }
        final = l == depth - 1
        outs_p, st_p = _layer(hp, tabs_p, lw, zero_hist, None, n_seq=batch, seq_len=seq, pos0=0,
                              topk=topk_prompt, tm_tok=tm_p, tm_seq=tm_p, tq=Q_TILE_PROMPT, final=final,
                              final_g=final_g)
        cache = (cache_k, cache_v, cache_idx_k, l)
        hist_s = (jnp.pad(state_conv[l], ((0, 0), (CONV_PAD - (CONV_K - 1), 0), (0, 0))),
                  jnp.pad(state_pool[l], ((0, 0), (POOL_PAD - POOL_HIST, 0), (0, 0))))
        outs_s, st_s = _layer(hs, tabs_s, lw, hist_s, cache, n_seq=dec_batch, seq_len=dec_seq, pos0=past_len,
                              topk=topk_sample, tm_tok=tm_s, tm_seq=tm_s, tq=Q_TILE_SAMPLE, final=final,
                              final_g=final_g)
        hp, hs = outs_p, outs_s
        p_states.append(st_p)
        s_states.append(st_s)

    stack = lambda sts, i: jnp.stack([st[i] for st in sts])
    return (hp.reshape(batch, seq, D_MODEL), hs.reshape(dec_batch, dec_seq, D_MODEL),
            stack(p_states, 0), stack(p_states, 1), stack(p_states, 2), stack(p_states, 3), stack(p_states, 4),
            stack(s_states, 0), stack(s_states, 1), stack(s_states, 2), stack(s_states, 3), stack(s_states, 4))
```

```python
import functools

import jax
import jax.numpy as jnp
from jax import lax
from jax.experimental import pallas as pl
from jax.experimental.pallas import tpu as pltpu

D_MODEL = 2048
CHUNK = 64
N_HEADS = 8
HEAD_DIM = 128
N_KV_HEADS = 2
GROUP = N_HEADS // N_KV_HEADS
ATTN_W = N_HEADS * HEAD_DIM
KV_W = N_KV_HEADS * HEAD_DIM
N_IDX_HEADS = 16
IDX_DIM = 64
IDX_Q_W = N_IDX_HEADS * IDX_DIM
MAX_TOPK = 256
CONV_W = 512
CONV_K = 3
POOL_W = 512
POOL_WINDOWS = (2, 4, 8, 16)
POOL_GROUP = 128
POOL_HIST = 15
ROPE_THETA = 10000.0
EPS = 1e-6

LANES = 128
CONV_PAD = 8
POOL_PAD = 16
KEY_BLOCK = 512
PREP_ROWS = 256
SCORE_GROUP = 4
PASS_GROUP = 4
Q_TILE_PROMPT = 256
Q_TILE_SAMPLE = 128
LOGIT_LANES = 1024
NEG_BIAS = -1e30
COUNT_CHAINS = 4
BISECT_UNROLL = 2
BISECT_UNCHECKED = 8
LOG2E = 1.4426950408889634
F32_LOWEST = -3.0e38
VMEM_LIMIT = 56 * 1024 * 1024

F32 = jnp.float32
BF16 = jnp.bfloat16

_SIZES = (ATTN_W, KV_W, KV_W, IDX_Q_W, IDX_DIM, N_IDX_HEADS, ATTN_W,
          CONV_W, CONV_W, CONV_W, CONV_W, POOL_W, POOL_W, D_MODEL, D_MODEL, D_MODEL)
_OFFS = [0]
for _s in _SIZES:
    _OFFS.append(_OFFS[-1] + _s)
(O_Q, O_K, O_V, O_IQ, O_IK, O_IW, O_GA, O_U, O_BG, O_CG, O_GB, O_PIN, O_GC, O_MA, O_MB, O_MC, O_END) = _OFFS

A_Q, A_K, A_V, A_IQ, A_IKW, A_GA, A_END = 0, 1024, 1280, 1536, 2560, 2688, 3712
B_U, B_BG, B_CG, B_GB, B_PIN, B_GC, B_MB, B_MC, B_END = 0, 512, 1024, 1536, 2048, 2560, 3072, 5120, 7168


def _dot(a, b):
    return jnp.dot(a, b, preferred_element_type=F32)


def _dot_nt(a, b):
    return lax.dot_general(a, b, (((1,), (1,)), ((), ())), preferred_element_type=F32)


def _rms_h(x, g):
    h = x * lax.rsqrt(jnp.mean(x * x, axis=-1, keepdims=True) + EPS) * g
    return h.astype(BF16)


def _silu(x):
    return x * jax.nn.sigmoid(x)


def _const_spec(shape):
    nd = len(shape)
    return pl.BlockSpec(shape, lambda *_: (0,) * nd, pipeline_mode=pl.Buffered(1))


def _prep_weights_kernel(w_ref, wa_ref, wb_ref, wma_ref):
    cols = lambda a, b: w_ref[0, :, a:b].astype(BF16)
    ikw_end = A_IKW + IDX_DIM + N_IDX_HEADS
    wa_ref[:, A_Q:ikw_end] = cols(O_Q, O_GA)
    wa_ref[:, ikw_end:A_GA] = jnp.zeros((wa_ref.shape[0], A_GA - ikw_end), BF16)
    wa_ref[:, A_GA:A_END] = cols(O_GA, O_U)
    wb_ref[:, B_U:B_MB] = cols(O_U, O_MA)
    wb_ref[:, B_MB:B_END] = cols(O_MB, O_END)
    wma_ref[...] = cols(O_MA, O_MB)


def _prep_weights(w_in, layer):
    d, in_w = w_in.shape[1:]
    rows = PREP_ROWS
    slab = lambda w: pl.BlockSpec((rows, w), lambda i: (i, 0))
    return pl.pallas_call(
        _prep_weights_kernel,
        grid=(d // rows,),
        in_specs=[pl.BlockSpec((1, rows, in_w), lambda i: (layer, i, 0))],
        out_specs=(slab(A_END), slab(B_END), slab(D_MODEL)),
        out_shape=(jax.ShapeDtypeStruct((d, A_END), BF16), jax.ShapeDtypeStruct((d, B_END), BF16),
                   jax.ShapeDtypeStruct((d, D_MODEL), BF16)),
        compiler_params=pltpu.CompilerParams(dimension_semantics=("arbitrary",), vmem_limit_bytes=VMEM_LIMIT),
        name="prep_weights",
    )(w_in)


def _proj_a_kernel(x_ref, g_ref, w_ref, c128_ref, s128_ref, c64_ref, s64_ref,
                   q_ref, k_ref, v_ref, kb_ref, vb_ref, iq_ref, ik_ref, ik2_ref, iw_ref, ga_ref):
    h = _rms_h(x_ref[...], g_ref[...])
    c128 = c128_ref[...]
    s128 = s128_ref[...]
    c64 = c64_ref[...]
    s64 = s64_ref[...]
    lane = lax.broadcasted_iota(jnp.int32, (1, LANES), 1)
    first_half64 = (lane % IDX_DIM) < (IDX_DIM // 2)

    def rope128(y):
        return y * c128 + pltpu.roll(y, HEAD_DIM // 2, 1) * s128

    def rope64(y):
        partner = jnp.where(first_half64, pltpu.roll(y, LANES - IDX_DIM // 2, 1),
                            pltpu.roll(y, IDX_DIM // 2, 1))
        return y * c64 + partner * s64

    y = _dot(h, w_ref[:, A_Q:A_K])
    for hd in range(N_HEADS):
        sl = slice(hd * HEAD_DIM, (hd + 1) * HEAD_DIM)
        q_ref[:, sl] = (rope128(y[:, sl]) * (HEAD_DIM ** -0.5 * LOG2E)).astype(BF16)

    y = _dot(h, w_ref[:, A_K:A_IQ])
    for hd in range(N_KV_HEADS):
        sl = slice(hd * HEAD_DIM, (hd + 1) * HEAD_DIM)
        kr = rope128(y[:, sl])
        rows_hd = pl.ds(hd, y.shape[0], stride=N_KV_HEADS)
        k_ref[rows_hd, :] = kr
        kb_ref[:, sl] = kr.astype(BF16)
        v_ref[rows_hd, :] = y[:, KV_W + hd * HEAD_DIM:KV_W + (hd + 1) * HEAD_DIM]
    vb_ref[...] = y[:, KV_W:].astype(BF16)

    y = _dot(h, w_ref[:, A_IQ:A_IKW])
    for c in range(IDX_Q_W // LANES):
        sl = slice(c * LANES, (c + 1) * LANES)
        iq_ref[:, sl] = rope64(y[:, sl]).astype(BF16)

    y = _dot(h, w_ref[:, A_IKW:A_GA])
    ikr = rope64(y)
    ik_ref[...] = ikr[:, :IDX_DIM]
    ikz = jnp.where(lane < IDX_DIM, ikr, 0.0)
    ik2_ref[:, :LANES] = ikz.astype(BF16)
    ik2_ref[:, LANES:] = pltpu.roll(ikz, IDX_DIM, 1).astype(BF16)
    iw_ref[...] = y[:, IDX_DIM:IDX_DIM + N_IDX_HEADS] * ((IDX_DIM ** -0.5) * (N_IDX_HEADS ** -0.5))

    ga_ref[...] = _silu(_dot(h, w_ref[:, A_GA:A_END])).astype(BF16)


def _proj_a(x, g, w_a, tabs, tm):
    m = x.shape[0]
    c128, s128, c64, s64 = tabs
    n_pt = c128.shape[0] // tm
    row = lambda w: pl.BlockSpec((tm, w), lambda i: (i, 0))
    tab = pl.BlockSpec((tm, LANES), lambda i: (i % n_pt, 0))
    out_shape = (
        jax.ShapeDtypeStruct((m, ATTN_W), BF16),
        jax.ShapeDtypeStruct((m * N_KV_HEADS, HEAD_DIM), F32),
        jax.ShapeDtypeStruct((m * N_KV_HEADS, HEAD_DIM), F32),
        jax.ShapeDtypeStruct((m, KV_W), BF16),
        jax.ShapeDtypeStruct((m, KV_W), BF16),
        jax.ShapeDtypeStruct((m, IDX_Q_W), BF16),
        jax.ShapeDtypeStruct((m, IDX_DIM), F32),
        jax.ShapeDtypeStruct((m, 2 * LANES), BF16),
        jax.ShapeDtypeStruct((m, N_IDX_HEADS), F32),
        jax.ShapeDtypeStruct((m, ATTN_W), BF16),
    )
    kv_heads = pl.BlockSpec((tm * N_KV_HEADS, HEAD_DIM), lambda i: (i, 0))
    out_specs = (row(ATTN_W), kv_heads, kv_heads, row(KV_W), row(KV_W), row(IDX_Q_W),
                 row(IDX_DIM), row(2 * LANES), row(N_IDX_HEADS), row(ATTN_W))
    return pl.pallas_call(
        _proj_a_kernel,
        grid=(m // tm,),
        in_specs=[row(D_MODEL), _const_spec((1, D_MODEL)), _const_spec((D_MODEL, A_END)),
                  tab, tab, tab, tab],
        out_specs=out_specs,
        out_shape=out_shape,
        compiler_params=pltpu.CompilerParams(dimension_semantics=("arbitrary",),
                                             vmem_limit_bytes=VMEM_LIMIT),
        name="proj_a",
    )(x, g, w_a, c128, s128, c64, s64)


def _grouped_blocks(n_blocks, trip, group):
    def full(t, carry):
        trip(group * t, group)
        return carry

    n_full = n_blocks // group
    lax.fori_loop(0, n_full, full, 0)
    done = n_full * group
    size = group // 2
    while size >= 1:
        has = ((n_blocks - done) // size) % 2 == 1

        @pl.when(has)
        def _(done=done, size=size):
            trip(done, size)

        done = done + jnp.where(has, size, 0)
        size //= 2


def _attend_kernel(q_ref, iq_ref, iwt_ref, kb_ref, vb_ref, ik2_ref, o_ref,
                   st_ref, sbuf_ref, state_ref, acc_ref, stat_ref,
                   *, tq, n_pack, tq_real, q0, topk):
    kb_sz = KEY_BLOCK
    tqe = tq // n_pack
    i = pl.program_id(1)
    qpos0 = q0 + i * tq_real
    n_keys = ((qpos0 + tq_real - 1) // CHUNK + 1) * CHUNK
    nkb = (n_keys + kb_sz - 1) // kb_sz

    lane_q = lax.broadcasted_iota(jnp.int32, (1, tq), 1)
    lane_e = lane_q % tqe
    n_adm_i = ((qpos0 + lane_e) // CHUNK + 1) * CHUNK
    iw = iwt_ref[0]

    def query_rows(ref, cols):
        if n_pack == 1:
            return ref[0, :, cols]
        zero = jnp.zeros((tqe, cols.stop - cols.start), ref.dtype)
        return jnp.concatenate(
            [jnp.concatenate([ref[e, :, cols] if c == e else zero for c in range(n_pack)], axis=1)
             for e in range(n_pack)], axis=0)

    def key_rows(ref, r0, cols):
        if n_pack == 1:
            return ref[0, pl.ds(r0, kb_sz), cols]
        return jnp.concatenate([ref[e, pl.ds(r0, kb_sz), cols] for e in range(n_pack)], axis=1)

    state_ref[0:1, :] = jnp.full((1, tq), jnp.inf, F32)
    state_ref[1:2, :] = jnp.full((1, tq), -jnp.inf, F32)

    def score_trip(kb0, n_blk):
        mn, mx = state_ref[0:1, :], state_ref[1:2, :]
        for kb in [kb0 + d for d in range(n_blk)]:
            r0 = pl.multiple_of(kb * kb_sz, kb_sz)
            ik_even = key_rows(ik2_ref, r0, slice(0, LANES))
            ik_odd = key_rows(ik2_ref, r0, slice(LANES, 2 * LANES))
            acc = jnp.zeros((kb_sz, tq), F32)
            for p in range(N_IDX_HEADS // 2):
                iq_pair = query_rows(iq_ref, slice(p * LANES, (p + 1) * LANES))
                d0 = _dot_nt(ik_even, iq_pair)
                d1 = _dot_nt(ik_odd, iq_pair)
                acc = acc + jnp.maximum(d0, 0.0) * iw[2 * p:2 * p + 1]
                acc = acc + jnp.maximum(d1, 0.0) * iw[2 * p + 1:2 * p + 2]
            key = r0 + lax.broadcasted_iota(jnp.int32, (kb_sz, 1), 0)
            masked = jnp.where(key < n_adm_i, acc, -jnp.inf)
            st_ref[pl.ds(r0, kb_sz), :] = masked
            mx = jnp.maximum(mx, jnp.max(masked, axis=0, keepdims=True))
            mn = jnp.minimum(mn, jnp.min(acc, axis=0, keepdims=True))
        state_ref[0:1, :] = mn
        state_ref[1:2, :] = mx

    _grouped_blocks(nkb, score_trip, SCORE_GROUP)
    mn, mx = state_ref[0:1, :], state_ref[1:2, :]

    n_adm = n_adm_i.astype(F32)
    topk_f = float(topk)
    active0 = jnp.logical_and(n_adm > topk_f, lane_e < tq_real).astype(F32)
    state_ref[0:1, :] = mn
    state_ref[1:2, :] = mx + jnp.maximum(jnp.abs(mx), 1e-30)
    state_ref[2:3, :] = jnp.full((1, tq), F32_LOWEST, F32)
    state_ref[3:4, :] = active0
    state_ref[4:5, :] = n_adm
    state_ref[5:6, :] = jnp.zeros((1, tq), F32)
    state_ref[6:7, :] = jnp.zeros((1, tq), F32)

    def count_ge(mid):
        sub = kb_sz // COUNT_CHAINS

        def cnt_blk(kb, accs):
            r0 = pl.multiple_of(kb * kb_sz, kb_sz)
            out = []
            for c in range(COUNT_CHAINS):
                ge = (st_ref[pl.ds(r0 + c * sub, sub), :] >= mid).astype(F32)
                out.append(accs[c] + jnp.sum(ge.reshape(sub // 8, 8, tq), axis=0))
            return tuple(out)

        accs = lax.fori_loop(0, nkb, cnt_blk, tuple(jnp.zeros((8, tq), F32) for _ in range(COUNT_CHAINS)))
        return jnp.sum(sum(accs), axis=0, keepdims=True)

    def bisect_step(st):
        lo, hi, thr, active, clo, tie, pend = st
        mid = 0.5 * lo + 0.5 * hi
        stuck = jnp.logical_or(mid <= lo, mid >= hi)
        cnt = count_ge(mid)
        ge_k = cnt >= topk_f
        exact = cnt == topk_f
        moving = jnp.logical_and(active, jnp.logical_not(stuck))
        ended = jnp.logical_and(active, stuck)
        thr = jnp.where(ended, lo, jnp.where(jnp.logical_and(moving, exact), mid, thr))
        tie = jnp.logical_or(tie, jnp.logical_and(ended, clo > topk_f))
        up = jnp.logical_and(moving, ge_k)
        down = jnp.logical_and(moving, jnp.logical_not(ge_k))
        near = jnp.logical_and(moving, cnt == topk_f - 1.0)
        done = jnp.logical_or(exact, near)
        return (jnp.where(up, mid, lo), jnp.where(down, mid, hi), thr,
                jnp.logical_and(moving, jnp.logical_not(done)), jnp.where(up, cnt, clo), tie,
                jnp.logical_or(pend, near))

    def bisect_steps():
        st = (state_ref[0:1, :], state_ref[1:2, :], state_ref[2:3, :], state_ref[3:4, :] > 0.0,
              state_ref[4:5, :], state_ref[5:6, :] > 0.0, state_ref[6:7, :] > 0.0)
        for _ in range(BISECT_UNROLL):
            st = bisect_step(st)
        active_f = st[3].astype(F32)
        state_ref[0:1, :] = st[0]
        state_ref[1:2, :] = st[1]
        state_ref[2:3, :] = st[2]
        state_ref[3:4, :] = active_f
        state_ref[4:5, :] = st[4]
        state_ref[5:6, :] = st[5].astype(F32)
        state_ref[6:7, :] = st[6].astype(F32)
        return active_f

    def unchecked(i, carry):
        bisect_steps()
        return carry

    lax.fori_loop(0, BISECT_UNCHECKED, unchecked, 0)
    lax.while_loop(lambda c: jnp.logical_and(c[0] > 0.0, c[1] < 128),
                   lambda c: (jnp.max(bisect_steps()), c[1] + 1),
                   (jnp.max(state_ref[3:4, :]), jnp.int32(0)))

    @pl.when(jnp.max(state_ref[6:7, :]) > 0.0)
    def _():
        pend = state_ref[6:7, :] > 0.0
        hi = state_ref[1:2, :]

        def below_blk(kb, acc):
            r0 = pl.multiple_of(kb * kb_sz, kb_sz)
            blk = st_ref[pl.ds(r0, kb_sz), :]
            below = jnp.where(blk < hi, blk, -jnp.inf)
            return jnp.maximum(acc, jnp.max(below.reshape(COUNT_CHAINS, kb_sz // (8 * COUNT_CHAINS), 8, tq), axis=1))

        top = lax.fori_loop(0, nkb, below_blk, jnp.full((COUNT_CHAINS, 8, tq), -jnp.inf, F32))
        thr_p = jnp.where(pend, jnp.max(jnp.max(top, axis=0), axis=0, keepdims=True), state_ref[2:3, :])
        state_ref[2:3, :] = thr_p
        dup = jnp.logical_and(pend, count_ge(thr_p) > topk_f)
        state_ref[5:6, :] = jnp.maximum(state_ref[5:6, :], dup.astype(F32))

    thr = state_ref[2:3, :]

    @pl.when(jnp.max(state_ref[5:6, :]) > 0.0)
    def _():
        tie_row = state_ref[5:6, :] > 0.0

        def gt_blk(kb, acc):
            r0 = pl.multiple_of(kb * kb_sz, kb_sz)
            return acc + jnp.sum((st_ref[pl.ds(r0, kb_sz), :] > thr).astype(F32), axis=0, keepdims=True)

        need = topk_f - lax.fori_loop(0, nkb, gt_blk, jnp.zeros((1, tq), F32))
        tri = (lax.broadcasted_iota(jnp.int32, (kb_sz, kb_sz), 0)
               >= lax.broadcasted_iota(jnp.int32, (kb_sz, kb_sz), 1)).astype(BF16)

        def fix_blk(kb, run):
            r0 = pl.multiple_of(kb * kb_sz, kb_sz)
            blk = st_ref[pl.ds(r0, kb_sz), :]
            eq = jnp.logical_and(blk == thr, tie_row)
            eq_f = eq.astype(F32)
            rank = _dot(tri, eq_f.astype(BF16)) - eq_f + run
            st_ref[pl.ds(r0, kb_sz), :] = jnp.where(jnp.logical_and(eq, rank >= need), -jnp.inf, blk)
            return run + jnp.sum(eq_f, axis=0, keepdims=True)

        lax.fori_loop(0, nkb, fix_blk, jnp.zeros((1, tq), F32))

    red = lambda a, op: op(a.reshape(2, kb_sz // 16, 8, a.shape[-1]), axis=1)
    gw = GROUP * tq
    heads_per_trip = LOGIT_LANES // gw
    entry_of_lane = (lax.broadcasted_iota(jnp.int32, (1, gw), 1) % tq) // tqe
    for j0 in range(0, N_KV_HEADS, heads_per_trip):
        trip_heads = range(j0, j0 + heads_per_trip)
        qs = {j: jnp.concatenate(
            [query_rows(q_ref, slice((GROUP * j + g) * HEAD_DIM, (GROUP * j + g + 1) * HEAD_DIM))
             for g in range(GROUP)], axis=0) for j in trip_heads}

        def qk_trip(kb0, n_blk):
            m8 = stat_ref[...]
            for kb in [kb0 + d for d in range(n_blk)]:
                r0 = pl.multiple_of(kb * kb_sz, kb_sz)
                bias = jnp.where(st_ref[pl.ds(r0, kb_sz), :] >= thr, 0.0, NEG_BIAS)
                parts = []
                for j in trip_heads:
                    s = _dot_nt(key_rows(kb_ref, r0, slice(j * HEAD_DIM, (j + 1) * HEAD_DIM)), qs[j])
                    for g in range(GROUP):
                        sg = s[:, g * tq:(g + 1) * tq] + bias
                        c0 = (j - j0) * gw + g * tq
                        sbuf_ref[kb, :, c0:c0 + tq] = sg
                        parts.append(red(sg, jnp.max))
                m8 = jnp.maximum(m8, jnp.concatenate(parts, axis=-1))
            stat_ref[...] = m8

        stat_ref[...] = jnp.full(stat_ref.shape, NEG_BIAS, F32)
        _grouped_blocks(nkb, qk_trip, PASS_GROUP)
        m = jnp.max(jnp.max(stat_ref[...], axis=0), axis=0, keepdims=True)
        stat_ref[...] = jnp.zeros(stat_ref.shape, F32)
        acc_ref[...] = jnp.zeros(acc_ref.shape, F32)

        def pv_trip(kb0, n_blk):
            sums = []
            for j in trip_heads:
                cols = slice((j - j0) * gw, (j - j0 + 1) * gw)
                pv, psum = 0.0, 0.0
                for kb in [kb0 + d for d in range(n_blk)]:
                    r0 = pl.multiple_of(kb * kb_sz, kb_sz)
                    p = jnp.exp2(sbuf_ref[kb, :, cols] - m[:, cols])
                    p16 = p.astype(BF16)
                    t = None
                    for e in range(n_pack):
                        te = lax.dot_general(vb_ref[e, pl.ds(r0, kb_sz), j * HEAD_DIM:(j + 1) * HEAD_DIM], p16,
                                             (((0,), (0,)), ((), ())), preferred_element_type=F32)
                        t = te if t is None else jnp.where(entry_of_lane == e, te, t)
                    pv = pv + t
                    psum = psum + red(p, jnp.sum)
                acc_ref[:, cols] += pv
                sums.append(psum)
            stat_ref[...] += jnp.concatenate(sums, axis=-1)

        _grouped_blocks(nkb, pv_trip, PASS_GROUP)
        o = acc_ref[...] / jnp.sum(jnp.sum(stat_ref[...], axis=0), axis=0, keepdims=True)
        for j in trip_heads:
            for g in range(GROUP):
                hd = GROUP * j + g
                c0 = (j - j0) * gw + g * tq
                o_t = o[:, c0:c0 + tq].T.astype(o_ref.dtype)
                for e in range(n_pack):
                    o_ref[e, :, hd * HEAD_DIM:(hd + 1) * HEAD_DIM] = o_t[e * tqe:(e + 1) * tqe]


def _attend(q, iq, iwt, kb, vb, ik2, *, tq, n_pack, tq_real, q0, topk):
    b, t_q, _ = q.shape
    l_keys = kb.shape[1]
    tqe = tq // n_pack
    assert b % n_pack == 0 and t_q % tqe == 0 and (n_pack == 1 or t_q == tqe) and l_keys % KEY_BLOCK == 0
    n_kb = l_keys // KEY_BLOCK
    qspec = pl.BlockSpec((n_pack, tqe, ATTN_W), lambda bi, i: (bi, i, 0))
    kspec = pl.BlockSpec((n_pack, l_keys, KV_W), lambda bi, i: (bi, 0, 0))
    return pl.pallas_call(
        functools.partial(_attend_kernel, tq=tq, n_pack=n_pack, tq_real=tq_real, q0=q0, topk=topk),
        grid=(b // n_pack, t_q // tqe),
        in_specs=[qspec, qspec, pl.BlockSpec((1, N_IDX_HEADS, tq), lambda bi, i: (bi, 0, i)),
                  kspec, kspec, kspec],
        out_specs=qspec,
        out_shape=jax.ShapeDtypeStruct((b, t_q, ATTN_W), F32),
        scratch_shapes=[
            pltpu.VMEM((l_keys, tq), F32),
            pltpu.VMEM((n_kb, KEY_BLOCK, LOGIT_LANES), F32),
            pltpu.VMEM((8, tq), F32),
            pltpu.VMEM((HEAD_DIM, LOGIT_LANES), F32),
            pltpu.VMEM((2, 8, LOGIT_LANES), F32),
        ],
        compiler_params=pltpu.CompilerParams(dimension_semantics=("arbitrary", "arbitrary"),
                                             vmem_limit_bytes=VMEM_LIMIT),
        name="attend",
    )(q, iq, iwt, kb, vb, ik2)


def _pack_cache_kernel(c_ref, o_ref, *, past):
    for hd in range(N_KV_HEADS):
        rows_hd = pl.ds(hd, past, stride=N_KV_HEADS)
        o_ref[0, 0:past, hd * HEAD_DIM:(hd + 1) * HEAD_DIM] = c_ref[0, 0, rows_hd, :].astype(BF16)
    o_ref[0, past:, :] = jnp.zeros((o_ref.shape[1] - past, KV_W), BF16)


def _pack_idx_cache_kernel(c_ref, e_ref, o_ref, *, past):
    o_ref[0, 0:past, :] = _dot(c_ref[0, 0].astype(BF16), e_ref[...]).astype(BF16)
    o_ref[0, past:, :] = jnp.zeros((o_ref.shape[1] - past, 2 * LANES), BF16)


def _pack_cache(c, layer, l_pad):
    depth, b, past = c.shape[:3]
    c = c.reshape(depth, b, past * N_KV_HEADS, HEAD_DIM)
    return pl.pallas_call(
        functools.partial(_pack_cache_kernel, past=past),
        grid=(b,),
        in_specs=[pl.BlockSpec((1, 1, past * N_KV_HEADS, HEAD_DIM), lambda bi: (layer, bi, 0, 0))],
        out_specs=pl.BlockSpec((1, l_pad, KV_W), lambda bi: (bi, 0, 0)),
        out_shape=jax.ShapeDtypeStruct((b, l_pad, KV_W), BF16),
        compiler_params=pltpu.CompilerParams(dimension_semantics=("arbitrary",), vmem_limit_bytes=VMEM_LIMIT),
        name="pack_cache",
    )(c)


def _pack_idx_cache(c, layer, l_pad):
    _, b, past, _ = c.shape
    eye = jnp.eye(IDX_DIM, dtype=BF16)
    zero = jnp.zeros((IDX_DIM, IDX_DIM), BF16)
    select = jnp.concatenate([eye, zero, zero, eye], axis=1)
    return pl.pallas_call(
        functools.partial(_pack_idx_cache_kernel, past=past),
        grid=(b,),
        in_specs=[pl.BlockSpec((1, 1, past, IDX_DIM), lambda bi: (layer, bi, 0, 0)),
                  pl.BlockSpec((IDX_DIM, 2 * LANES), lambda bi: (0, 0))],
        out_specs=pl.BlockSpec((1, l_pad, 2 * LANES), lambda bi: (bi, 0, 0)),
        out_shape=jax.ShapeDtypeStruct((b, l_pad, 2 * LANES), BF16),
        compiler_params=pltpu.CompilerParams(dimension_semantics=("arbitrary",), vmem_limit_bytes=VMEM_LIMIT),
        name="pack_idx_cache",
    )(c, select)


def _proj_b_kernel(x_ref, g_ref, w_ref, convw_ref, convb_ref, poolw_ref, pscale_ref, liftb_ref, liftc_ref,
                   chist_ref, phist_ref, zbc_ref, cstate_ref, pstate_ref, cin_ext, pin_ext,
                   *, n_sub, ls, pos0):
    j = pl.program_id(1)

    @pl.when(j == 0)
    def _():
        cin_ext[:, 0:CONV_PAD, :] = chist_ref[...]
        pin_ext[:, 0:POOL_PAD, :] = phist_ref[...]

    @pl.when(j > 0)
    def _():
        cin_ext[:, 0:CONV_PAD, :] = cin_ext[:, ls:ls + CONV_PAD, :]
        pin_ext[:, 0:POOL_PAD, :] = pin_ext[:, ls:ls + POOL_PAD, :]

    h = _rms_h(x_ref[...], g_ref[...])
    rows = lambda a: jnp.concatenate(a, axis=0) if n_sub > 1 else a[0]

    y = _dot(h, w_ref[:, B_U:B_PIN])
    u = y[:, 0:CONV_W]
    b_gate = y[:, CONV_W:2 * CONV_W]
    c_gate = y[:, 2 * CONV_W:3 * CONV_W]
    gate_b = y[:, 3 * CONV_W:4 * CONV_W]
    cin = c_gate * u
    for s in range(n_sub):
        cin_ext[s, CONV_PAD:CONV_PAD + ls, :] = cin[s * ls:(s + 1) * ls]
    conv = (rows([cin_ext[s, CONV_PAD - 2:CONV_PAD - 2 + ls, :] for s in range(n_sub)]) * convw_ref[0:1, :]
            + rows([cin_ext[s, CONV_PAD - 1:CONV_PAD - 1 + ls, :] for s in range(n_sub)]) * convw_ref[1:2, :]
            + cin * convw_ref[2:3, :] + convb_ref[...])
    y_b = b_gate * conv * _silu(gate_b)
    zb = _dot(y_b.astype(BF16), liftb_ref[...])

    y = _dot(h, w_ref[:, B_PIN:B_MB])
    p_in = y[:, 0:POOL_W]
    gate_c = y[:, POOL_W:2 * POOL_W]
    for s in range(n_sub):
        pin_ext[s, POOL_PAD:POOL_PAD + ls, :] = p_in[s * ls:(s + 1) * ls]
    pos = rows([pos0 + j * ls + lax.broadcasted_iota(jnp.int32, (ls, 1), 0)] * n_sub)
    yc_parts = []
    for gi, win in enumerate(POOL_WINDOWS):
        sl = slice(gi * POOL_GROUP, (gi + 1) * POOL_GROUP)
        tot = p_in[:, sl]
        for back in range(1, win):
            tot = tot + rows([pin_ext[s, POOL_PAD - back:POOL_PAD - back + ls, sl] for s in range(n_sub)])
        cnt = jnp.minimum(win, pos + 1).astype(F32)
        d = tot / cnt - p_in[:, sl]
        mixed = _dot(d.astype(BF16), poolw_ref[gi])
        yc_parts.append(mixed * pscale_ref[:, sl] * _silu(gate_c[:, sl]))
    y_c = jnp.concatenate(yc_parts, axis=-1)
    zc = _dot(y_c.astype(BF16), liftc_ref[...])

    m_b = _dot(h, w_ref[:, B_MB:B_MC])
    m_c = _dot(h, w_ref[:, B_MC:B_END])
    zbc_ref[...] = jax.nn.sigmoid(m_b) * zb + jax.nn.sigmoid(m_c) * zc

    @pl.when(j == pl.num_programs(1) - 1)
    def _():
        cstate_ref[...] = cin_ext[:, ls:ls + CONV_PAD, :]
        pstate_ref[...] = pin_ext[:, ls:ls + POOL_PAD, :]


def _proj_b(x, g, w_b, conv_w, conv_b, pool_w, pool_scale, lift_b, lift_c, chist, phist, *, seq_len, tm, pos0):
    m = x.shape[0]
    n_seq = m // seq_len
    n_sub, ls = (1, tm) if tm <= seq_len else (tm // seq_len, seq_len)
    n_t = seq_len // ls
    out_shape = (
        jax.ShapeDtypeStruct((m, D_MODEL), F32),
        jax.ShapeDtypeStruct((n_seq, CONV_PAD, CONV_W), F32),
        jax.ShapeDtypeStruct((n_seq, POOL_PAD, POOL_W), F32),
    )
    hist = lambda r, w: pl.BlockSpec((n_sub, r, w), lambda s, j: (s, 0, 0))
    return pl.pallas_call(
        functools.partial(_proj_b_kernel, n_sub=n_sub, ls=ls, pos0=pos0),
        grid=(n_seq // n_sub, n_t),
        in_specs=[pl.BlockSpec((tm, D_MODEL), lambda s, j: (s * n_t + j, 0)),
                  _const_spec((1, D_MODEL)), _const_spec((D_MODEL, B_END)),
                  _const_spec((CONV_K, CONV_W)), _const_spec((1, CONV_W)),
                  _const_spec((len(POOL_WINDOWS), POOL_GROUP, POOL_GROUP)), _const_spec((1, POOL_W)),
                  _const_spec((CONV_W, D_MODEL)), _const_spec((POOL_W, D_MODEL)),
                  hist(CONV_PAD, CONV_W), hist(POOL_PAD, POOL_W)],
        out_specs=(pl.BlockSpec((tm, D_MODEL), lambda s, j: (s * n_t + j, 0)),
                   hist(CONV_PAD, CONV_W), hist(POOL_PAD, POOL_W)),
        out_shape=out_shape,
        scratch_shapes=[pltpu.VMEM((n_sub, ls + CONV_PAD, CONV_W), F32),
                        pltpu.VMEM((n_sub, ls + POOL_PAD, POOL_W), F32)],
        compiler_params=pltpu.CompilerParams(dimension_semantics=("arbitrary", "arbitrary"),
                                             vmem_limit_bytes=VMEM_LIMIT),
        name="proj_b",
    )(x, g, w_b, conv_w, conv_b, pool_w, pool_scale, lift_b, lift_c, chist, phist)


def _merge_kernel(x_ref, g_ref, wma_ref, attn_ref, ga_ref, lifta_ref, zbc_ref, wout_ref, fg_ref,
                  out_ref, *, final):
    x = x_ref[...]
    h = _rms_h(x, g_ref[...])
    y_a = attn_ref[...] * ga_ref[...].astype(F32)
    z = jax.nn.sigmoid(_dot(h, wma_ref[...])) * _dot(y_a.astype(BF16), lifta_ref[...]) + zbc_ref[...]
    out = x + _dot(z.astype(BF16), wout_ref[...])
    if final:
        out = out * lax.rsqrt(jnp.mean(out * out, axis=-1, keepdims=True) + EPS) * fg_ref[...]
    out_ref[...] = out


def _merge(x, g, w_ma, attn, ga, lift_a, zbc, w_out, final_g, *, tm, final):
    m = x.shape[0]
    row = lambda w: pl.BlockSpec((tm, w), lambda i: (i, 0))
    return pl.pallas_call(
        functools.partial(_merge_kernel, final=final),
        grid=(m // tm,),
        in_specs=[row(D_MODEL), _const_spec((1, D_MODEL)), _const_spec((D_MODEL, D_MODEL)),
                  row(ATTN_W), row(ATTN_W), _const_spec((ATTN_W, D_MODEL)), row(D_MODEL),
                  _const_spec((D_MODEL, D_MODEL)), _const_spec((1, D_MODEL))],
        out_specs=row(D_MODEL),
        out_shape=jax.ShapeDtypeStruct((m, D_MODEL), F32),
        compiler_params=pltpu.CompilerParams(dimension_semantics=("arbitrary",),
                                             vmem_limit_bytes=VMEM_LIMIT),
        name="merge",
    )(x, g, w_ma, attn, ga, lift_a, zbc, w_out, final_g)


def _rope_tables(pos):
    def tab(half, reps):
        inv = ROPE_THETA ** (-jnp.arange(half, dtype=F32) / half)
        ang = pos.astype(F32)[:, None] * inv[None, :]
        cos, sin = jnp.cos(ang), jnp.sin(ang)
        return (jnp.tile(jnp.concatenate([cos, cos], axis=-1), (1, reps)),
                jnp.tile(jnp.concatenate([-sin, sin], axis=-1), (1, reps)))
    c128, s128 = tab(HEAD_DIM // 2, 1)
    c64, s64 = tab(IDX_DIM // 2, LANES // IDX_DIM)
    return c128, s128, c64, s64


def _layer(x, tabs, lw, hist, cache, *, n_seq, seq_len, pos0, topk, tm_tok, tm_seq, tq, final, final_g):
    m = n_seq * seq_len
    q, k, v, kb, vb, iq, ik, ik2, iw, ga = _proj_a(x, lw["g"], lw["w_a"], tabs, tm_tok)

    def seq(a):
        return a.reshape(n_seq, seq_len, a.shape[-1])

    iwt = jnp.swapaxes(seq(iw), 1, 2)
    q3, iq3, kb3, vb3, ik23 = seq(q), seq(iq), seq(kb), seq(vb), seq(ik2)
    if cache is not None:
        ck, cv, cik, layer = cache
        past = ck.shape[2]
        l_pad = past + seq_len + (-(past + seq_len) % KEY_BLOCK)
        kb3 = lax.dynamic_update_slice(_pack_cache(ck, layer, l_pad), kb3, (0, past, 0))
        vb3 = lax.dynamic_update_slice(_pack_cache(cv, layer, l_pad), vb3, (0, past, 0))
        ik23 = lax.dynamic_update_slice(_pack_idx_cache(cik, layer, l_pad), ik23, (0, past, 0))
    n_pack = tq // seq_len if seq_len < tq else 1
    assert tq == n_pack * min(seq_len, tq) and n_seq % n_pack == 0
    if n_pack > 1:
        iwt = jnp.swapaxes(iwt.reshape(n_seq // n_pack, n_pack, N_IDX_HEADS, seq_len), 1, 2)
        iwt = iwt.reshape(n_seq // n_pack, N_IDX_HEADS, tq)
    attn = _attend(q3, iq3, iwt, kb3, vb3, ik23, tq=tq, n_pack=n_pack, tq_real=min(seq_len, tq), q0=pos0, topk=topk)
    attn = attn.reshape(m, ATTN_W)

    zbc, cstate, pstate = _proj_b(x, lw["g"], lw["w_b"], lw["conv_w"], lw["conv_b"], lw["pool_w"],
                                  lw["pool_scale"], lw["lift_b"], lw["lift_c"], hist[0], hist[1],
                                  seq_len=seq_len, tm=tm_seq, pos0=pos0)
    outs = _merge(x, lw["g"], lw["w_ma"], attn, ga, lw["lift_a"], zbc, lw["w_out"], final_g,
                  tm=tm_tok, final=final)
    states = (k.reshape(n_seq, seq_len, N_KV_HEADS, HEAD_DIM), v.reshape(n_seq, seq_len, N_KV_HEADS, HEAD_DIM),
              ik.reshape(n_seq, seq_len, IDX_DIM), cstate[:, CONV_PAD - (CONV_K - 1):], pstate[:, POOL_PAD - POOL_HIST:])
    return outs, states


def kernel(x_prompt, x_sample, cache_k, cache_v, cache_idx_k, state_conv, state_pool, norm_g, w_in, conv_w,
           conv_b, pool_w, pool_scale, lift_a, lift_b, lift_c, w_out, final_norm_g):
    batch, seq, _ = x_prompt.shape
    dec_batch, dec_seq, _ = x_sample.shape
    depth = w_in.shape[0]
    past_len = cache_k.shape[2]
    topk_prompt = min(MAX_TOPK, seq // 4)
    topk_sample = min(MAX_TOPK, (past_len + dec_seq) // 4)

    tm_p, tm_s = 256, 256
    tabs_p = _rope_tables(jnp.arange(seq, dtype=jnp.int32))
    tabs_s = _rope_tables(past_len + (jnp.arange(tm_s, dtype=jnp.int32) % dec_seq))
    final_g = final_norm_g.reshape(1, D_MODEL)

    hp = x_prompt.reshape(batch * seq, D_MODEL)
    hs = x_sample.reshape(dec_batch * dec_seq, D_MODEL)
    zero_hist = (jnp.zeros((batch, CONV_PAD, CONV_W), F32), jnp.zeros((batch, POOL_PAD, POOL_W), F32))
    p_states, s_states = [], []
    for l in range(depth):
        w_a, w_b, w_ma = _prep_weights(w_in, l)
        lw = {
            "g": norm_g[l].reshape(1, D_MODEL),
            "w_a": w_a, "w_b": w_b, "w_ma": w_ma,
            "conv_w": conv_w[l], "conv_b": conv_b[l].reshape(1, CONV_W),
            "pool_w": pool_w[l].astype(BF16), "pool_scale": pool_scale[l].reshape(1, POOL_W),
            "lift_a": lift_a[l].astype(BF16), "lift_b": lift_b[l].astype(BF16), "lift_c": lift_c[l].astype(BF16),
            "w_out": w_out[l].astype(BF16),
        }
        final = l == depth - 1
        outs_p, st_p = _layer(hp, tabs_p, lw, zero_hist, None, n_seq=batch, seq_len=seq, pos0=0,
                              topk=topk_prompt, tm_tok=tm_p, tm_seq=tm_p, tq=Q_TILE_PROMPT, final=final,
                              final_g=final_g)
        cache = (cache_k, cache_v, cache_idx_k, l)
        hist_s = (jnp.pad(state_conv[l], ((0, 0), (CONV_PAD - (CONV_K - 1), 0), (0, 0))),
                  jnp.pad(state_pool[l], ((0, 0), (POOL_PAD - POOL_HIST, 0), (0, 0))))
        outs_s, st_s = _layer(hs, tabs_s, lw, hist_s, cache, n_seq=dec_batch, seq_len=dec_seq, pos0=past_len,
                              topk=topk_sample, tm_tok=tm_s, tm_seq=tm_s, tq=Q_TILE_SAMPLE, final=final,
                              final_g=final_g)
        hp, hs = outs_p, outs_s
        p_states.append(st_p)
        s_states.append(st_s)

    stack = lambda sts, i: jnp.stack([st[i] for st in sts])
    return (hp.reshape(batch, seq, D_MODEL), hs.reshape(dec_batch, dec_seq, D_MODEL),
            stack(p_states, 0), stack(p_states, 1), stack(p_states, 2), stack(p_states, 3), stack(p_states, 4),
            stack(s_states, 0), stack(s_states, 1), stack(s_states, 2), stack(s_states, 3), stack(s_states, 4))
```
